```python
import math
import jax, jax.numpy as jnp
from jax import lax
import numpy as np

D_MODEL = 1024
BATCH = 8
SEQ = 2048
DEPTH = 4
DEC_BATCH = 128
DEC_SEQ = 8
PAST_LEN = 8192
PAGE_SIZE = 128

N_MIXERS = 4
NL_CONV = (DEPTH + 3) // 4
NL_SWA = (DEPTH + 2) // 4
NL_POOL = (DEPTH + 1) // 4
NL_MLA = DEPTH // 4

D_FF = 4 * D_MODEL
CONV_WIDTH = 3

SWA_GROUPS = ((128, 1), (512, 4), (2048, 16))
N_SWA_GROUPS = 3
SWA_HEADS = 8
SWA_HEAD_DIM = 64
SWA_WIDTH = SWA_HEADS * SWA_HEAD_DIM
SWA_BLK = 128
SWA_SCALE = SWA_HEAD_DIM ** -0.5

POOL_WINDOWS = (2, 4, 8, 16)
POOL_GROUP = D_MODEL // 4
POOL_HIST = 15

MLA_HEADS = 8
Q_LORA = 384
KV_LORA = 256
QK_NOPE = 128
QK_ROPE = 64
V_DIM = 128
MLA_BLK = 128
MLA_SCALE = (QK_NOPE + QK_ROPE) ** -0.5
ROPE_BASE = 10000.0

EPS = 1e-6
NEG = -1e30

kernel_name = 'hybrid_conv_dilswa_pool_mla_adaln_step'


def rms_norm(x, g):
    xf = x.astype(jnp.float32)
    y = xf * lax.rsqrt(jnp.mean(xf * xf, axis=-1, keepdims=True) + EPS)
    return (y * g.astype(jnp.float32)).astype(x.dtype)


def adaln(c, w, b):
    mod = jax.nn.silu(c) @ w + b
    return jnp.split(mod[:, None, :], 6, axis=-1)


def modulate(x, g, shift, scale):
    return rms_norm(x, g) * (1 + scale) + shift


def sq_relu_mlp(h, w1, w2):
    return jnp.square(jax.nn.relu(h @ w1)) @ w2


def conv_mixer(h, buf, w_in, k, w_out):
    b_gate, c_gate, u = jnp.split(h @ w_in, 3, axis=-1)
    u_ext = jnp.concatenate([buf.astype(h.dtype), c_gate * u], axis=1)
    v = lax.conv_general_dilated(u_ext, k[:, None, :].astype(h.dtype), (1,), 'VALID',
                                 dimension_numbers=('NWC', 'WIO', 'NWC'),
                                 feature_group_count=D_MODEL)
    return (b_gate * v) @ w_out, u_ext[:, -(CONV_WIDTH - 1):]


def swa_project(h, w_qkv, qn_g, kn_g):
    n, l, _ = h.shape
    qkv = (h @ w_qkv).reshape(n, l, 3, N_SWA_GROUPS, SWA_HEADS, SWA_HEAD_DIM)
    q = rms_norm(qkv[:, :, 0], qn_g[:, None, :])
    k = rms_norm(qkv[:, :, 1], kn_g[:, None, :])
    return q, k, qkv[:, :, 2]


def dilate(a, r):
    b, t = a.shape[:2]
    return a.reshape(b, t // r, r, *a.shape[2:]).swapaxes(1, 2).reshape(b * r, t // r, *a.shape[2:])


def undilate(a, r, b):
    l = a.shape[1]
    return a.reshape(b, r, l, *a.shape[2:]).swapaxes(1, 2).reshape(b, l * r, *a.shape[2:])


def band_attention(q, k, v, span):
    n, l, h, d = q.shape
    nb = -(-l // SWA_BLK)
    pad = nb * SWA_BLK - l
    blk = lambda a: jnp.pad(a, ((0, 0), (0, pad), (0, 0), (0, 0))).reshape(n, nb, SWA_BLK, h, d)
    with_prev = lambda a: jnp.concatenate([jnp.concatenate([jnp.zeros_like(a[:, :1]), a[:, :-1]], axis=1), a], axis=2)
    qb = blk(q)
    kk = with_prev(blk(k))
    vv = with_prev(blk(v))
    s = jnp.einsum('nbqhd,nbkhd->nbhqk', qb, kk).astype(jnp.float32) * SWA_SCALE
    qpos = jnp.arange(nb)[:, None] * SWA_BLK + jnp.arange(SWA_BLK)[None, :]
    kpos = (jnp.arange(nb)[:, None] - 1) * SWA_BLK + jnp.arange(2 * SWA_BLK)[None, :]
    dist = qpos[:, :, None] - kpos[:, None, :]
    valid = (kpos[:, None, :] >= 0) & (dist >= 0) & (dist <= span)
    s = jnp.where(valid[None, :, None], s, NEG)
    lse = jax.nn.logsumexp(s, axis=-1)
    p = jnp.exp(s - lse[..., None]).astype(v.dtype)
    o = jnp.einsum('nbhqk,nbkhd->nbqhd', p, vv).reshape(n, nb * SWA_BLK, h, d)[:, :l]
    lse = lse.transpose(0, 1, 3, 2).reshape(n, nb * SWA_BLK, h)[:, :l]
    return o, lse


def merge_groups(outs, lses):
    wts = jax.nn.softmax(jnp.stack(lses, 0), axis=0)
    return jnp.einsum('gnlh,gnlhd->nlhd', wts.astype(outs[0].dtype), jnp.stack(outs, 0))


def swa_prompt(h, w_qkv, qn_g, kn_g, w_o):
    n, t, _ = h.shape
    q, k, v = swa_project(h, w_qkv, qn_g, kn_g)
    outs, lses, states = [], [], []
    for g, (w, r) in enumerate(SWA_GROUPS):
        o, lse = band_attention(dilate(q[:, :, g], r), dilate(k[:, :, g], r), dilate(v[:, :, g], r), w // r)
        outs.append(undilate(o, r, n))
        lses.append(undilate(lse, r, n))
        keep = min(w, t)
        states.append(jnp.stack([k[:, t - keep:, g], v[:, t - keep:, g]], axis=2))
    o = merge_groups(outs, lses).reshape(n, t, SWA_WIDTH)
    return o @ w_o, states


def swa_decode(h, bufs, w_qkv, qn_g, kn_g, w_o):
    n, s, _ = h.shape
    q, k, v = swa_project(h, w_qkv, qn_g, kn_g)
    outs, lses, states = [], [], []
    for g, ((w, r), buf) in enumerate(zip(SWA_GROUPS, bufs)):
        lb = buf.shape[1]
        kf = jnp.concatenate([buf[:, :, 0].astype(k.dtype), k[:, :, g]], axis=1)
        vf = jnp.concatenate([buf[:, :, 1].astype(v.dtype), v[:, :, g]], axis=1)
        idx = lb + jnp.arange(s)[:, None] - r * jnp.arange(w // r + 1)[None, :]
        valid = idx >= 0
        idx = jnp.maximum(idx, 0)
        kg, vg = kf[:, idx], vf[:, idx]
        sc = jnp.einsum('nshd,nskhd->nhsk', q[:, :, g], kg).astype(jnp.float32) * SWA_SCALE
        sc = jnp.where(valid[None, None], sc, NEG)
        lse = jax.nn.logsumexp(sc, axis=-1)
        p = jnp.exp(sc - lse[..., None]).astype(vg.dtype)
        outs.append(jnp.einsum('nhsk,nskhd->nshd', p, vg))
        lses.append(lse.swapaxes(1, 2))
        keep = min(w, lb + s)
        states.append(jnp.stack([kf[:, lb + s - keep:], vf[:, lb + s - keep:]], axis=2))
    o = merge_groups(outs, lses).reshape(n, s, SWA_WIDTH)
    return o @ w_o, states


def pool_mixer(h, buf, pos0, w_pool, scale):
    n, l, _ = h.shape
    ext = jnp.concatenate([buf.astype(h.dtype), h], axis=1)
    cs = jnp.pad(jnp.cumsum(ext.astype(jnp.float32), axis=1), ((0, 0), (1, 0), (0, 0)))
    cnt_pos = pos0 + jnp.arange(l) + 1
    outs = []
    for g, w in enumerate(POOL_WINDOWS):
        sl = slice(g * POOL_GROUP, (g + 1) * POOL_GROUP)
        win_sum = cs[:, POOL_HIST + 1:, sl] - cs[:, POOL_HIST + 1 - w:POOL_HIST + 1 - w + l, sl]
        cnt = jnp.minimum(w, cnt_pos).astype(jnp.float32)
        d = (win_sum / cnt[None, :, None] - h[..., sl].astype(jnp.float32)).astype(h.dtype)
        outs.append(d @ w_pool[g])
    return jnp.concatenate(outs, axis=-1) * scale, ext[:, -POOL_HIST:]


def rope(x, pos):
    half = QK_ROPE // 2
    inv = ROPE_BASE ** (-jnp.arange(half, dtype=jnp.float32) / half)
    ang = pos.astype(jnp.float32)[:, None] * inv[None, :]
    shape = (1, pos.shape[0]) + (1,) * (x.ndim - 3) + (half,)
    cos, sin = jnp.cos(ang).reshape(shape), jnp.sin(ang).reshape(shape)
    xf = x.astype(jnp.float32)
    x1, x2 = xf[..., :half], xf[..., half:]
    return jnp.concatenate([x1 * cos - x2 * sin, x1 * sin + x2 * cos], axis=-1).astype(x.dtype)


def mla_project(h, pos, w_qa, g_qa, w_qb, w_kva, g_kva, g_qn, g_qr, g_kr):
    n, l, _ = h.shape
    q = (rms_norm(h @ w_qa, g_qa) @ w_qb).reshape(n, l, MLA_HEADS, QK_NOPE + QK_ROPE)
    q_nope = rms_norm(q[..., :QK_NOPE], g_qn)
    q_pe = rope(rms_norm(q[..., QK_NOPE:], g_qr), pos)
    kv = h @ w_kva
    lat = jnp.concatenate([rms_norm(kv[..., :KV_LORA], g_kva),
                           rope(rms_norm(kv[..., KV_LORA:], g_kr), pos)], axis=-1)
    return q_nope, q_pe, lat


def mla_expand(lat, w_kvb, g_kn):
    kv = jnp.einsum('nkc,che->nkhe', lat[..., :KV_LORA], w_kvb.reshape(KV_LORA, MLA_HEADS, QK_NOPE + V_DIM))
    return rms_norm(kv[..., :QK_NOPE], g_kn), kv[..., QK_NOPE:], lat[..., KV_LORA:]


def mla_scores(q_nope, q_pe, k_nope, kpe):
    s = jnp.einsum('nqhd,nkhd->nhqk', q_nope, k_nope) + jnp.einsum('nqhr,nkr->nhqk', q_pe, kpe)
    return s.astype(jnp.float32) * MLA_SCALE


def mla_prompt(q_nope, q_pe, lat, w_kvb, g_kn):
    n, t = q_nope.shape[:2]
    k_nope, v, kpe = mla_expand(lat, w_kvb, g_kn)
    nb = t // MLA_BLK
    blocks = lambda a: a.reshape(n, nb, MLA_BLK, *a.shape[2:]).swapaxes(0, 1)
    kpos = jnp.arange(t)

    def one_block(args):
        qn, qp, b = args
        qpos = b * MLA_BLK + jnp.arange(MLA_BLK)
        s = jnp.where(kpos[None, :] <= qpos[:, None], mla_scores(qn, qp, k_nope, kpe), NEG)
        p = jax.nn.softmax(s, axis=-1).astype(v.dtype)
        return jnp.einsum('nhqk,nkhd->nqhd', p, v)

    o = lax.map(one_block, (blocks(q_nope), blocks(q_pe), jnp.arange(nb)))
    return o.swapaxes(0, 1).reshape(n, t, MLA_HEADS, V_DIM)


def mla_decode(q_nope, q_pe, lat_new, cache, page_table, w_kvb, g_kn):
    n, s = q_nope.shape[:2]

    def page_part(phys):
        k_nope, v, kpe = mla_expand(cache[phys], w_kvb, g_kn)
        sc = mla_scores(q_nope, q_pe, k_nope, kpe)
        m = sc.max(axis=-1)
        e = jnp.exp(sc - m[..., None])
        return m, e.sum(axis=-1), jnp.einsum('nhsk,nkhd->nhsd', e.astype(v.dtype), v).astype(jnp.float32)

    m_p, l_p, a_p = lax.map(page_part, page_table.T)
    k_nope, v, kpe = mla_expand(lat_new, w_kvb, g_kn)
    sc = jnp.where(jnp.tril(jnp.ones((s, s), bool)), mla_scores(q_nope, q_pe, k_nope, kpe), NEG)
    m_n = sc.max(axis=-1)
    e = jnp.exp(sc - m_n[..., None])
    a_n = jnp.einsum('nhsk,nkhd->nhsd', e.astype(v.dtype), v).astype(jnp.float32)
    m_all = jnp.concatenate([m_p, m_n[None]], axis=0)
    l_all = jnp.concatenate([l_p, e.sum(axis=-1)[None]], axis=0)
    a_all = jnp.concatenate([a_p, a_n[None]], axis=0)
    wt = jnp.exp(m_all - m_all.max(axis=0))
    o = (wt[..., None] * a_all).sum(axis=0) / (wt * l_all).sum(axis=0)[..., None]
    return o.swapaxes(1, 2).astype(v.dtype)


def setup_inputs(seed: int = 0) -> dict:
    key = jax.random.key(seed)
    ks = iter(jax.random.split(key, 48))
    nrm = lambda shape, scale: jax.random.normal(next(ks), shape, jnp.float32) * scale
    gain = lambda shape: 1.0 + 0.1 * jax.random.normal(next(ks), shape, jnp.float32)
    n_pages = PAST_LEN // PAGE_SIZE
    n_used = DEC_BATCH * n_pages
    n_phys = n_used + n_used // 4
    D = D_MODEL
    inp = {}
    inp['x_prompt'] = nrm((BATCH, SEQ, D), 1.0)
    inp['x_sample'] = nrm((DEC_BATCH, DEC_SEQ, D), 1.0)
    inp['state_conv'] = nrm((NL_CONV, DEC_BATCH, CONV_WIDTH - 1, D), 1.0)
    for g, (w, r) in enumerate(SWA_GROUPS):
        inp['state_swa%d' % g] = nrm((NL_SWA, DEC_BATCH, min(w, PAST_LEN), 2, SWA_HEADS, SWA_HEAD_DIM), 1.0)
    inp['state_pool'] = nrm((NL_POOL, DEC_BATCH, POOL_HIST, D), 1.0)
    inp['cache_mla'] = nrm((NL_MLA, n_phys, PAGE_SIZE, KV_LORA + QK_ROPE), 1.0)
    inp['page_table'] = jax.random.permutation(next(ks), n_phys)[:n_used].reshape(DEC_BATCH, n_pages).astype(jnp.int32)
    inp['c_prompt'] = nrm((BATCH, D), 1.0)
    inp['c_sample'] = nrm((DEC_BATCH, D), 1.0)
    inp['ada_w'] = nrm((DEPTH, D, 6 * D), 0.5 * D ** -0.5)
    inp['ada_b'] = nrm((DEPTH, 6 * D), 0.01)
    inp['norm1_g'] = gain((DEPTH, D))
    inp['norm2_g'] = gain((DEPTH, D))
    inp['mlp_w1'] = nrm((DEPTH, D, D_FF), D ** -0.5)
    inp['mlp_w2'] = nrm((DEPTH, D_FF, D), D_FF ** -0.5)
    inp['conv_w_in'] = nrm((NL_CONV, D, 3 * D), D ** -0.5)
    inp['conv_k'] = nrm((NL_CONV, CONV_WIDTH, D), CONV_WIDTH ** -0.5)
    inp['conv_w_out'] = nrm((NL_CONV, D, D), D ** -0.5)
    inp['swa_w_qkv'] = nrm((NL_SWA, D, 3 * N_SWA_GROUPS * SWA_WIDTH), D ** -0.5)
    inp['swa_qn_g'] = gain((NL_SWA, N_SWA_GROUPS, SWA_HEAD_DIM))
    inp['swa_kn_g'] = gain((NL_SWA, N_SWA_GROUPS, SWA_HEAD_DIM))
    inp['swa_w_o'] = nrm((NL_SWA, SWA_WIDTH, D), SWA_WIDTH ** -0.5)
    inp['pool_w'] = nrm((NL_POOL, len(POOL_WINDOWS), POOL_GROUP, POOL_GROUP), POOL_GROUP ** -0.5)
    inp['pool_scale'] = gain((NL_POOL, D))
    inp['mla_w_qa'] = nrm((NL_MLA, D, Q_LORA), D ** -0.5)
    inp['mla_g_qa'] = gain((NL_MLA, Q_LORA))
    inp['mla_w_qb'] = nrm((NL_MLA, Q_LORA, MLA_HEADS * (QK_NOPE + QK_ROPE)), Q_LORA ** -0.5)
    inp['mla_w_kva'] = nrm((NL_MLA, D, KV_LORA + QK_ROPE), D ** -0.5)
    inp['mla_g_kva'] = gain((NL_MLA, KV_LORA))
    inp['mla_w_kvb'] = nrm((NL_MLA, KV_LORA, MLA_HEADS * (QK_NOPE + V_DIM)), KV_LORA ** -0.5)
    inp['mla_g_qn'] = gain((NL_MLA, QK_NOPE))
    inp['mla_g_kn'] = gain((NL_MLA, QK_NOPE))
    inp['mla_g_qr'] = gain((NL_MLA, QK_ROPE))
    inp['mla_g_kr'] = gain((NL_MLA, QK_ROPE))
    inp['mla_w_o'] = nrm((NL_MLA, MLA_HEADS * V_DIM, D), (MLA_HEADS * V_DIM) ** -0.5)
    return inp


def reference(x_prompt, x_sample, state_conv, state_swa0, state_swa1, state_swa2, state_pool, cache_mla, page_table,
              c_prompt, c_sample, ada_w, ada_b, norm1_g, norm2_g, mlp_w1, mlp_w2, conv_w_in, conv_k, conv_w_out,
              swa_w_qkv, swa_qn_g, swa_kn_g, swa_w_o, pool_w, pool_scale, mla_w_qa, mla_g_qa, mla_w_qb, mla_w_kva,
              mla_g_kva, mla_w_kvb, mla_g_qn, mla_g_kn, mla_g_qr, mla_g_kr, mla_w_o):
    xp, xs = x_prompt, x_sample
    bp, t = xp.shape[:2]
    bs, s = xs.shape[:2]
    past = page_table.shape[1] * PAGE_SIZE
    pos_p = jnp.arange(t)
    pos_s = past + jnp.arange(s)
    swa_states = (state_swa0, state_swa1, state_swa2)
    conv_p, conv_s, pool_p, pool_s, mla_p, mla_s = [], [], [], [], [], []
    swa_p = [[] for _ in SWA_GROUPS]
    swa_s = [[] for _ in SWA_GROUPS]
    for i in range(DEPTH):
        kind, j = i % N_MIXERS, i // N_MIXERS
        sh1p, sc1p, g1p, sh2p, sc2p, g2p = adaln(c_prompt, ada_w[i], ada_b[i])
        sh1s, sc1s, g1s, sh2s, sc2s, g2s = adaln(c_sample, ada_w[i], ada_b[i])
        hp = modulate(xp, norm1_g[i], sh1p, sc1p)
        hs = modulate(xs, norm1_g[i], sh1s, sc1s)
        if kind == 0:
            yp, stp = conv_mixer(hp, jnp.zeros((bp, CONV_WIDTH - 1, D_MODEL), hp.dtype), conv_w_in[j], conv_k[j], conv_w_out[j])
            ys, sts = conv_mixer(hs, state_conv[j], conv_w_in[j], conv_k[j], conv_w_out[j])
            conv_p.append(stp)
            conv_s.append(sts)
        elif kind == 1:
            yp, stp = swa_prompt(hp, swa_w_qkv[j], swa_qn_g[j], swa_kn_g[j], swa_w_o[j])
            ys, sts = swa_decode(hs, [st[j] for st in swa_states], swa_w_qkv[j], swa_qn_g[j], swa_kn_g[j], swa_w_o[j])
            for g in range(N_SWA_GROUPS):
                swa_p[g].append(stp[g])
                swa_s[g].append(sts[g])
        elif kind == 2:
            yp, stp = pool_mixer(hp, jnp.zeros((bp, POOL_HIST, D_MODEL), hp.dtype), 0, pool_w[j], pool_scale[j])
            ys, sts = pool_mixer(hs, state_pool[j], past, pool_w[j], pool_scale[j])
            pool_p.append(stp)
            pool_s.append(sts)
        else:
            qn, qr, latp = mla_project(hp, pos_p, mla_w_qa[j], mla_g_qa[j], mla_w_qb[j], mla_w_kva[j], mla_g_kva[j], mla_g_qn[j], mla_g_qr[j], mla_g_kr[j])
            yp = mla_prompt(qn, qr, latp, mla_w_kvb[j], mla_g_kn[j]).reshape(bp, t, MLA_HEADS * V_DIM) @ mla_w_o[j]
            qn, qr, lats = mla_project(hs, pos_s, mla_w_qa[j], mla_g_qa[j], mla_w_qb[j], mla_w_kva[j], mla_g_kva[j], mla_g_qn[j], mla_g_qr[j], mla_g_kr[j])
            ys = mla_decode(qn, qr, lats, cache_mla[j], page_table, mla_w_kvb[j], mla_g_kn[j]).reshape(bs, s, MLA_HEADS * V_DIM) @ mla_w_o[j]
            mla_p.append(latp)
            mla_s.append(lats)
        xp = xp + g1p * yp
        xs = xs + g1s * ys
        xp = xp + g2p * sq_relu_mlp(modulate(xp, norm2_g[i], sh2p, sc2p), mlp_w1[i], mlp_w2[i])
        xs = xs + g2s * sq_relu_mlp(modulate(xs, norm2_g[i], sh2s, sc2s), mlp_w1[i], mlp_w2[i])
    return (xp, xs, jnp.stack(conv_p), jnp.stack(conv_s),
            jnp.stack(swa_p[0]), jnp.stack(swa_s[0]), jnp.stack(swa_p[1]), jnp.stack(swa_s[1]),
            jnp.stack(swa_p[2]), jnp.stack(swa_s[2]), jnp.stack(pool_p), jnp.stack(pool_s),
            jnp.stack(mla_p), jnp.stack(mla_s))
```

```python
import functools

import jax
import jax.numpy as jnp
from jax import lax
from jax.experimental import pallas as pl
from jax.experimental.pallas import tpu as pltpu

F32 = jnp.float32
BF16 = jnp.bfloat16

EPS = 1e-6
NEG = -1e30

CONV_WIDTH = 3
SWA_GROUPS = ((128, 1), (512, 4), (2048, 16))
SWA_HEADS = 8
SWA_HEAD_DIM = 64
SWA_WIDTH = SWA_HEADS * SWA_HEAD_DIM
SWA_BLK = 128
SWA_SCALE = SWA_HEAD_DIM ** -0.5
POOL_WINDOWS = (2, 4, 8, 16)
POOL_HIST = 15
MLA_HEADS = 8
Q_LORA = 384
KV_LORA = 256
QK_NOPE = 128
QK_ROPE = 64
V_DIM = 128
MLA_SCALE = (QK_NOPE + QK_ROPE) ** -0.5
ROPE_BASE = 10000.0
PAGE_SIZE = 128

VMEM_LIMIT_BYTES = 56 * 1024 * 1024
CONV_GRP = 16
POOL_GRP = 32


def _params(*sem):
    return pltpu.CompilerParams(dimension_semantics=sem, vmem_limit_bytes=VMEM_LIMIT_BYTES)


def _rms_mod(x, g, shift, scale):
    y = x * lax.rsqrt(jnp.mean(x * x, axis=-1, keepdims=True) + EPS)
    return (y * g) * (1.0 + scale) + shift


def _dot(a, b):
    return jnp.dot(a, b, preferred_element_type=F32)


def _dot_nt(a, b):
    return lax.dot_general(a, b, (((1,), (1,)), ((), ())), preferred_element_type=F32)


def _mod_spec(m, tm, rows_per_group):
    d = m.shape[-1]
    if m.shape[1] == 1:
        tpg = rows_per_group // tm
        return pl.BlockSpec((1, 1, d), lambda i, *_: (i // tpg, 0, 0))
    return pl.BlockSpec((1, tm, d), lambda i, *_: (i, 0, 0))


def _per_token(m, reps, tm):
    e = jnp.repeat(m, reps, axis=0)
    return e.reshape(e.shape[0] // tm, tm, e.shape[1])


def _ada_kernel(c_ref, w_ref, b_ref, o_ref):
    c = c_ref[...]
    s = (c * (1.0 / (1.0 + jnp.exp(-c)))).astype(BF16)
    o_ref[0] = _dot(s, w_ref[0].astype(BF16)) + b_ref[0]


def _ada_call(c, ada_w, ada_b):
    depth, d, n6 = ada_w.shape
    n = c.shape[0]
    tn = 1536
    return pl.pallas_call(
        _ada_kernel,
        grid=(depth, n6 // tn),
        in_specs=[pl.BlockSpec((n, d), lambda i, j: (0, 0)),
                  pl.BlockSpec((1, d, tn), lambda i, j: (i, 0, j)),
                  pl.BlockSpec((1, 1, tn), lambda i, j: (i, 0, j))],
        out_specs=pl.BlockSpec((1, n, tn), lambda i, j: (i, 0, j)),
        out_shape=jax.ShapeDtypeStruct((depth, n, n6), F32),
        compiler_params=_params("parallel", "parallel"),
        name="adaln",
    )(c, ada_w, ada_b.reshape(depth, 1, n6))


def _mlp_kernel(x_ref, g_ref, sh_ref, sc_ref, gt_ref, w1_ref, w2_ref, o_ref, h_scr, acc_scr):
    j = pl.program_id(1)

    @pl.when(j == 0)
    def _():
        h_scr[...] = _rms_mod(x_ref[...], g_ref[...], sh_ref[0], sc_ref[0]).astype(BF16)
        acc_scr[...] = jnp.zeros_like(acc_scr)

    a = _dot(h_scr[...], w1_ref[...])
    a = jnp.square(jnp.maximum(a, 0.0)).astype(BF16)
    acc_scr[...] += _dot(a, w2_ref[...])

    @pl.when(j == pl.num_programs(1) - 1)
    def _():
        o_ref[...] = x_ref[...] + gt_ref[0] * acc_scr[...]


def _mlp_call(x, g, sh, sc, gt, w1, w2, rows_per_group):
    rows, d = x.shape
    f = w1.shape[1]
    tm = min(1024, rows)
    tf = 512
    ms = _mod_spec(sh, tm, rows_per_group)
    return pl.pallas_call(
        _mlp_kernel,
        grid=(rows // tm, f // tf),
        in_specs=[pl.BlockSpec((tm, d), lambda i, j: (i, 0)),
                  pl.BlockSpec((1, d), lambda i, j: (0, 0)),
                  ms, ms, ms,
                  pl.BlockSpec((d, tf), lambda i, j: (0, j)),
                  pl.BlockSpec((tf, d), lambda i, j: (j, 0))],
        out_specs=pl.BlockSpec((tm, d), lambda i, j: (i, 0)),
        out_shape=jax.ShapeDtypeStruct((rows, d), F32),
        scratch_shapes=[pltpu.VMEM((tm, d), BF16), pltpu.VMEM((tm, d), F32)],
        compiler_params=_params("parallel", "arbitrary"),
        name="mlp",
    )(x, g, sh, sc, gt, w1, w2)


def _outproj_kernel(a_ref, w_ref, x_ref, gt_ref, o_ref):
    o_ref[...] = x_ref[...] + gt_ref[0] * _dot(a_ref[...], w_ref[...])


def _outproj_call(a, w, x, gt, rows_per_group):
    rows, d = x.shape
    k = a.shape[1]
    tm = min(512, rows)
    return pl.pallas_call(
        _outproj_kernel,
        grid=(rows // tm,),
        in_specs=[pl.BlockSpec((tm, k), lambda i: (i, 0)),
                  pl.BlockSpec((k, d), lambda i: (0, 0)),
                  pl.BlockSpec((tm, d), lambda i: (i, 0)),
                  _mod_spec(gt, tm, rows_per_group)],
        out_specs=pl.BlockSpec((tm, d), lambda i: (i, 0)),
        out_shape=jax.ShapeDtypeStruct((rows, d), F32),
        compiler_params=_params("parallel"),
        name="outproj",
    )(a, w, x, gt)


def _conv_kernel(*refs, tm, d, tiles_per_seq, has_override):
    if has_override:
        x_ref, g_ref, sh_ref, sc_ref, w_ref, k_ref, ov_ref, bv_ref, cu_ref, e_scr = refs
    else:
        x_ref, g_ref, sh_ref, sc_ref, w_ref, k_ref, bv_ref, cu_ref, e_scr = refs
    i = pl.program_id(0)
    h = _rms_mod(x_ref[...], g_ref[...], sh_ref[0], sc_ref[0]).astype(BF16)
    proj = _dot(h, w_ref[...])
    b = proj[:, :d]
    cu = proj[:, d:2 * d] * proj[:, 2 * d:]
    if has_override:
        row = lax.broadcasted_iota(jnp.int32, (tm, 1), 0)
        cu = jnp.where((row % CONV_GRP) >= CONV_GRP // 2, cu, ov_ref[...])
        e_scr[0:8, :] = jnp.zeros((8, d), F32)
    else:
        @pl.when(i % tiles_per_seq == 0)
        def _():
            e_scr[0:8, :] = jnp.zeros((8, d), F32)

        @pl.when(i % tiles_per_seq != 0)
        def _():
            e_scr[0:8, :] = e_scr[tm:tm + 8, :]
    e_scr[8:tm + 8, :] = cu
    cu1 = e_scr[7:tm + 7, :]
    cu2 = e_scr[6:tm + 6, :]
    k = k_ref[...]
    v = k[0:1, :] * cu2 + k[1:2, :] * cu1 + k[2:3, :] * cu
    bv_ref[...] = (b * v).astype(BF16)
    cu_ref[...] = cu


def _conv_call(x, g, sh, sc, w_in, k, override, rows_per_group):
    rows, d = x.shape
    tm = min(512, rows)
    has_override = override is not None
    ms = _mod_spec(sh, tm, rows_per_group)
    in_specs = [pl.BlockSpec((tm, d), lambda i: (i, 0)),
                pl.BlockSpec((1, d), lambda i: (0, 0)),
                ms, ms,
                pl.BlockSpec((d, 3 * d), lambda i: (0, 0)),
                pl.BlockSpec((CONV_WIDTH, d), lambda i: (0, 0))]
    args = [x, g, sh, sc, w_in, k]
    if has_override:
        in_specs.append(pl.BlockSpec((tm, d), lambda i: (i, 0)))
        args.append(override)
    kern = functools.partial(_conv_kernel, tm=tm, d=d, tiles_per_seq=max(rows_per_group // tm, 1),
                             has_override=has_override)
    return pl.pallas_call(
        kern,
        grid=(rows // tm,),
        in_specs=in_specs,
        out_specs=[pl.BlockSpec((tm, d), lambda i: (i, 0)), pl.BlockSpec((tm, d), lambda i: (i, 0))],
        out_shape=[jax.ShapeDtypeStruct((rows, d), BF16), jax.ShapeDtypeStruct((rows, d), F32)],
        scratch_shapes=[pltpu.VMEM((tm + 8, d), F32)],
        compiler_params=_params("arbitrary"),
        name="conv_in",
    )(*args)


def _pool_kernel(*refs, tm, d, tiles_per_seq, sample, pos0):
    x_ref, g_ref, sh_ref, sc_ref, gt_ref, w_ref, ps_ref, o_ref, st_ref, e_scr = refs
    i = pl.program_id(0)
    x = x_ref[...]
    h = _rms_mod(x, g_ref[...], sh_ref[0], sc_ref[0])
    row = lax.broadcasted_iota(jnp.int32, (tm, 1), 0)
    if sample:
        local = (row % POOL_GRP) - POOL_GRP // 2
        h = jnp.where(local >= 0, h, x)
        e_scr[0:16, :] = jnp.zeros((16, d), F32)
    else:
        local = (i % tiles_per_seq) * tm + row

        @pl.when(i % tiles_per_seq == 0)
        def _():
            e_scr[0:16, :] = jnp.zeros((16, d), F32)

        @pl.when(i % tiles_per_seq != 0)
        def _():
            e_scr[0:16, :] = e_scr[tm:tm + 16, :]
    e_scr[16:tm + 16, :] = h
    if sample:
        st_ref[...] = h
    else:
        st_ref[0] = e_scr[tm:tm + 16, :]
    gq = d // len(POOL_WINDOWS)
    for gi, w in enumerate(POOL_WINDOWS):
        lo = gi * gq
        win = h[:, lo:lo + gq]
        for j in range(1, w):
            win = win + e_scr[16 - j:16 - j + tm, lo:lo + gq]
        cnt = jnp.clip(pos0 + local + 1, 1, w).astype(F32)
        dd = (win / cnt - h[:, lo:lo + gq]).astype(BF16)
        y = _dot(dd, w_ref[gi]) * ps_ref[:, lo:lo + gq]
        o_ref[:, lo:lo + gq] = x[:, lo:lo + gq] + gt_ref[0][:, lo:lo + gq] * y


def _pool_call(x, g, sh, sc, gt, w, ps, rows_per_group, sample, pos0):
    rows, d = x.shape
    tm = min(512, rows)
    ms = _mod_spec(sh, tm, rows_per_group)
    gq = d // len(POOL_WINDOWS)
    if sample:
        st_spec = pl.BlockSpec((tm, d), lambda i: (i, 0))
        st_shape = jax.ShapeDtypeStruct((rows, d), F32)
    else:
        st_spec = pl.BlockSpec((1, 16, d), lambda i: (i, 0, 0))
        st_shape = jax.ShapeDtypeStruct((rows // tm, 16, d), F32)
    kern = functools.partial(_pool_kernel, tm=tm, d=d, tiles_per_seq=max(rows_per_group // tm, 1),
                             sample=sample, pos0=pos0)
    return pl.pallas_call(
        kern,
        grid=(rows // tm,),
        in_specs=[pl.BlockSpec((tm, d), lambda i: (i, 0)),
                  pl.BlockSpec((1, d), lambda i: (0, 0)),
                  ms, ms, ms,
                  pl.BlockSpec((len(POOL_WINDOWS), gq, gq), lambda i: (0, 0, 0)),
                  pl.BlockSpec((1, d), lambda i: (0, 0))],
        out_specs=[pl.BlockSpec((tm, d), lambda i: (i, 0)), st_spec],
        out_shape=[jax.ShapeDtypeStruct((rows, d), F32), st_shape],
        scratch_shapes=[pltpu.VMEM((tm + 16, d), F32)],
        compiler_params=_params("arbitrary"),
        name="pool",
    )(x, g, sh, sc, gt, w, ps)


def _swa_qkv_kernel(x_ref, g_ref, sh_ref, sc_ref, w_ref, gn_ref, seg_ref, o_ref, h_scr):
    j = pl.program_id(1)

    @pl.when(j == 0)
    def _():
        h_scr[...] = _rms_mod(x_ref[...], g_ref[...], sh_ref[0], sc_ref[0]).astype(BF16)

    z = _dot(h_scr[...], w_ref[...])

    @pl.when(j < 6)
    def _():
        ms = _dot((z * z).astype(BF16), seg_ref[...])
        o_ref[...] = z * lax.rsqrt(ms + EPS) * gn_ref[0]

    @pl.when(j >= 6)
    def _():
        o_ref[...] = z


def _swa_qkv_call(x, g, sh, sc, w, gn, seg, rows_per_group):
    rows, d = x.shape
    n = w.shape[1]
    tm = min(1024, rows)
    tn = SWA_WIDTH
    ms = _mod_spec(sh, tm, rows_per_group)
    return pl.pallas_call(
        _swa_qkv_kernel,
        grid=(rows // tm, n // tn),
        in_specs=[pl.BlockSpec((tm, d), lambda i, j: (i, 0)),
                  pl.BlockSpec((1, d), lambda i, j: (0, 0)),
                  ms, ms,
                  pl.BlockSpec((d, tn), lambda i, j: (0, j)),
                  pl.BlockSpec((1, 1, tn), lambda i, j: (jnp.minimum(j, 5), 0, 0)),
                  pl.BlockSpec((tn, tn), lambda i, j: (0, 0))],
        out_specs=pl.BlockSpec((tm, tn), lambda i, j: (i, j)),
        out_shape=jax.ShapeDtypeStruct((rows, n), F32),
        scratch_shapes=[pltpu.VMEM((tm, d), BF16)],
        compiler_params=_params("parallel", "arbitrary"),
        name="swa_qkv",
    )(x, g, sh, sc, w, gn, seg)


def _swa_band_kernel(q_ref, kc_ref, kp_ref, vc_ref, vp_ref, o_ref, l_ref):
    b = pl.program_id(2)
    q = q_ref[0].astype(BF16)
    kc = kc_ref[0].astype(BF16)
    kp = kp_ref[0].astype(BF16)
    vc = vc_ref[0].astype(BF16)
    vp = vp_ref[0].astype(BF16)
    qi = lax.broadcasted_iota(jnp.int32, (SWA_BLK, SWA_BLK), 0)
    kj = lax.broadcasted_iota(jnp.int32, (SWA_BLK, SWA_BLK), 1)
    cur_ok = kj <= qi
    prev_ok = jnp.logical_and(kj >= qi, b > 0)
    hd = SWA_HEAD_DIM
    for h in range(SWA_HEADS):
        sl = slice(h * hd, (h + 1) * hd)
        sc_ = jnp.where(cur_ok, _dot_nt(q[:, sl], kc[:, sl]) * SWA_SCALE, NEG)
        sp_ = jnp.where(prev_ok, _dot_nt(q[:, sl], kp[:, sl]) * SWA_SCALE, NEG)
        m = jnp.maximum(jnp.max(sc_, axis=-1, keepdims=True), jnp.max(sp_, axis=-1, keepdims=True))
        pc = jnp.exp(sc_ - m)
        pp = jnp.exp(sp_ - m)
        l = jnp.sum(pc, axis=-1, keepdims=True) + jnp.sum(pp, axis=-1, keepdims=True)
        o = (_dot(pc.astype(BF16), vc[:, sl]) + _dot(pp.astype(BF16), vp[:, sl])) / l
        o_ref[0, :, sl] = o
        l_ref[0, :, sl] = jnp.broadcast_to(m + jnp.log(l), (SWA_BLK, hd))


def _swa_band_call(qkv3, g, r, n_seq, seq_len):
    lr = seq_len // r
    nb = lr // SWA_BLK
    w = SWA_WIDTH
    blk = (1, SWA_BLK, w)

    def col(which):
        return lambda n, rho, b: (n, b, rho * 9 + which * 3 + g)

    def col_prev(which):
        return lambda n, rho, b: (n, jnp.maximum(b - 1, 0), rho * 9 + which * 3 + g)

    out_spec = pl.BlockSpec(blk, lambda n, rho, b: (n, b, rho))
    o, l = pl.pallas_call(
        _swa_band_kernel,
        grid=(n_seq, r, nb),
        in_specs=[pl.BlockSpec(blk, col(0)),
                  pl.BlockSpec(blk, col(1)), pl.BlockSpec(blk, col_prev(1)),
                  pl.BlockSpec(blk, col(2)), pl.BlockSpec(blk, col_prev(2))],
        out_specs=[out_spec, out_spec],
        out_shape=[jax.ShapeDtypeStruct((n_seq, lr, r * w), F32)] * 2,
        compiler_params=_params("parallel", "parallel", "arbitrary"),
        name="swa_band",
    )(qkv3, qkv3, qkv3, qkv3, qkv3)
    return o.reshape(n_seq * seq_len, w), l.reshape(n_seq * seq_len, w)


def _swa_merge_kernel(o0, o1, o2, l0, l1, l2, w_ref, x_ref, gt_ref, out_ref):
    a0, a1, a2 = l0[...], l1[...], l2[...]
    m = jnp.maximum(jnp.maximum(a0, a1), a2)
    e0, e1, e2 = jnp.exp(a0 - m), jnp.exp(a1 - m), jnp.exp(a2 - m)
    inv = 1.0 / (e0 + e1 + e2)
    merged = (e0 * inv) * o0[...] + (e1 * inv) * o1[...] + (e2 * inv) * o2[...]
    out_ref[...] = x_ref[...] + gt_ref[0] * _dot(merged.astype(BF16), w_ref[...])


def _swa_merge_call(os_, ls_, w_o, x, gt, rows_per_group):
    rows, d = x.shape
    tm = min(512, rows)
    w = SWA_WIDTH
    sp = pl.BlockSpec((tm, w), lambda i: (i, 0))
    return pl.pallas_call(
        _swa_merge_kernel,
        grid=(rows // tm,),
        in_specs=[sp] * 6 + [pl.BlockSpec((w, d), lambda i: (0, 0)),
                             pl.BlockSpec((tm, d), lambda i: (i, 0)),
                             _mod_spec(gt, tm, rows_per_group)],
        out_specs=pl.BlockSpec((tm, d), lambda i: (i, 0)),
        out_shape=jax.ShapeDtypeStruct((rows, d), F32),
        compiler_params=_params("parallel"),
        name="swa_merge",
    )(*os_, *ls_, w_o, x, gt)


def _swa_decode_kernel(qkv_ref, s0_ref, s1_ref, s2_ref, o_ref, *, s_new):
    w = SWA_WIDTH
    hd = SWA_HEAD_DIM
    nrow = SWA_HEADS * s_new
    qkv = qkv_ref[0]
    rowi = lax.broadcasted_iota(jnp.int32, (nrow, w), 0)
    lanei = lax.broadcasted_iota(jnp.int32, (nrow, w), 1)
    diag = (rowi // s_new) == (lanei // hd)
    ms, ls, accs = [], [], []
    for g, ((win, r), st_ref) in enumerate(zip(SWA_GROUPS, (s0_ref, s1_ref, s2_ref))):
        lb = st_ref.shape[1]
        q = qkv[:, g * w:(g + 1) * w]
        qbd = jnp.where(diag, jnp.concatenate([q] * SWA_HEADS, axis=0), 0.0).astype(BF16)
        k_new = qkv[:, (3 + g) * w:(4 + g) * w].astype(BF16)
        v_new = qkv[:, (6 + g) * w:(7 + g) * w].astype(BF16)
        k_old = st_ref[0, :, 0:w].astype(BF16)
        v_old = st_ref[0, :, w:2 * w].astype(BF16)
        s_idx = lax.broadcasted_iota(jnp.int32, (nrow, lb), 0) % s_new
        dist = lb + s_idx - lax.broadcasted_iota(jnp.int32, (nrow, lb), 1)
        ok_old = jnp.logical_and(dist % r == 0, dist <= win)
        s_idx2 = lax.broadcasted_iota(jnp.int32, (nrow, s_new), 0) % s_new
        dist2 = s_idx2 - lax.broadcasted_iota(jnp.int32, (nrow, s_new), 1)
        ok_new = jnp.logical_and(jnp.logical_and(dist2 >= 0, dist2 % r == 0), dist2 <= win)
        sc_old = jnp.where(ok_old, _dot_nt(qbd, k_old) * SWA_SCALE, NEG)
        sc_new = jnp.where(ok_new, _dot_nt(qbd, k_new) * SWA_SCALE, NEG)
        m = jnp.maximum(jnp.max(sc_old, axis=-1, keepdims=True), jnp.max(sc_new, axis=-1, keepdims=True))
        p_old = jnp.exp(sc_old - m)
        p_new = jnp.exp(sc_new - m)
        l = jnp.sum(p_old, axis=-1, keepdims=True) + jnp.sum(p_new, axis=-1, keepdims=True)
        acc = _dot(p_old.astype(BF16), v_old) + _dot(p_new.astype(BF16), v_new)
        ms.append(m)
        ls.append(l)
        accs.append(acc / l)
    lse = [m + jnp.log(l) for m, l in zip(ms, ls)]
    mx = jnp.maximum(jnp.maximum(lse[0], lse[1]), lse[2])
    es = [jnp.exp(a - mx) for a in lse]
    inv = 1.0 / (es[0] + es[1] + es[2])
    merged = (es[0] * inv) * accs[0] + (es[1] * inv) * accs[1] + (es[2] * inv) * accs[2]
    merged = jnp.where(diag, merged, 0.0)
    out = merged[0:s_new]
    for h in range(1, SWA_HEADS):
        out = out + merged[h * s_new:(h + 1) * s_new]
    o_ref[0] = out.astype(BF16)


def _swa_decode_call(qkv, states, n_seq, s_new):
    w = SWA_WIDTH
    qkv3 = qkv.reshape(n_seq, s_new, 9 * w)
    sts = [st.reshape(n_seq, st.shape[1], 2 * w) for st in states]
    in_specs = [pl.BlockSpec((1, s_new, 9 * w), lambda n: (n, 0, 0))]
    in_specs += [pl.BlockSpec((1, st.shape[1], 2 * w), lambda n: (n, 0, 0)) for st in sts]
    o = pl.pallas_call(
        functools.partial(_swa_decode_kernel, s_new=s_new),
        grid=(n_seq,),
        in_specs=in_specs,
        out_specs=pl.BlockSpec((1, s_new, w), lambda n: (n, 0, 0)),
        out_shape=jax.ShapeDtypeStruct((n_seq, s_new, w), BF16),
        compiler_params=_params("parallel"),
        name="swa_decode",
    )(qkv3, *sts)
    return o.reshape(n_seq * s_new, w)


def _rope128(bn, cos_t, sin_t):
    return bn * cos_t + pltpu.roll(bn, 64, axis=1) * sin_t


def _mla_proj_kernel(*refs, decode):
    (x_ref, g_ref, sh_ref, sc_ref, w1_ref, gqa_ref, wqb_ref, gq_ref, gkv_ref, gkr_ref,
     cos_ref, sin_ref, wx_ref) = refs[:13]
    outs = refs[13:]
    nh = MLA_HEADS
    h = _rms_mod(x_ref[...], g_ref[...], sh_ref[0], sc_ref[0]).astype(BF16)
    a = _dot(h, w1_ref[...])
    qa = a[:, :Q_LORA]
    qa = (qa * lax.rsqrt(jnp.mean(qa * qa, axis=-1, keepdims=True) + EPS) * gqa_ref[...]).astype(BF16)
    q = _dot(qa, wqb_ref[...])
    cos_t = cos_ref[...]
    sin_t = sin_ref[...]
    latc = a[:, Q_LORA:Q_LORA + KV_LORA]
    latc = latc * lax.rsqrt(jnp.mean(latc * latc, axis=-1, keepdims=True) + EPS) * gkv_ref[...]
    kb = a[:, Q_LORA + KV_LORA:]
    kb = kb * lax.rsqrt(jnp.sum(kb * kb, axis=-1, keepdims=True) * (1.0 / QK_ROPE) + EPS) * gkr_ref[...]
    kb = _rope128(kb, cos_t, sin_t)
    lane = lax.broadcasted_iota(jnp.int32, kb.shape, 1)
    kstd = jnp.where(lane < 32, kb, pltpu.roll(kb, 96, axis=1))
    if decode:
        lat_ref, qabs_ref, qpe_ref = outs
    else:
        lat_ref, qcat_ref, kcat_ref, v_ref = outs
    lat_ref[:, 0:KV_LORA] = latc
    lat_ref[:, KV_LORA:KV_LORA + QK_ROPE] = kstd[:, 0:QK_ROPE]
    gq = gq_ref[...]
    for hh in range(nh):
        an = q[:, hh * 256:hh * 256 + 128]
        an = an * lax.rsqrt(jnp.mean(an * an, axis=-1, keepdims=True) + EPS) * gq[:, 0:128]
        bn = q[:, hh * 256 + 128:hh * 256 + 256]
        bn = bn * lax.rsqrt(jnp.sum(bn * bn, axis=-1, keepdims=True) * (1.0 / QK_ROPE) + EPS) * gq[:, 128:256]
        bn = _rope128(bn, cos_t, sin_t)
        if decode:
            ag = an * gq[:, 256:384]
            a_hi = ag.astype(BF16)
            a_lo = (ag - a_hi.astype(F32)).astype(BF16)
            qabs_ref[:, hh * 256:(hh + 1) * 256] = (_dot(a_hi, wx_ref[hh]) + _dot(a_lo, wx_ref[hh])).astype(BF16)
            qpe_ref[:, hh * 128:(hh + 1) * 128] = jnp.where(lane < 32, bn, pltpu.roll(bn, 96, axis=1)).astype(BF16)
        else:
            qcat_ref[:, hh * 256:hh * 256 + 128] = an.astype(BF16)
            qcat_ref[:, hh * 256 + 128:(hh + 1) * 256] = bn.astype(BF16)
    if not decode:
        gkn = gq_ref[:, 256:384]
        kvx = _dot(latc.astype(BF16), wx_ref[...])
        kbb = kb.astype(BF16)
        for hh in range(nh):
            kn = kvx[:, hh * 128:(hh + 1) * 128]
            kn = kn * lax.rsqrt(jnp.mean(kn * kn, axis=-1, keepdims=True) + EPS) * gkn
            kcat_ref[:, hh * 256:hh * 256 + 128] = kn.astype(BF16)
            kcat_ref[:, hh * 256 + 128:(hh + 1) * 256] = kbb
        v_ref[...] = kvx[:, nh * 128:].astype(BF16)


def _mla_proj_call(x, g, sh, sc, wts, cos_t, sin_t, rows_per_group, decode):
    rows, d = x.shape
    tm = min(512, rows)
    nh = MLA_HEADS
    ms = _mod_spec(sh, tm, rows_per_group)
    tps = max(rows_per_group // tm, 1)
    if decode:
        tab_spec = pl.BlockSpec((tm, 128), lambda i: (0, 0))
        wx = wts["wk_t"]
        wx_spec = pl.BlockSpec(wx.shape, lambda i: (0, 0, 0))
        out_specs = [pl.BlockSpec((tm, KV_LORA + QK_ROPE), lambda i: (i, 0)),
                     pl.BlockSpec((tm, nh * 256), lambda i: (i, 0)),
                     pl.BlockSpec((tm, nh * 128), lambda i: (i, 0))]
        out_shape = [jax.ShapeDtypeStruct((rows, KV_LORA + QK_ROPE), F32),
                     jax.ShapeDtypeStruct((rows, nh * 256), BF16),
                     jax.ShapeDtypeStruct((rows, nh * 128), BF16)]
    else:
        tab_spec = pl.BlockSpec((tm, 128), lambda i: (i % tps, 0))
        wx = wts["w_kvb"]
        wx_spec = pl.BlockSpec(wx.shape, lambda i: (0, 0))
        out_specs = [pl.BlockSpec((tm, KV_LORA + QK_ROPE), lambda i: (i, 0)),
                     pl.BlockSpec((tm, nh * 256), lambda i: (i, 0)),
                     pl.BlockSpec((tm, nh * 256), lambda i: (i, 0)),
                     pl.BlockSpec((tm, nh * 128), lambda i: (i, 0))]
        out_shape = [jax.ShapeDtypeStruct((rows, KV_LORA + QK_ROPE), F32),
                     jax.ShapeDtypeStruct((rows, nh * 256), BF16),
                     jax.ShapeDtypeStruct((rows, nh * 256), BF16),
                     jax.ShapeDtypeStruct((rows, nh * 128), BF16)]

    def full(a):
        return pl.BlockSpec(a.shape, lambda i: (0,) * a.ndim)

    return pl.pallas_call(
        functools.partial(_mla_proj_kernel, decode=decode),
        grid=(rows // tm,),
        in_specs=[pl.BlockSpec((tm, d), lambda i: (i, 0)),
                  pl.BlockSpec((1, d), lambda i: (0, 0)),
                  ms, ms,
                  full(wts["w1"]), full(wts["g_qa"]), full(wts["w_qb"]), full(wts["gq"]),
                  full(wts["g_kva"]), full(wts["g_kr"]),
                  tab_spec, tab_spec, wx_spec],
        out_specs=out_specs,
        out_shape=out_shape,
        compiler_params=_params("parallel"),
        name="mla_proj_decode" if decode else "mla_proj",
    )(x, g, sh, sc, wts["w1"], wts["g_qa"], wts["w_qb"], wts["gq"], wts["g_kva"], wts["g_kr"],
      cos_t, sin_t, wx)


def _mla_flash_kernel(q_ref, k_ref, v_ref, o_ref, m_scr, l_scr, acc_scr, *, tq):
    qi = pl.program_id(2)
    ki = pl.program_id(3)

    @pl.when(ki == 0)
    def _():
        m_scr[...] = jnp.full_like(m_scr, NEG)
        l_scr[...] = jnp.zeros_like(l_scr)
        acc_scr[...] = jnp.zeros_like(acc_scr)

    def step(masked):
        s = _dot_nt(q_ref[0], k_ref[0]) * MLA_SCALE
        if masked:
            r = lax.broadcasted_iota(jnp.int32, (tq, tq), 0)
            c = lax.broadcasted_iota(jnp.int32, (tq, tq), 1)
            s = jnp.where(c <= r, s, NEG)
        m_old = m_scr[...]
        m_new = jnp.maximum(m_old, jnp.max(s, axis=-1, keepdims=True))
        alpha = jnp.exp(m_old - m_new)
        p = jnp.exp(s - m_new)
        l_scr[...] = alpha * l_scr[...] + jnp.sum(p, axis=-1, keepdims=True)
        acc_scr[...] = alpha * acc_scr[...] + _dot(p.astype(BF16), v_ref[0])
        m_scr[...] = m_new

    @pl.when(ki < qi)
    def _():
        step(False)

    @pl.when(ki == qi)
    def _():
        step(True)
        o_ref[0] = (acc_scr[...] / l_scr[...]).astype(BF16)


def _mla_flash_call(qcat, kcat, v, n_seq, seq_len):
    nh = MLA_HEADS
    tq = min(1024, seq_len)
    nq = seq_len // tq
    q3 = qcat.reshape(n_seq, seq_len, nh * 256)
    k3 = kcat.reshape(n_seq, seq_len, nh * 256)
    v3 = v.reshape(n_seq, seq_len, nh * V_DIM)
    o = pl.pallas_call(
        functools.partial(_mla_flash_kernel, tq=tq),
        grid=(n_seq, nh, nq, nq),
        in_specs=[pl.BlockSpec((1, tq, 256), lambda n, h, qi, ki: (n, qi, h)),
                  pl.BlockSpec((1, tq, 256), lambda n, h, qi, ki: (n, jnp.minimum(ki, qi), h)),
                  pl.BlockSpec((1, tq, V_DIM), lambda n, h, qi, ki: (n, jnp.minimum(ki, qi), h))],
        out_specs=pl.BlockSpec((1, tq, V_DIM), lambda n, h, qi, ki: (n, qi, h)),
        out_shape=jax.ShapeDtypeStruct((n_seq, seq_len, nh * V_DIM), BF16),
        scratch_shapes=[pltpu.VMEM((tq, 1), F32), pltpu.VMEM((tq, 1), F32), pltpu.VMEM((tq, V_DIM), F32)],
        compiler_params=_params("parallel", "parallel", "parallel", "arbitrary"),
        name="mla_flash",
    )(q3, k3, v3)
    return o.reshape(n_seq * seq_len, nh * V_DIM)


def _mla_decode_kernel(pt_ref, *refs, n_pg, s_new):
    del pt_ref
    page_refs = refs[:n_pg]
    (qabs_ref, qpe_ref, latn_ref, wkt_ref, wv_ref, o_ref,
     lhs_scr, new_scr, m_scr, l_scr, ctx_scr) = refs[n_pg:]
    c = pl.program_id(1)
    nh = MLA_HEADS
    nrow = nh * s_new
    nk = nh * QK_NOPE

    @pl.when(c == 0)
    def _():
        lhs_scr[0:nk, :] = wkt_ref[...]
        lhs_scr[nk:nk + nrow, :] = qabs_ref[0]
        m_scr[...] = jnp.full_like(m_scr, NEG)
        l_scr[...] = jnp.zeros_like(l_scr)
        ctx_scr[...] = jnp.zeros_like(ctx_scr)

    def process(latc, kpe, mask):
        r = _dot_nt(lhs_scr[...], latc)
        spe = _dot_nt(qpe_ref[0][:, 0:QK_ROPE], kpe)
        parts = []
        for h in range(nh):
            blk = r[h * QK_NOPE:(h + 1) * QK_NOPE]
            rstd = lax.rsqrt(jnp.sum(blk * blk, axis=0, keepdims=True) * (1.0 / QK_NOPE) + EPS)
            parts.append(r[nk + h * s_new:nk + (h + 1) * s_new] * rstd)
        s = (jnp.concatenate(parts, axis=0) + spe) * MLA_SCALE
        if mask is not None:
            s = jnp.where(mask, s, NEG)
        m_old = m_scr[...]
        m_new = jnp.maximum(m_old, jnp.max(s, axis=-1, keepdims=True))
        alpha = jnp.exp(m_old - m_new)
        p = jnp.exp(s - m_new)
        l_scr[...] = alpha * l_scr[...] + jnp.sum(p, axis=-1, keepdims=True)
        ctx_scr[...] = alpha * ctx_scr[...] + _dot(p.astype(BF16), latc)
        m_scr[...] = m_new

    latc = jnp.concatenate([pr[0, :, 0:KV_LORA].astype(BF16) for pr in page_refs], axis=0)
    kpe = jnp.concatenate([pr[0, :, KV_LORA:KV_LORA + QK_ROPE].astype(BF16) for pr in page_refs], axis=0)
    process(latc, kpe, None)

    @pl.when(c == pl.num_programs(1) - 1)
    def _():
        new_scr[...] = jnp.zeros_like(new_scr)
        new_scr[0:s_new, :] = latn_ref[0]
        lat = new_scr[...]
        si = lax.broadcasted_iota(jnp.int32, (nrow, PAGE_SIZE), 0) % s_new
        kj = lax.broadcasted_iota(jnp.int32, (nrow, PAGE_SIZE), 1)
        process(lat[:, 0:KV_LORA].astype(BF16), lat[:, KV_LORA:KV_LORA + QK_ROPE].astype(BF16), kj <= si)
        ctx = ctx_scr[...] / l_scr[...]
        for h in range(nh):
            ctx_h = ctx[h * s_new:(h + 1) * s_new].astype(BF16)
            o_ref[0, :, h * V_DIM:(h + 1) * V_DIM] = _dot(ctx_h, wv_ref[h]).astype(BF16)


def _mla_decode_call(cache, page_table, qabs, qpe, lat_new, wkt, wv, n_seq, s_new):
    nh = MLA_HEADS
    n_pages = page_table.shape[1]
    n_pg = min(8, n_pages)
    lw = KV_LORA + QK_ROPE
    nrow = nh * s_new

    def page_spec(p):
        return pl.BlockSpec((1, PAGE_SIZE, lw), lambda n, c, pt: (pt[n * n_pages + c * n_pg + p], 0, 0))

    grid_spec = pltpu.PrefetchScalarGridSpec(
        num_scalar_prefetch=1,
        grid=(n_seq, n_pages // n_pg),
        in_specs=[page_spec(p) for p in range(n_pg)] + [
            pl.BlockSpec((1, nrow, 256), lambda n, c, pt: (n, 0, 0)),
            pl.BlockSpec((1, nrow, 128), lambda n, c, pt: (n, 0, 0)),
            pl.BlockSpec((1, s_new, lw), lambda n, c, pt: (n, 0, 0)),
            pl.BlockSpec(wkt.shape, lambda n, c, pt: (0, 0)),
            pl.BlockSpec(wv.shape, lambda n, c, pt: (0, 0, 0))],
        out_specs=pl.BlockSpec((1, s_new, nh * V_DIM), lambda n, c, pt: (n, 0, 0)),
        scratch_shapes=[pltpu.VMEM((nh * QK_NOPE + nrow, KV_LORA), BF16),
                        pltpu.VMEM((PAGE_SIZE, lw), F32),
                        pltpu.VMEM((nrow, 1), F32), pltpu.VMEM((nrow, 1), F32),
                        pltpu.VMEM((nrow, KV_LORA), F32)])
    o = pl.pallas_call(
        functools.partial(_mla_decode_kernel, n_pg=n_pg, s_new=s_new),
        grid_spec=grid_spec,
        out_shape=jax.ShapeDtypeStruct((n_seq, s_new, nh * V_DIM), BF16),
        compiler_params=_params("parallel", "arbitrary"),
        name="mla_decode",
    )(page_table.reshape(-1), *([cache] * n_pg),
      jnp.transpose(qabs.reshape(n_seq, s_new, nh, 256), (0, 2, 1, 3)).reshape(n_seq, nrow, 256),
      jnp.transpose(qpe.reshape(n_seq, s_new, nh, 128), (0, 2, 1, 3)).reshape(n_seq, nrow, 128),
      lat_new.reshape(n_seq, s_new, lw), wkt, wv)
    return o.reshape(n_seq * s_new, nh * V_DIM)


def _mla_weights(w_qa, g_qa, w_qb, w_kva, g_kva, w_kvb, g_qn, g_kn, g_qr, g_kr):
    nh = MLA_HEADS
    half = QK_ROPE // 2

    def pad_rope_cols(w):
        z = jnp.zeros(w.shape[:-1] + (half,), w.dtype)
        return jnp.concatenate([w[..., :half], z, w[..., half:], z], axis=-1)

    w_kva_p = jnp.concatenate([w_kva[:, :KV_LORA], pad_rope_cols(w_kva[:, KV_LORA:])], axis=1)
    w1 = jnp.concatenate([w_qa, w_kva_p], axis=1).astype(BF16)
    wq = w_qb.reshape(Q_LORA, nh, QK_NOPE + QK_ROPE)
    wq = jnp.concatenate([wq[..., :QK_NOPE], pad_rope_cols(wq[..., QK_NOPE:])], axis=-1)
    w_qb_p = wq.reshape(Q_LORA, nh * 256).astype(BF16)
    gq = jnp.concatenate([g_qn, pad_rope_cols(g_qr), g_kn])[None, :]
    wkv = w_kvb.reshape(KV_LORA, nh, QK_NOPE + V_DIM)
    w_kvb_p = jnp.concatenate([wkv[..., :QK_NOPE].reshape(KV_LORA, nh * QK_NOPE),
                               wkv[..., QK_NOPE:].reshape(KV_LORA, nh * V_DIM)], axis=1).astype(BF16)
    wk_t = jnp.transpose(wkv[..., :QK_NOPE], (1, 2, 0)).astype(BF16)
    return dict(w1=w1, g_qa=g_qa[None, :], w_qb=w_qb_p, gq=gq, g_kva=g_kva[None, :],
                g_kr=pad_rope_cols(g_kr)[None, :], w_kvb=w_kvb_p, wk_t=wk_t,
                wkt_flat=wk_t.reshape(nh * QK_NOPE, KV_LORA),
                wv=jnp.transpose(wkv[..., QK_NOPE:], (1, 0, 2)).astype(BF16))


def _rope_tables(pos):
    half = QK_ROPE // 2
    inv = ROPE_BASE ** (-jnp.arange(half, dtype=F32) / half)
    ang = pos.astype(F32)[:, None] * inv[None, :]
    cos, sin = jnp.cos(ang), jnp.sin(ang)
    z = jnp.zeros_like(cos)
    return (jnp.concatenate([cos, z, cos, z], axis=1), jnp.concatenate([-sin, z, sin, z], axis=1))


def kernel(x_prompt, x_sample, state_conv, state_swa0, state_swa1, state_swa2, state_pool, cache_mla, page_table,
           c_prompt, c_sample, ada_w, ada_b, norm1_g, norm2_g, mlp_w1, mlp_w2, conv_w_in, conv_k, conv_w_out,
           swa_w_qkv, swa_qn_g, swa_kn_g, swa_w_o, pool_w, pool_scale, mla_w_qa, mla_g_qa, mla_w_qb, mla_w_kva,
           mla_g_kva, mla_w_kvb, mla_g_qn, mla_g_kn, mla_g_qr, mla_g_kr, mla_w_o):
    bp, t, d = x_prompt.shape
    bs, s, _ = x_sample.shape
    depth = ada_w.shape[0]
    past = page_table.shape[1] * PAGE_SIZE
    swa_states = (state_swa0, state_swa1, state_swa2)
    for (win, _), st in zip(SWA_GROUPS, swa_states):
        assert st.shape[2] == win, "sliding-window state must hold a full window"
    assert t % (SWA_GROUPS[-1][1] * SWA_BLK) == 0, "prompt length must split into full dilated blocks"
    rows_p, rows_s = bp * t, bs * s
    xp = x_prompt.reshape(rows_p, d)
    xs = x_sample.reshape(rows_s, d)

    mod = _ada_call(jnp.concatenate([c_prompt, c_sample], axis=0), ada_w, ada_b)

    def mods(i, tm_s, reps=s, pad_to=None):
        out_p, out_s = [], []
        for j in range(6):
            m = mod[i, :, j * d:(j + 1) * d]
            out_p.append(m[:bp, None, :])
            out_s.append(m[bp:])
        return out_p, out_s

    def tok(m, tm):
        return _per_token(m, s, tm)

    tm_s512 = min(512, rows_s)
    tm_s1024 = min(1024, rows_s)
    conv_p = conv_s = pool_p = pool_s = mla_p = mla_s = None
    swa_p, swa_s = [None] * 3, [None] * 3

    for i in range(depth):
        kind, j = i % 4, i // 4
        (sh1p, sc1p, g1p, sh2p, sc2p, g2p), (sh1s, sc1s, g1s, sh2s, sc2s, g2s) = mods(i, None)
        n1 = norm1_g[i][None, :]
        if kind == 0:
            w_in = conv_w_in[j].astype(BF16)
            w_out = conv_w_out[j].astype(BF16)
            bv, cu = _conv_call(xp, n1, sh1p, sc1p, w_in, conv_k[j], None, t)
            conv_p = cu.reshape(bp, t, d)[:, t - (CONV_WIDTH - 1):]
            xp = _outproj_call(bv, w_out, xp, g1p, t)
            grp = CONV_GRP
            pad = jnp.zeros((bs, grp - s, d), F32)
            x_ext = jnp.concatenate([pad, xs.reshape(bs, s, d)], axis=1).reshape(bs * grp, d)
            ov = jnp.concatenate([jnp.zeros((bs, grp - s - 2, d), F32), state_conv[j], jnp.zeros((bs, s, d), F32)],
                                 axis=1).reshape(bs * grp, d)
            tme = min(512, bs * grp)
            she = _per_token(sh1s, grp, tme)
            sce = _per_token(sc1s, grp, tme)
            bv, cu = _conv_call(x_ext, n1, she, sce, w_in, conv_k[j], ov, grp)
            conv_s = cu.reshape(bs, grp, d)[:, grp - (CONV_WIDTH - 1):]
            bv = bv.reshape(bs, grp, d)[:, grp - s:].reshape(rows_s, d)
            xs = _outproj_call(bv, w_out, xs, tok(g1s, tm_s512), s)
        elif kind == 1:
            w_qkv = swa_w_qkv[j].astype(BF16)
            w_o = swa_w_o[j].astype(BF16)
            gn = jnp.concatenate([jnp.tile(swa_qn_g[j], (1, SWA_HEADS)), jnp.tile(swa_kn_g[j], (1, SWA_HEADS))],
                                 axis=0)[:, None, :]
            seg = (jnp.arange(SWA_WIDTH)[:, None] // SWA_HEAD_DIM == jnp.arange(SWA_WIDTH)[None, :] // SWA_HEAD_DIM)
            seg = (seg.astype(F32) / SWA_HEAD_DIM).astype(BF16)
            qkv = _swa_qkv_call(xp, n1, sh1p, sc1p, w_qkv, gn, seg, t)
            os_, ls_ = [], []
            for g, (win, r) in enumerate(SWA_GROUPS):
                o_g, l_g = _swa_band_call(qkv.reshape(bp, t // r, r * 9 * SWA_WIDTH), g, r, bp, t)
                os_.append(o_g)
                ls_.append(l_g)
            xp_new = _swa_merge_call(os_, ls_, w_o, xp, g1p, t)
            q5 = qkv.reshape(bp, t, 3, 3, SWA_HEADS, SWA_HEAD_DIM)
            for g, (win, r) in enumerate(SWA_GROUPS):
                keep = min(win, t)
                swa_p[g] = jnp.stack([q5[:, t - keep:, 1, g], q5[:, t - keep:, 2, g]], axis=2)[None]
            xp = xp_new
            qkv_s = _swa_qkv_call(xs, n1, tok(sh1s, tm_s1024), tok(sc1s, tm_s1024), w_qkv, gn, seg, s)
            sts = [st[j] for st in swa_states]
            o_s = _swa_decode_call(qkv_s, sts, bs, s)
            q5 = qkv_s.reshape(bs, s, 3, 3, SWA_HEADS, SWA_HEAD_DIM)
            for g, (win, r) in enumerate(SWA_GROUPS):
                new = jnp.stack([q5[:, :, 1, g], q5[:, :, 2, g]], axis=2)
                swa_s[g] = jnp.concatenate([sts[g][:, s:], new], axis=1)[None]
            xs = _outproj_call(o_s, w_o, xs, tok(g1s, tm_s512), s)
        elif kind == 2:
            pw = pool_w[j].astype(BF16)
            ps = pool_scale[j][None, :]
            xp, tail = _pool_call(xp, n1, sh1p, sc1p, g1p, pw, ps, t, False, 0)
            tpt = t // min(512, rows_p)
            pool_p = tail.reshape(bp, tpt, 16, d)[:, -1, 1:]
            grp = POOL_GRP
            x_ext = jnp.concatenate([jnp.zeros((bs, 1, d), F32), state_pool[j], xs.reshape(bs, s, d),
                                     jnp.zeros((bs, grp - 16 - s, d), F32)], axis=1).reshape(bs * grp, d)
            tme = min(512, bs * grp)
            y_ext, ext = _pool_call(x_ext, n1, _per_token(sh1s, grp, tme), _per_token(sc1s, grp, tme),
                                    _per_token(g1s, grp, tme), pw, ps, grp, True, past)
            xs = y_ext.reshape(bs, grp, d)[:, 16:16 + s].reshape(rows_s, d)
            pool_s = ext.reshape(bs, grp, d)[:, 16 + s - POOL_HIST:16 + s]
        else:
            wts = _mla_weights(mla_w_qa[j], mla_g_qa[j], mla_w_qb[j], mla_w_kva[j], mla_g_kva[j], mla_w_kvb[j],
                               mla_g_qn[j], mla_g_kn[j], mla_g_qr[j], mla_g_kr[j])
            w_o = mla_w_o[j].astype(BF16)
            cos_p, sin_p = _rope_tables(jnp.arange(t))
            lat_p, qcat, kcat, v = _mla_proj_call(xp, n1, sh1p, sc1p, wts, cos_p, sin_p, t, False)
            o_p = _mla_flash_call(qcat, kcat, v, bp, t)
            xp = _outproj_call(o_p, w_o, xp, g1p, t)
            mla_p = lat_p.reshape(bp, t, KV_LORA + QK_ROPE)
            cos_s, sin_s = _rope_tables(past + jnp.arange(s))
            reps = tm_s512 // s
            cos_s, sin_s = jnp.tile(cos_s, (reps, 1)), jnp.tile(sin_s, (reps, 1))
            lat_s, qabs, qpe = _mla_proj_call(xs, n1, tok(sh1s, tm_s512), tok(sc1s, tm_s512), wts, cos_s, sin_s, s, True)
            o_s = _mla_decode_call(cache_mla[j], page_table, qabs, qpe, lat_s, wts["wkt_flat"], wts["wv"], bs, s)
            xs = _outproj_call(o_s, w_o, xs, tok(g1s, tm_s512), s)
            mla_s = lat_s.reshape(bs, s, KV_LORA + QK_ROPE)
        n2 = norm2_g[i][None, :]
        w1 = mlp_w1[i].astype(BF16)
        w2 = mlp_w2[i].astype(BF16)
        xp = _mlp_call(xp, n2, sh2p, sc2p, g2p, w1, w2, t)
        xs = _mlp_call(xs, n2, tok(sh2s, tm_s1024), tok(sc2s, tm_s1024), tok(g2s, tm_s1024), w1, w2, s)

    return (xp.reshape(bp, t, d), xs.reshape(bs, s, d), conv_p[None], conv_s[None],
            swa_p[0], swa_s[0], swa_p[1], swa_s[1], swa_p[2], swa_s[2],
            pool_p[None], pool_s[None], mla_p[None], mla_s[None])
```

```python
import functools

import jax
import jax.numpy as jnp
from jax import lax
from jax.experimental import pallas as pl
from jax.experimental.pallas import tpu as pltpu

F32 = jnp.float32
BF16 = jnp.bfloat16

EPS = 1e-6
NEG = -1e30

CONV_WIDTH = 3
SWA_GROUPS = ((128, 1), (512, 4), (2048, 16))
SWA_HEADS = 8
SWA_HEAD_DIM = 64
SWA_WIDTH = SWA_HEADS * SWA_HEAD_DIM
SWA_BLK = 128
SWA_SCALE = SWA_HEAD_DIM ** -0.5
POOL_WINDOWS = (2, 4, 8, 16)
POOL_HIST = 15
MLA_HEADS = 8
Q_LORA = 384
KV_LORA = 256
QK_NOPE = 128
QK_ROPE = 64
V_DIM = 128
MLA_SCALE = (QK_NOPE + QK_ROPE) ** -0.5
ROPE_BASE = 10000.0
PAGE_SIZE = 128

VMEM_LIMIT_BYTES = 56 * 1024 * 1024
CONV_GRP = 16
POOL_GRP = 32


def _params(*sem):
    return pltpu.CompilerParams(dimension_semantics=sem, vmem_limit_bytes=VMEM_LIMIT_BYTES)


def _rms_mod(x, g, shift, scale):
    y = x * lax.rsqrt(jnp.mean(x * x, axis=-1, keepdims=True) + EPS)
    return (y * g) * (1.0 + scale) + shift


def _dot(a, b):
    return jnp.dot(a, b, preferred_element_type=F32)


def _dot_nt(a, b):
    return lax.dot_general(a, b, (((1,), (1,)), ((), ())), preferred_element_type=F32)


def _mod_spec(m, tm, rows_per_group):
    d = m.shape[-1]
    if m.shape[1] == 1:
        tpg = rows_per_group // tm
        return pl.BlockSpec((1, 1, d), lambda i, *_: (i // tpg, 0, 0))
    return pl.BlockSpec((1, tm, d), lambda i, *_: (i, 0, 0))


def _per_token(m, reps, tm):
    e = jnp.repeat(m, reps, axis=0)
    return e.reshape(e.shape[0] // tm, tm, e.shape[1])


def _ada_kernel(c_ref, w_ref, b_ref, o_ref):
    c = c_ref[...]
    s = (c * (1.0 / (1.0 + jnp.exp(-c)))).astype(BF16)
    o_ref[0] = _dot(s, w_ref[0].astype(BF16)) + b_ref[0]


def _ada_call(c, ada_w, ada_b):
    depth, d, n6 = ada_w.shape
    n = c.shape[0]
    tn = 1536
    return pl.pallas_call(
        _ada_kernel,
        grid=(depth, n6 // tn),
        in_specs=[pl.BlockSpec((n, d), lambda i, j: (0, 0)),
                  pl.BlockSpec((1, d, tn), lambda i, j: (i, 0, j)),
                  pl.BlockSpec((1, 1, tn), lambda i, j: (i, 0, j))],
        out_specs=pl.BlockSpec((1, n, tn), lambda i, j: (i, 0, j)),
        out_shape=jax.ShapeDtypeStruct((depth, n, n6), F32),
        compiler_params=_params("parallel", "parallel"),
        name="adaln",
    )(c, ada_w, ada_b.reshape(depth, 1, n6))


def _mlp_kernel(x_ref, g_ref, sh_ref, sc_ref, gt_ref, w1_ref, w2_ref, o_ref, h_scr, acc_scr):
    j = pl.program_id(1)

    @pl.when(j == 0)
    def _():
        h_scr[...] = _rms_mod(x_ref[...], g_ref[...], sh_ref[0], sc_ref[0]).astype(BF16)
        acc_scr[...] = jnp.zeros_like(acc_scr)

    a = _dot(h_scr[...], w1_ref[...])
    a = jnp.square(jnp.maximum(a, 0.0)).astype(BF16)
    acc_scr[...] += _dot(a, w2_ref[...])

    @pl.when(j == pl.num_programs(1) - 1)
    def _():
        o_ref[...] = x_ref[...] + gt_ref[0] * acc_scr[...]


def _mlp_call(x, g, sh, sc, gt, w1, w2, rows_per_group):
    rows, d = x.shape
    f = w1.shape[1]
    tm = min(1024, rows)
    tf = 512
    ms = _mod_spec(sh, tm, rows_per_group)
    return pl.pallas_call(
        _mlp_kernel,
        grid=(rows // tm, f // tf),
        in_specs=[pl.BlockSpec((tm, d), lambda i, j: (i, 0)),
                  pl.BlockSpec((1, d), lambda i, j: (0, 0)),
                  ms, ms, ms,
                  pl.BlockSpec((d, tf), lambda i, j: (0, j)),
                  pl.BlockSpec((tf, d), lambda i, j: (j, 0))],
        out_specs=pl.BlockSpec((tm, d), lambda i, j: (i, 0)),
        out_shape=jax.ShapeDtypeStruct((rows, d), F32),
        scratch_shapes=[pltpu.VMEM((tm, d), BF16), pltpu.VMEM((tm, d), F32)],
        compiler_params=_params("parallel", "arbitrary"),
        name="mlp",
    )(x, g, sh, sc, gt, w1, w2)


def _outproj_kernel(a_ref, w_ref, x_ref, gt_ref, o_ref):
    o_ref[...] = x_ref[...] + gt_ref[0] * _dot(a_ref[...], w_ref[...])


def _outproj_call(a, w, x, gt, rows_per_group):
    rows, d = x.shape
    k = a.shape[1]
    tm = min(512, rows)
    return pl.pallas_call(
        _outproj_kernel,
        grid=(rows // tm,),
        in_specs=[pl.BlockSpec((tm, k), lambda i: (i, 0)),
                  pl.BlockSpec((k, d), lambda i: (0, 0)),
                  pl.BlockSpec((tm, d), lambda i: (i, 0)),
                  _mod_spec(gt, tm, rows_per_group)],
        out_specs=pl.BlockSpec((tm, d), lambda i: (i, 0)),
        out_shape=jax.ShapeDtypeStruct((rows, d), F32),
        compiler_params=_params("parallel"),
        name="outproj",
    )(a, w, x, gt)


def _conv_kernel(*refs, tm, d, tiles_per_seq, has_override):
    if has_override:
        x_ref, g_ref, sh_ref, sc_ref, w_ref, k_ref, ov_ref, bv_ref, cu_ref, e_scr = refs
    else:
        x_ref, g_ref, sh_ref, sc_ref, w_ref, k_ref, bv_ref, cu_ref, e_scr = refs
    i = pl.program_id(0)
    h = _rms_mod(x_ref[...], g_ref[...], sh_ref[0], sc_ref[0]).astype(BF16)
    proj = _dot(h, w_ref[...])
    b = proj[:, :d]
    cu = proj[:, d:2 * d] * proj[:, 2 * d:]
    if has_override:
        row = lax.broadcasted_iota(jnp.int32, (tm, 1), 0)
        cu = jnp.where((row % CONV_GRP) >= CONV_GRP // 2, cu, ov_ref[...])
        e_scr[0:8, :] = jnp.zeros((8, d), F32)
    else:
        @pl.when(i % tiles_per_seq == 0)
        def _():
            e_scr[0:8, :] = jnp.zeros((8, d), F32)

        @pl.when(i % tiles_per_seq != 0)
        def _():
            e_scr[0:8, :] = e_scr[tm:tm + 8, :]
    e_scr[8:tm + 8, :] = cu
    cu1 = e_scr[7:tm + 7, :]
    cu2 = e_scr[6:tm + 6, :]
    k = k_ref[...]
    v = k[0:1, :] * cu2 + k[1:2, :] * cu1 + k[2:3, :] * cu
    bv_ref[...] = (b * v).astype(BF16)
    cu_ref[...] = cu


def _conv_call(x, g, sh, sc, w_in, k, override, rows_per_group):
    rows, d = x.shape
    tm = min(512, rows)
    has_override = override is not None
    ms = _mod_spec(sh, tm, rows_per_group)
    in_specs = [pl.BlockSpec((tm, d), lambda i: (i, 0)),
                pl.BlockSpec((1, d), lambda i: (0, 0)),
                ms, ms,
                pl.BlockSpec((d, 3 * d), lambda i: (0, 0)),
                pl.BlockSpec((CONV_WIDTH, d), lambda i: (0, 0))]
    args = [x, g, sh, sc, w_in, k]
    if has_override:
        in_specs.append(pl.BlockSpec((tm, d), lambda i: (i, 0)))
        args.append(override)
    kern = functools.partial(_conv_kernel, tm=tm, d=d, tiles_per_seq=max(rows_per_group // tm, 1),
                             has_override=has_override)
    return pl.pallas_call(
        kern,
        grid=(rows // tm,),
        in_specs=in_specs,
        out_specs=[pl.BlockSpec((tm, d), lambda i: (i, 0)), pl.BlockSpec((tm, d), lambda i: (i, 0))],
        out_shape=[jax.ShapeDtypeStruct((rows, d), BF16), jax.ShapeDtypeStruct((rows, d), F32)],
        scratch_shapes=[pltpu.VMEM((tm + 8, d), F32)],
        compiler_params=_params("arbitrary"),
        name="conv_in",
    )(*args)


def _pool_kernel(*refs, tm, d, tiles_per_seq, sample, pos0):
    x_ref, g_ref, sh_ref, sc_ref, gt_ref, w_ref, ps_ref, o_ref, st_ref, e_scr = refs
    i = pl.program_id(0)
    x = x_ref[...]
    h = _rms_mod(x, g_ref[...], sh_ref[0], sc_ref[0])
    row = lax.broadcasted_iota(jnp.int32, (tm, 1), 0)
    if sample:
        local = (row % POOL_GRP) - POOL_GRP // 2
        h = jnp.where(local >= 0, h, x)
        e_scr[0:16, :] = jnp.zeros((16, d), F32)
    else:
        local = (i % tiles_per_seq) * tm + row

        @pl.when(i % tiles_per_seq == 0)
        def _():
            e_scr[0:16, :] = jnp.zeros((16, d), F32)

        @pl.when(i % tiles_per_seq != 0)
        def _():
            e_scr[0:16, :] = e_scr[tm:tm + 16, :]
    e_scr[16:tm + 16, :] = h
    if sample:
        st_ref[...] = h
    else:
        st_ref[0] = e_scr[tm:tm + 16, :]
    gq = d // len(POOL_WINDOWS)
    for gi, w in enumerate(POOL_WINDOWS):
        lo = gi * gq
        win = h[:, lo:lo + gq]
        for j in range(1, w):
            win = win + e_scr[16 - j:16 - j + tm, lo:lo + gq]
        cnt = jnp.clip(pos0 + local + 1, 1, w).astype(F32)
        dd = (win / cnt - h[:, lo:lo + gq]).astype(BF16)
        y = _dot(dd, w_ref[gi]) * ps_ref[:, lo:lo + gq]
        o_ref[:, lo:lo + gq] = x[:, lo:lo + gq] + gt_ref[0][:, lo:lo + gq] * y


def _pool_call(x, g, sh, sc, gt, w, ps, rows_per_group, sample, pos0):
    rows, d = x.shape
    tm = min(512, rows)
    ms = _mod_spec(sh, tm, rows_per_group)
    gq = d // len(POOL_WINDOWS)
    if sample:
        st_spec = pl.BlockSpec((tm, d), lambda i: (i, 0))
        st_shape = jax.ShapeDtypeStruct((rows, d), F32)
    else:
        st_spec = pl.BlockSpec((1, 16, d), lambda i: (i, 0, 0))
        st_shape = jax.ShapeDtypeStruct((rows // tm, 16, d), F32)
    kern = functools.partial(_pool_kernel, tm=tm, d=d, tiles_per_seq=max(rows_per_group // tm, 1),
                             sample=sample, pos0=pos0)
    return pl.pallas_call(
        kern,
        grid=(rows // tm,),
        in_specs=[pl.BlockSpec((tm, d), lambda i: (i, 0)),
                  pl.BlockSpec((1, d), lambda i: (0, 0)),
                  ms, ms, ms,
                  pl.BlockSpec((len(POOL_WINDOWS), gq, gq), lambda i: (0, 0, 0)),
                  pl.BlockSpec((1, d), lambda i: (0, 0))],
        out_specs=[pl.BlockSpec((tm, d), lambda i: (i, 0)), st_spec],
        out_shape=[jax.ShapeDtypeStruct((rows, d), F32), st_shape],
        scratch_shapes=[pltpu.VMEM((tm + 16, d), F32)],
        compiler_params=_params("arbitrary"),
        name="pool",
    )(x, g, sh, sc, gt, w, ps)


def _swa_qkv_kernel(*refs, tm, dil):
    x_ref, g_ref, sh_ref, sc_ref, w_ref, gn_ref, seg_ref = refs[:7]
    outs = refs[7:16]
    z_scr = refs[16]
    w = SWA_WIDTH
    h = _rms_mod(x_ref[...], g_ref[...], sh_ref[0], sc_ref[0]).astype(BF16)
    for jb in range(9):
        which, g = divmod(jb, 3)
        z = _dot(h, w_ref[:, jb * w:(jb + 1) * w])
        if which < 2:
            ms = _dot((z * z).astype(BF16), seg_ref[...])
            z = z * lax.rsqrt(ms + EPS) * gn_ref[jb:jb + 1, :]
        r = dil[g]
        if r == 1:
            outs[jb][0, 0] = z
        else:
            for c in range(w // 128):
                z_scr[c] = z[:, c * 128:(c + 1) * 128]
            for rho in range(r):
                for c in range(w // 128):
                    outs[jb][0, rho, :, c * 128:(c + 1) * 128] = z_scr[c, pl.ds(rho, tm // r, stride=r), :]


def _swa_qkv_call(x, g, sh, sc, w, gn, seg, n_seq, seq_len, dil, rows_per_group):
    rows, d = x.shape
    wd = SWA_WIDTH
    tm = min(512, seq_len)
    tps = seq_len // tm
    ms = _mod_spec(sh, tm, rows_per_group)
    out_specs, out_shape = [], []
    for jb in range(9):
        r = dil[jb % 3]
        out_specs.append(pl.BlockSpec((1, r, tm // r, wd), lambda i: (i // tps, 0, i % tps, 0)))
        out_shape.append(jax.ShapeDtypeStruct((n_seq, r, seq_len // r, wd), F32))
    return pl.pallas_call(
        functools.partial(_swa_qkv_kernel, tm=tm, dil=dil),
        grid=(rows // tm,),
        in_specs=[pl.BlockSpec((tm, d), lambda i: (i, 0)),
                  pl.BlockSpec((1, d), lambda i: (0, 0)),
                  ms, ms,
                  pl.BlockSpec(w.shape, lambda i: (0, 0), pipeline_mode=pl.Buffered(1)),
                  pl.BlockSpec(gn.shape, lambda i: (0, 0)),
                  pl.BlockSpec(seg.shape, lambda i: (0, 0))],
        out_specs=out_specs,
        out_shape=out_shape,
        scratch_shapes=[pltpu.VMEM((wd // 128, tm, 128), F32)],
        compiler_params=_params("parallel"),
        name="swa_qkv",
    )(x, g, sh, sc, w, gn, seg)


def _swa_band_kernel(q_ref, kc_ref, kp_ref, vc_ref, vp_ref, o_ref, l_ref):
    b = pl.program_id(2)
    q = q_ref[0, 0].astype(BF16)
    kc = kc_ref[0, 0].astype(BF16)
    kp = kp_ref[0, 0].astype(BF16)
    vc = vc_ref[0, 0].astype(BF16)
    vp = vp_ref[0, 0].astype(BF16)
    qi = lax.broadcasted_iota(jnp.int32, (SWA_BLK, SWA_BLK), 0)
    kj = lax.broadcasted_iota(jnp.int32, (SWA_BLK, SWA_BLK), 1)
    cur_ok = kj <= qi
    prev_ok = jnp.logical_and(kj >= qi, b > 0)
    hd = SWA_HEAD_DIM
    for h in range(SWA_HEADS):
        sl = slice(h * hd, (h + 1) * hd)
        sc_ = jnp.where(cur_ok, _dot_nt(q[:, sl], kc[:, sl]) * SWA_SCALE, NEG)
        sp_ = jnp.where(prev_ok, _dot_nt(q[:, sl], kp[:, sl]) * SWA_SCALE, NEG)
        m = jnp.maximum(jnp.max(sc_, axis=-1, keepdims=True), jnp.max(sp_, axis=-1, keepdims=True))
        pc = jnp.exp(sc_ - m)
        pp = jnp.exp(sp_ - m)
        l = jnp.sum(pc, axis=-1, keepdims=True) + jnp.sum(pp, axis=-1, keepdims=True)
        o = (_dot(pc.astype(BF16), vc[:, sl]) + _dot(pp.astype(BF16), vp[:, sl])) / l
        o_ref[0, :, sl] = o
        l_ref[0, :, sl] = jnp.broadcast_to(m + jnp.log(l), (SWA_BLK, hd))


def _swa_band_call(q, k, v):
    n_seq, r, lr, w = q.shape
    nb = lr // SWA_BLK
    blk = (1, 1, SWA_BLK, w)
    cur = pl.BlockSpec(blk, lambda n, rho, b: (n, rho, b, 0))
    prev = pl.BlockSpec(blk, lambda n, rho, b: (n, rho, jnp.maximum(b - 1, 0), 0))
    out_spec = pl.BlockSpec((1, SWA_BLK, w), lambda n, rho, b: (n, b, rho))
    o, l = pl.pallas_call(
        _swa_band_kernel,
        grid=(n_seq, r, nb),
        in_specs=[cur, cur, prev, cur, prev],
        out_specs=[out_spec, out_spec],
        out_shape=[jax.ShapeDtypeStruct((n_seq, lr, r * w), F32)] * 2,
        compiler_params=_params("parallel", "parallel", "arbitrary"),
        name="swa_band",
    )(q, k, k, v, v)
    return o.reshape(n_seq * lr * r, w), l.reshape(n_seq * lr * r, w)


def _swa_merge_kernel(o0, o1, o2, l0, l1, l2, w_ref, x_ref, gt_ref, out_ref):
    a0, a1, a2 = l0[...], l1[...], l2[...]
    m = jnp.maximum(jnp.maximum(a0, a1), a2)
    e0, e1, e2 = jnp.exp(a0 - m), jnp.exp(a1 - m), jnp.exp(a2 - m)
    inv = 1.0 / (e0 + e1 + e2)
    merged = (e0 * inv) * o0[...] + (e1 * inv) * o1[...] + (e2 * inv) * o2[...]
    out_ref[...] = x_ref[...] + gt_ref[0] * _dot(merged.astype(BF16), w_ref[...])


def _swa_merge_call(os_, ls_, w_o, x, gt, rows_per_group):
    rows, d = x.shape
    tm = min(512, rows)
    w = SWA_WIDTH
    sp = pl.BlockSpec((tm, w), lambda i: (i, 0))
    return pl.pallas_call(
        _swa_merge_kernel,
        grid=(rows // tm,),
        in_specs=[sp] * 6 + [pl.BlockSpec((w, d), lambda i: (0, 0)),
                             pl.BlockSpec((tm, d), lambda i: (i, 0)),
                             _mod_spec(gt, tm, rows_per_group)],
        out_specs=pl.BlockSpec((tm, d), lambda i: (i, 0)),
        out_shape=jax.ShapeDtypeStruct((rows, d), F32),
        compiler_params=_params("parallel"),
        name="swa_merge",
    )(*os_, *ls_, w_o, x, gt)


def _swa_step_kernel(q_ref, st_ref, new_ref, ns_ref, o_ref, l_ref, *, win, r, lb, s_new):
    w = SWA_WIDTH
    hd = SWA_HEAD_DIM
    nrow = SWA_HEADS * s_new
    pad = 128 - s_new
    q = q_ref[0]
    rowi = lax.broadcasted_iota(jnp.int32, (nrow, w), 0)
    lanei = lax.broadcasted_iota(jnp.int32, (nrow, w), 1)
    diag = (rowi // s_new) == (lanei // hd)
    qbd = jnp.where(diag, jnp.concatenate([q] * SWA_HEADS, axis=0), 0.0).astype(BF16)
    kt = jnp.concatenate([st_ref[0, 0], new_ref[0, 0]], axis=1).astype(BF16)
    vt = jnp.concatenate([st_ref[0, 1], new_ref[0, 1]], axis=1).astype(BF16)
    sc = _dot(qbd, kt) * SWA_SCALE
    s_idx = lax.broadcasted_iota(jnp.int32, (nrow, lb + 128), 0) % s_new
    col = lax.broadcasted_iota(jnp.int32, (nrow, lb + 128), 1)
    dist = lb + s_idx - jnp.where(col < lb, col, col - pad)
    ok = jnp.logical_and(jnp.logical_and(dist >= 0, dist % r == 0), dist <= win)
    ok = jnp.logical_and(ok, jnp.logical_or(col < lb, col >= lb + pad))
    sc = jnp.where(ok, sc, NEG)
    m = jnp.max(sc, axis=-1, keepdims=True)
    p = jnp.exp(sc - m)
    l = jnp.sum(p, axis=-1, keepdims=True)
    acc = jnp.where(diag, _dot_nt(p.astype(BF16), vt) / l, 0.0)
    lse = jnp.where(diag, m + jnp.log(l), 0.0)
    o = acc[0:s_new]
    ls = lse[0:s_new]
    for h in range(1, SWA_HEADS):
        o = o + acc[h * s_new:(h + 1) * s_new]
        ls = ls + lse[h * s_new:(h + 1) * s_new]
    o_ref[0] = o
    l_ref[0] = ls
    for kv in range(2):
        ns_ref[0, kv, :, 0:lb - s_new] = st_ref[0, kv, :, s_new:lb]
        ns_ref[0, kv, :, lb - s_new:lb] = new_ref[0, kv, :, pad:128]


def _swa_step_call(q, st_t, new_t, win, r):
    n_seq, s_new, w = q.shape
    lb = st_t.shape[-1]
    return pl.pallas_call(
        functools.partial(_swa_step_kernel, win=win, r=r, lb=lb, s_new=s_new),
        grid=(n_seq,),
        in_specs=[pl.BlockSpec((1, s_new, w), lambda n: (n, 0, 0)),
                  pl.BlockSpec((1, 2, w, lb), lambda n: (n, 0, 0, 0)),
                  pl.BlockSpec((1, 2, w, 128), lambda n: (n, 0, 0, 0))],
        out_specs=[pl.BlockSpec((1, 2, w, lb), lambda n: (n, 0, 0, 0)),
                   pl.BlockSpec((1, s_new, w), lambda n: (n, 0, 0)),
                   pl.BlockSpec((1, s_new, w), lambda n: (n, 0, 0))],
        out_shape=[jax.ShapeDtypeStruct((n_seq, 2, w, lb), F32),
                   jax.ShapeDtypeStruct((n_seq, s_new, w), F32),
                   jax.ShapeDtypeStruct((n_seq, s_new, w), F32)],
        compiler_params=_params("parallel"),
        name="swa_step",
    )(q, st_t, new_t)


def _rope128(bn, cos_t, sin_t):
    return bn * cos_t + pltpu.roll(bn, 64, axis=1) * sin_t


def _mla_proj_kernel(*refs, decode):
    (x_ref, g_ref, sh_ref, sc_ref, w1_ref, gqa_ref, wqb_ref, gq_ref, gkv_ref, gkr_ref,
     cos_ref, sin_ref, wx_ref) = refs[:13]
    outs = refs[13:]
    nh = MLA_HEADS
    h = _rms_mod(x_ref[...], g_ref[...], sh_ref[0], sc_ref[0]).astype(BF16)
    a = _dot(h, w1_ref[...])
    qa = a[:, :Q_LORA]
    qa = (qa * lax.rsqrt(jnp.mean(qa * qa, axis=-1, keepdims=True) + EPS) * gqa_ref[...]).astype(BF16)
    q = _dot(qa, wqb_ref[...])
    cos_t = cos_ref[...]
    sin_t = sin_ref[...]
    latc = a[:, Q_LORA:Q_LORA + KV_LORA]
    latc = latc * lax.rsqrt(jnp.mean(latc * latc, axis=-1, keepdims=True) + EPS) * gkv_ref[...]
    kb = a[:, Q_LORA + KV_LORA:]
    kb = kb * lax.rsqrt(jnp.sum(kb * kb, axis=-1, keepdims=True) * (1.0 / QK_ROPE) + EPS) * gkr_ref[...]
    kb = _rope128(kb, cos_t, sin_t)
    lane = lax.broadcasted_iota(jnp.int32, kb.shape, 1)
    kstd = jnp.where(lane < 32, kb, pltpu.roll(kb, 96, axis=1))
    if decode:
        lat_ref, qabs_ref, qpe_ref = outs
    else:
        lat_ref, qcat_ref, kcat_ref, v_ref = outs
    lat_ref[:, 0:KV_LORA] = latc
    lat_ref[:, KV_LORA:KV_LORA + QK_ROPE] = kstd[:, 0:QK_ROPE]
    gq = gq_ref[...]
    for hh in range(nh):
        an = q[:, hh * 256:hh * 256 + 128]
        an = an * lax.rsqrt(jnp.mean(an * an, axis=-1, keepdims=True) + EPS) * gq[:, 0:128]
        bn = q[:, hh * 256 + 128:hh * 256 + 256]
        bn = bn * lax.rsqrt(jnp.sum(bn * bn, axis=-1, keepdims=True) * (1.0 / QK_ROPE) + EPS) * gq[:, 128:256]
        bn = _rope128(bn, cos_t, sin_t)
        if decode:
            ag = an * gq[:, 256:384]
            a_hi = ag.astype(BF16)
            a_lo = (ag - a_hi.astype(F32)).astype(BF16)
            qabs_ref[:, hh * 256:(hh + 1) * 256] = (_dot(a_hi, wx_ref[hh]) + _dot(a_lo, wx_ref[hh])).astype(BF16)
            qpe_ref[:, hh * 128:(hh + 1) * 128] = jnp.where(lane < 32, bn, pltpu.roll(bn, 96, axis=1)).astype(BF16)
        else:
            qcat_ref[:, hh * 256:hh * 256 + 128] = an.astype(BF16)
            qcat_ref[:, hh * 256 + 128:(hh + 1) * 256] = bn.astype(BF16)
    if not decode:
        gkn = gq_ref[:, 256:384]
        kvx = _dot(latc.astype(BF16), wx_ref[...])
        kbb = kb.astype(BF16)
        for hh in range(nh):
            kn = kvx[:, hh * 128:(hh + 1) * 128]
            kn = kn * lax.rsqrt(jnp.mean(kn * kn, axis=-1, keepdims=True) + EPS) * gkn
            kcat_ref[:, hh * 256:hh * 256 + 128] = kn.astype(BF16)
            kcat_ref[:, hh * 256 + 128:(hh + 1) * 256] = kbb
        v_ref[...] = kvx[:, nh * 128:].astype(BF16)


def _mla_proj_call(x, g, sh, sc, wts, cos_t, sin_t, rows_per_group, decode):
    rows, d = x.shape
    tm = min(512, rows)
    nh = MLA_HEADS
    ms = _mod_spec(sh, tm, rows_per_group)
    tps = max(rows_per_group // tm, 1)
    if decode:
        tab_spec = pl.BlockSpec((tm, 128), lambda i: (0, 0))
        wx = wts["wk_t"]
        wx_spec = pl.BlockSpec(wx.shape, lambda i: (0, 0, 0))
        out_specs = [pl.BlockSpec((tm, KV_LORA + QK_ROPE), lambda i: (i, 0)),
                     pl.BlockSpec((tm, nh * 256), lambda i: (i, 0)),
                     pl.BlockSpec((tm, nh * 128), lambda i: (i, 0))]
        out_shape = [jax.ShapeDtypeStruct((rows, KV_LORA + QK_ROPE), F32),
                     jax.ShapeDtypeStruct((rows, nh * 256), BF16),
                     jax.ShapeDtypeStruct((rows, nh * 128), BF16)]
    else:
        tab_spec = pl.BlockSpec((tm, 128), lambda i: (i % tps, 0))
        wx = wts["w_kvb"]
        wx_spec = pl.BlockSpec(wx.shape, lambda i: (0, 0))
        out_specs = [pl.BlockSpec((tm, KV_LORA + QK_ROPE), lambda i: (i, 0)),
                     pl.BlockSpec((tm, nh * 256), lambda i: (i, 0)),
                     pl.BlockSpec((tm, nh * 256), lambda i: (i, 0)),
                     pl.BlockSpec((tm, nh * 128), lambda i: (i, 0))]
        out_shape = [jax.ShapeDtypeStruct((rows, KV_LORA + QK_ROPE), F32),
                     jax.ShapeDtypeStruct((rows, nh * 256), BF16),
                     jax.ShapeDtypeStruct((rows, nh * 256), BF16),
                     jax.ShapeDtypeStruct((rows, nh * 128), BF16)]

    def full(a):
        return pl.BlockSpec(a.shape, lambda i: (0,) * a.ndim)

    return pl.pallas_call(
        functools.partial(_mla_proj_kernel, decode=decode),
        grid=(rows // tm,),
        in_specs=[pl.BlockSpec((tm, d), lambda i: (i, 0)),
                  pl.BlockSpec((1, d), lambda i: (0, 0)),
                  ms, ms,
                  full(wts["w1"]), full(wts["g_qa"]), full(wts["w_qb"]), full(wts["gq"]),
                  full(wts["g_kva"]), full(wts["g_kr"]),
                  tab_spec, tab_spec, wx_spec],
        out_specs=out_specs,
        out_shape=out_shape,
        compiler_params=_params("parallel"),
        name="mla_proj_decode" if decode else "mla_proj",
    )(x, g, sh, sc, wts["w1"], wts["g_qa"], wts["w_qb"], wts["gq"], wts["g_kva"], wts["g_kr"],
      cos_t, sin_t, wx)


def _mla_flash_kernel(q_ref, k_ref, v_ref, o_ref, m_scr, l_scr, acc_scr, *, tq):
    qi = pl.program_id(2)
    ki = pl.program_id(3)

    @pl.when(ki == 0)
    def _():
        m_scr[...] = jnp.full_like(m_scr, NEG)
        l_scr[...] = jnp.zeros_like(l_scr)
        acc_scr[...] = jnp.zeros_like(acc_scr)

    def step(masked):
        s = _dot_nt(q_ref[0], k_ref[0]) * MLA_SCALE
        if masked:
            r = lax.broadcasted_iota(jnp.int32, (tq, tq), 0)
            c = lax.broadcasted_iota(jnp.int32, (tq, tq), 1)
            s = jnp.where(c <= r, s, NEG)
        m_old = m_scr[...]
        m_new = jnp.maximum(m_old, jnp.max(s, axis=-1, keepdims=True))
        alpha = jnp.exp(m_old - m_new)
        p = jnp.exp(s - m_new)
        l_scr[...] = alpha * l_scr[...] + jnp.sum(p, axis=-1, keepdims=True)
        acc_scr[...] = alpha * acc_scr[...] + _dot(p.astype(BF16), v_ref[0])
        m_scr[...] = m_new

    @pl.when(ki < qi)
    def _():
        step(False)

    @pl.when(ki == qi)
    def _():
        step(True)
        o_ref[0] = (acc_scr[...] / l_scr[...]).astype(BF16)


def _mla_flash_call(qcat, kcat, v, n_seq, seq_len):
    nh = MLA_HEADS
    tq = min(1024, seq_len)
    nq = seq_len // tq
    q3 = qcat.reshape(n_seq, seq_len, nh * 256)
    k3 = kcat.reshape(n_seq, seq_len, nh * 256)
    v3 = v.reshape(n_seq, seq_len, nh * V_DIM)
    o = pl.pallas_call(
        functools.partial(_mla_flash_kernel, tq=tq),
        grid=(n_seq, nh, nq, nq),
        in_specs=[pl.BlockSpec((1, tq, 256), lambda n, h, qi, ki: (n, qi, h)),
                  pl.BlockSpec((1, tq, 256), lambda n, h, qi, ki: (n, jnp.minimum(ki, qi), h)),
                  pl.BlockSpec((1, tq, V_DIM), lambda n, h, qi, ki: (n, jnp.minimum(ki, qi), h))],
        out_specs=pl.BlockSpec((1, tq, V_DIM), lambda n, h, qi, ki: (n, qi, h)),
        out_shape=jax.ShapeDtypeStruct((n_seq, seq_len, nh * V_DIM), BF16),
        scratch_shapes=[pltpu.VMEM((tq, 1), F32), pltpu.VMEM((tq, 1), F32), pltpu.VMEM((tq, V_DIM), F32)],
        compiler_params=_params("parallel", "parallel", "parallel", "arbitrary"),
        name="mla_flash",
    )(q3, k3, v3)
    return o.reshape(n_seq * seq_len, nh * V_DIM)


def _mla_decode_kernel(pt_ref, *refs, n_pg, sub_pg, s_new):
    del pt_ref
    page_refs = refs[:n_pg]
    (qabs_ref, qpe_ref, latn_ref, wkt_ref, wv_ref, o_ref,
     lhs_scr, m_scr, l_scr, ctx_scr) = refs[n_pg:]
    c = pl.program_id(1)
    nh = MLA_HEADS
    nrow = nh * s_new
    nk = nh * QK_NOPE

    @pl.when(c == 0)
    def _():
        lhs_scr[0:nk, :] = wkt_ref[...]
        lhs_scr[nk:nk + nrow, :] = qabs_ref[0]
        m_scr[...] = jnp.full_like(m_scr, NEG)
        l_scr[...] = jnp.zeros_like(l_scr)
        ctx_scr[...] = jnp.zeros_like(ctx_scr)

    def process(pages, mask):
        latc = jnp.concatenate([pg[0:KV_LORA, :].astype(BF16) for pg in pages], axis=1)
        kpe = jnp.concatenate([pg[KV_LORA:KV_LORA + QK_ROPE, :].astype(BF16) for pg in pages], axis=1)
        r = _dot(lhs_scr[...], latc)
        spe = _dot(qpe_ref[0][:, 0:QK_ROPE], kpe)
        parts = []
        for h in range(nh):
            blk = r[h * QK_NOPE:(h + 1) * QK_NOPE]
            rstd = lax.rsqrt(jnp.sum(blk * blk, axis=0, keepdims=True) * (1.0 / QK_NOPE) + EPS)
            parts.append(r[nk + h * s_new:nk + (h + 1) * s_new] * rstd)
        s = (jnp.concatenate(parts, axis=0) + spe) * MLA_SCALE
        if mask is not None:
            s = jnp.where(mask, s, NEG)
        m_old = m_scr[...]
        m_new = jnp.maximum(m_old, jnp.max(s, axis=-1, keepdims=True))
        alpha = jnp.exp(m_old - m_new)
        p = jnp.exp(s - m_new)
        l_scr[...] = alpha * l_scr[...] + jnp.sum(p, axis=-1, keepdims=True)
        ctx_scr[...] = alpha * ctx_scr[...] + _dot_nt(p.astype(BF16), latc)
        m_scr[...] = m_new

    for lo in range(0, n_pg, sub_pg):
        process([pr[0, 0] for pr in page_refs[lo:lo + sub_pg]], None)

    @pl.when(c == pl.num_programs(1) - 1)
    def _():
        si = lax.broadcasted_iota(jnp.int32, (nrow, PAGE_SIZE), 0) % s_new
        kj = lax.broadcasted_iota(jnp.int32, (nrow, PAGE_SIZE), 1)
        process([latn_ref[0]], kj <= si)
        ctx = ctx_scr[...] / l_scr[...]
        for h in range(nh):
            ctx_h = ctx[h * s_new:(h + 1) * s_new].astype(BF16)
            o_ref[0, :, h * V_DIM:(h + 1) * V_DIM] = _dot(ctx_h, wv_ref[h]).astype(BF16)


def _mla_decode_call(cache_t, layer, page_table, qabs, qpe, lat_new, wkt, wv, n_seq, s_new):
    nh = MLA_HEADS
    n_pages = page_table.shape[1]
    n_pg = min(16, n_pages)
    sub_pg = min(8, n_pg)
    lw = KV_LORA + QK_ROPE
    nrow = nh * s_new

    def page_spec(p):
        return pl.BlockSpec((1, 1, lw, PAGE_SIZE),
                            lambda n, c, pt: (layer, pt[n * n_pages + c * n_pg + p], 0, 0))

    latn_t = jnp.transpose(lat_new.reshape(n_seq, s_new, lw), (0, 2, 1))
    latn_t = jnp.pad(latn_t, ((0, 0), (0, 0), (0, PAGE_SIZE - s_new)))
    grid_spec = pltpu.PrefetchScalarGridSpec(
        num_scalar_prefetch=1,
        grid=(n_seq, n_pages // n_pg),
        in_specs=[page_spec(p) for p in range(n_pg)] + [
            pl.BlockSpec((1, nrow, 256), lambda n, c, pt: (n, 0, 0)),
            pl.BlockSpec((1, nrow, 128), lambda n, c, pt: (n, 0, 0)),
            pl.BlockSpec((1, lw, PAGE_SIZE), lambda n, c, pt: (n, 0, 0)),
            pl.BlockSpec(wkt.shape, lambda n, c, pt: (0, 0)),
            pl.BlockSpec(wv.shape, lambda n, c, pt: (0, 0, 0))],
        out_specs=pl.BlockSpec((1, s_new, nh * V_DIM), lambda n, c, pt: (n, 0, 0)),
        scratch_shapes=[pltpu.VMEM((nh * QK_NOPE + nrow, KV_LORA), BF16),
                        pltpu.VMEM((nrow, 1), F32), pltpu.VMEM((nrow, 1), F32),
                        pltpu.VMEM((nrow, KV_LORA), F32)])
    o = pl.pallas_call(
        functools.partial(_mla_decode_kernel, n_pg=n_pg, sub_pg=sub_pg, s_new=s_new),
        grid_spec=grid_spec,
        out_shape=jax.ShapeDtypeStruct((n_seq, s_new, nh * V_DIM), BF16),
        compiler_params=_params("parallel", "arbitrary"),
        name="mla_decode",
    )(page_table.reshape(-1), *([cache_t] * n_pg),
      jnp.transpose(qabs.reshape(n_seq, s_new, nh, 256), (0, 2, 1, 3)).reshape(n_seq, nrow, 256),
      jnp.transpose(qpe.reshape(n_seq, s_new, nh, 128), (0, 2, 1, 3)).reshape(n_seq, nrow, 128),
      latn_t, wkt, wv)
    return o.reshape(n_seq * s_new, nh * V_DIM)


def _mla_weights(w_qa, g_qa, w_qb, w_kva, g_kva, w_kvb, g_qn, g_kn, g_qr, g_kr):
    nh = MLA_HEADS
    half = QK_ROPE // 2

    def pad_rope_cols(w):
        z = jnp.zeros(w.shape[:-1] + (half,), w.dtype)
        return jnp.concatenate([w[..., :half], z, w[..., half:], z], axis=-1)

    w_kva_p = jnp.concatenate([w_kva[:, :KV_LORA], pad_rope_cols(w_kva[:, KV_LORA:])], axis=1)
    w1 = jnp.concatenate([w_qa, w_kva_p], axis=1).astype(BF16)
    wq = w_qb.reshape(Q_LORA, nh, QK_NOPE + QK_ROPE)
    wq = jnp.concatenate([wq[..., :QK_NOPE], pad_rope_cols(wq[..., QK_NOPE:])], axis=-1)
    w_qb_p = wq.reshape(Q_LORA, nh * 256).astype(BF16)
    gq = jnp.concatenate([g_qn, pad_rope_cols(g_qr), g_kn])[None, :]
    wkv = w_kvb.reshape(KV_LORA, nh, QK_NOPE + V_DIM)
    w_kvb_p = jnp.concatenate([wkv[..., :QK_NOPE].reshape(KV_LORA, nh * QK_NOPE),
                               wkv[..., QK_NOPE:].reshape(KV_LORA, nh * V_DIM)], axis=1).astype(BF16)
    wk_t = jnp.transpose(wkv[..., :QK_NOPE], (1, 2, 0)).astype(BF16)
    return dict(w1=w1, g_qa=g_qa[None, :], w_qb=w_qb_p, gq=gq, g_kva=g_kva[None, :],
                g_kr=pad_rope_cols(g_kr)[None, :], w_kvb=w_kvb_p, wk_t=wk_t,
                wkt_flat=wk_t.reshape(nh * QK_NOPE, KV_LORA),
                wv=jnp.transpose(wkv[..., QK_NOPE:], (1, 0, 2)).astype(BF16))


def _rope_tables(pos):
    half = QK_ROPE // 2
    inv = ROPE_BASE ** (-jnp.arange(half, dtype=F32) / half)
    ang = pos.astype(F32)[:, None] * inv[None, :]
    cos, sin = jnp.cos(ang), jnp.sin(ang)
    z = jnp.zeros_like(cos)
    return (jnp.concatenate([cos, z, cos, z], axis=1), jnp.concatenate([-sin, z, sin, z], axis=1))


def kernel(x_prompt, x_sample, state_conv, state_swa0, state_swa1, state_swa2, state_pool, cache_mla, page_table,
           c_prompt, c_sample, ada_w, ada_b, norm1_g, norm2_g, mlp_w1, mlp_w2, conv_w_in, conv_k, conv_w_out,
           swa_w_qkv, swa_qn_g, swa_kn_g, swa_w_o, pool_w, pool_scale, mla_w_qa, mla_g_qa, mla_w_qb, mla_w_kva,
           mla_g_kva, mla_w_kvb, mla_g_qn, mla_g_kn, mla_g_qr, mla_g_kr, mla_w_o):
    bp, t, d = x_prompt.shape
    bs, s, _ = x_sample.shape
    depth = ada_w.shape[0]
    past = page_table.shape[1] * PAGE_SIZE
    swa_states = (state_swa0, state_swa1, state_swa2)
    for (win, _), st in zip(SWA_GROUPS, swa_states):
        assert st.shape[2] == win, "sliding-window state must hold a full window"
    assert t % (SWA_GROUPS[-1][1] * SWA_BLK) == 0, "prompt length must split into full dilated blocks"
    rows_p, rows_s = bp * t, bs * s
    xp = x_prompt.reshape(rows_p, d)
    xs = x_sample.reshape(rows_s, d)

    mod = _ada_call(jnp.concatenate([c_prompt, c_sample], axis=0), ada_w, ada_b)

    def mods(i):
        out_p, out_s = [], []
        for j in range(6):
            m = mod[i, :, j * d:(j + 1) * d]
            out_p.append(m[:bp, None, :])
            out_s.append(m[bp:])
        return out_p, out_s

    def tok(m, tm):
        return _per_token(m, s, tm)

    tm_s512 = min(512, rows_s)
    tm_s1024 = min(1024, rows_s)
    conv_p = conv_s = pool_p = pool_s = mla_p = mla_s = None
    swa_p, swa_s = [None] * 3, [None] * 3

    for i in range(depth):
        kind, j = i % 4, i // 4
        (sh1p, sc1p, g1p, sh2p, sc2p, g2p), (sh1s, sc1s, g1s, sh2s, sc2s, g2s) = mods(i)
        n1 = norm1_g[i][None, :]
        if kind == 0:
            w_in = conv_w_in[j].astype(BF16)
            w_out = conv_w_out[j].astype(BF16)
            bv, cu = _conv_call(xp, n1, sh1p, sc1p, w_in, conv_k[j], None, t)
            conv_p = cu.reshape(bp, t, d)[:, t - (CONV_WIDTH - 1):]
            xp = _outproj_call(bv, w_out, xp, g1p, t)
            grp = CONV_GRP
            pad = jnp.zeros((bs, grp - s, d), F32)
            x_ext = jnp.concatenate([pad, xs.reshape(bs, s, d)], axis=1).reshape(bs * grp, d)
            ov = jnp.concatenate([jnp.zeros((bs, grp - s - 2, d), F32), state_conv[j], jnp.zeros((bs, s, d), F32)],
                                 axis=1).reshape(bs * grp, d)
            tme = min(512, bs * grp)
            she = _per_token(sh1s, grp, tme)
            sce = _per_token(sc1s, grp, tme)
            bv, cu = _conv_call(x_ext, n1, she, sce, w_in, conv_k[j], ov, grp)
            conv_s = cu.reshape(bs, grp, d)[:, grp - (CONV_WIDTH - 1):]
            bv = bv.reshape(bs, grp, d)[:, grp - s:].reshape(rows_s, d)
            xs = _outproj_call(bv, w_out, xs, tok(g1s, tm_s512), s)
        elif kind == 1:
            w_qkv = swa_w_qkv[j].astype(BF16)
            w_o = swa_w_o[j].astype(BF16)
            gn = jnp.concatenate([jnp.tile(swa_qn_g[j], (1, SWA_HEADS)), jnp.tile(swa_kn_g[j], (1, SWA_HEADS))],
                                 axis=0)
            seg = (jnp.arange(SWA_WIDTH)[:, None] // SWA_HEAD_DIM == jnp.arange(SWA_WIDTH)[None, :] // SWA_HEAD_DIM)
            seg = (seg.astype(F32) / SWA_HEAD_DIM).astype(BF16)
            dil = tuple(r for _, r in SWA_GROUPS)
            qkv = _swa_qkv_call(xp, n1, sh1p, sc1p, w_qkv, gn, seg, bp, t, dil, t)
            os_, ls_ = [], []
            for g, (win, r) in enumerate(SWA_GROUPS):
                o_g, l_g = _swa_band_call(qkv[g], qkv[3 + g], qkv[6 + g])
                os_.append(o_g)
                ls_.append(l_g)
                keep = min(win, t)
                kv = [jnp.transpose(a, (0, 2, 1, 3)).reshape(bp, t, SWA_HEADS, SWA_HEAD_DIM)[:, t - keep:]
                      for a in (qkv[3 + g], qkv[6 + g])]
                swa_p[g] = jnp.stack(kv, axis=2)[None]
            xp = _swa_merge_call(os_, ls_, w_o, xp, g1p, t)
            tm_q = min(512, rows_s)
            qkv_s = _swa_qkv_call(xs, n1, tok(sh1s, tm_q), tok(sc1s, tm_q), w_qkv, gn, seg, 1, rows_s,
                                  (1, 1, 1), s)
            os_, ls_ = [], []
            for g, (win, r) in enumerate(SWA_GROUPS):
                q_g, k_g, v_g = (qkv_s[c * 3 + g].reshape(bs, s, SWA_WIDTH) for c in range(3))
                st_t = jnp.transpose(swa_states[g][j], (0, 2, 3, 4, 1)).reshape(bs, 2, SWA_WIDTH, win)
                new_t = jnp.transpose(jnp.stack([k_g, v_g], axis=1), (0, 1, 3, 2))
                new_t = jnp.pad(new_t, ((0, 0), (0, 0), (0, 0), (128 - s, 0)))
                ns, o_g, l_g = _swa_step_call(q_g, st_t, new_t, win, r)
                swa_s[g] = jnp.transpose(ns.reshape(bs, 2, SWA_HEADS, SWA_HEAD_DIM, win), (0, 4, 1, 2, 3))[None]
                os_.append(o_g.reshape(rows_s, SWA_WIDTH))
                ls_.append(l_g.reshape(rows_s, SWA_WIDTH))
            xs = _swa_merge_call(os_, ls_, w_o, xs, tok(g1s, tm_s512), s)
        elif kind == 2:
            pw = pool_w[j].astype(BF16)
            ps = pool_scale[j][None, :]
            xp, tail = _pool_call(xp, n1, sh1p, sc1p, g1p, pw, ps, t, False, 0)
            tpt = t // min(512, rows_p)
            pool_p = tail.reshape(bp, tpt, 16, d)[:, -1, 1:]
            grp = POOL_GRP
            x_ext = jnp.concatenate([jnp.zeros((bs, 1, d), F32), state_pool[j], xs.reshape(bs, s, d),
                                     jnp.zeros((bs, grp - 16 - s, d), F32)], axis=1).reshape(bs * grp, d)
            tme = min(512, bs * grp)
            y_ext, ext = _pool_call(x_ext, n1, _per_token(sh1s, grp, tme), _per_token(sc1s, grp, tme),
                                    _per_token(g1s, grp, tme), pw, ps, grp, True, past)
            xs = y_ext.reshape(bs, grp, d)[:, 16:16 + s].reshape(rows_s, d)
            pool_s = ext.reshape(bs, grp, d)[:, 16 + s - POOL_HIST:16 + s]
        else:
            wts = _mla_weights(mla_w_qa[j], mla_g_qa[j], mla_w_qb[j], mla_w_kva[j], mla_g_kva[j], mla_w_kvb[j],
                               mla_g_qn[j], mla_g_kn[j], mla_g_qr[j], mla_g_kr[j])
            w_o = mla_w_o[j].astype(BF16)
            cos_p, sin_p = _rope_tables(jnp.arange(t))
            lat_p, qcat, kcat, v = _mla_proj_call(xp, n1, sh1p, sc1p, wts, cos_p, sin_p, t, False)
            o_p = _mla_flash_call(qcat, kcat, v, bp, t)
            xp = _outproj_call(o_p, w_o, xp, g1p, t)
            mla_p = lat_p.reshape(bp, t, KV_LORA + QK_ROPE)
            cos_s, sin_s = _rope_tables(past + jnp.arange(s))
            reps = tm_s512 // s
            cos_s, sin_s = jnp.tile(cos_s, (reps, 1)), jnp.tile(sin_s, (reps, 1))
            lat_s, qabs, qpe = _mla_proj_call(xs, n1, tok(sh1s, tm_s512), tok(sc1s, tm_s512), wts, cos_s, sin_s, s, True)
            cache_t = jnp.transpose(cache_mla, (0, 1, 3, 2))
            o_s = _mla_decode_call(cache_t, j, page_table, qabs, qpe, lat_s, wts["wkt_flat"], wts["wv"], bs, s)
            xs = _outproj_call(o_s, w_o, xs, tok(g1s, tm_s512), s)
            mla_s = lat_s.reshape(bs, s, KV_LORA + QK_ROPE)
        n2 = norm2_g[i][None, :]
        w1 = mlp_w1[i].astype(BF16)
        w2 = mlp_w2[i].astype(BF16)
        xp = _mlp_call(xp, n2, sh2p, sc2p, g2p, w1, w2, t)
        xs = _mlp_call(xs, n2, tok(sh2s, tm_s1024), tok(sc2s, tm_s1024), tok(g2s, tm_s1024), w1, w2, s)

    return (xp.reshape(bp, t, d), xs.reshape(bs, s, d), conv_p[None], conv_s[None],
            swa_p[0], swa_s[0], swa_p[1], swa_s[1], swa_p[2], swa_s[2],
            pool_p[None], pool_s[None], mla_p[None], mla_s[None])
```

```python
import functools

import jax
import jax.numpy as jnp
from jax import lax
from jax.experimental import pallas as pl
from jax.experimental.pallas import tpu as pltpu

F32 = jnp.float32
BF16 = jnp.bfloat16

EPS = 1e-6
NEG = -1e30

CONV_WIDTH = 3
SWA_GROUPS = ((128, 1), (512, 4), (2048, 16))
SWA_HEADS = 8
SWA_HEAD_DIM = 64
SWA_WIDTH = SWA_HEADS * SWA_HEAD_DIM
SWA_BLK = 128
SWA_SCALE = SWA_HEAD_DIM ** -0.5
POOL_WINDOWS = (2, 4, 8, 16)
POOL_HIST = 15
MLA_HEADS = 8
Q_LORA = 384
KV_LORA = 256
QK_NOPE = 128
QK_ROPE = 64
V_DIM = 128
MLA_SCALE = (QK_NOPE + QK_ROPE) ** -0.5
ROPE_BASE = 10000.0
PAGE_SIZE = 128

VMEM_LIMIT_BYTES = 56 * 1024 * 1024
CONV_GRP = 16
POOL_GRP = 32


def _params(*sem):
    return pltpu.CompilerParams(dimension_semantics=sem, vmem_limit_bytes=VMEM_LIMIT_BYTES)


def _rms_mod(x, g, shift, scale):
    y = x * lax.rsqrt(jnp.mean(x * x, axis=-1, keepdims=True) + EPS)
    return (y * g) * (1.0 + scale) + shift


def _dot(a, b):
    return jnp.dot(a, b, preferred_element_type=F32)


def _dot_nt(a, b):
    return lax.dot_general(a, b, (((1,), (1,)), ((), ())), preferred_element_type=F32)


def _mod_spec(m, tm, rows_per_group):
    d = m.shape[-1]
    if m.shape[1] == 1:
        tpg = rows_per_group // tm
        return pl.BlockSpec((1, 1, d), lambda i, *_: (i // tpg, 0, 0))
    return pl.BlockSpec((1, tm, d), lambda i, *_: (i, 0, 0))


def _per_token(m, reps, tm):
    e = jnp.repeat(m, reps, axis=0)
    return e.reshape(e.shape[0] // tm, tm, e.shape[1])


def _ada_kernel(c_ref, w_ref, b_ref, o_ref):
    c = c_ref[...]
    s = (c * (1.0 / (1.0 + jnp.exp(-c)))).astype(BF16)
    o_ref[0] = _dot(s, w_ref[0].astype(BF16)) + b_ref[0]


def _ada_call(c, ada_w, ada_b):
    depth, d, n6 = ada_w.shape
    n = c.shape[0]
    tn = 1536
    return pl.pallas_call(
        _ada_kernel,
        grid=(depth, n6 // tn),
        in_specs=[pl.BlockSpec((n, d), lambda i, j: (0, 0)),
                  pl.BlockSpec((1, d, tn), lambda i, j: (i, 0, j)),
                  pl.BlockSpec((1, 1, tn), lambda i, j: (i, 0, j))],
        out_specs=pl.BlockSpec((1, n, tn), lambda i, j: (i, 0, j)),
        out_shape=jax.ShapeDtypeStruct((depth, n, n6), F32),
        compiler_params=_params("parallel", "parallel"),
        name="adaln",
    )(c, ada_w, ada_b.reshape(depth, 1, n6))


def _mlp_kernel(x_ref, g_ref, sh_ref, sc_ref, gt_ref, w1_ref, w2_ref, o_ref, h_scr, acc_scr):
    j = pl.program_id(1)

    @pl.when(j == 0)
    def _():
        h_scr[...] = _rms_mod(x_ref[...], g_ref[...], sh_ref[0], sc_ref[0]).astype(BF16)
        acc_scr[...] = jnp.zeros_like(acc_scr)

    a = _dot(h_scr[...], w1_ref[...])
    a = jnp.square(jnp.maximum(a, 0.0)).astype(BF16)
    acc_scr[...] += _dot(a, w2_ref[...])

    @pl.when(j == pl.num_programs(1) - 1)
    def _():
        o_ref[...] = x_ref[...] + gt_ref[0] * acc_scr[...]


def _mlp_call(x, g, sh, sc, gt, w1, w2, rows_per_group):
    rows, d = x.shape
    f = w1.shape[1]
    tm = min(1024, rows)
    tf = 512
    ms = _mod_spec(sh, tm, rows_per_group)
    return pl.pallas_call(
        _mlp_kernel,
        grid=(rows // tm, f // tf),
        in_specs=[pl.BlockSpec((tm, d), lambda i, j: (i, 0)),
                  pl.BlockSpec((1, d), lambda i, j: (0, 0)),
                  ms, ms, ms,
                  pl.BlockSpec((d, tf), lambda i, j: (0, j)),
                  pl.BlockSpec((tf, d), lambda i, j: (j, 0))],
        out_specs=pl.BlockSpec((tm, d), lambda i, j: (i, 0)),
        out_shape=jax.ShapeDtypeStruct((rows, d), F32),
        scratch_shapes=[pltpu.VMEM((tm, d), BF16), pltpu.VMEM((tm, d), F32)],
        compiler_params=_params("parallel", "arbitrary"),
        name="mlp",
    )(x, g, sh, sc, gt, w1, w2)


def _outproj_kernel(a_ref, w_ref, x_ref, gt_ref, o_ref):
    o_ref[...] = x_ref[...] + gt_ref[0] * _dot(a_ref[...], w_ref[...])


def _outproj_call(a, w, x, gt, rows_per_group):
    rows, d = x.shape
    k = a.shape[1]
    tm = min(512, rows)
    return pl.pallas_call(
        _outproj_kernel,
        grid=(rows // tm,),
        in_specs=[pl.BlockSpec((tm, k), lambda i: (i, 0)),
                  pl.BlockSpec((k, d), lambda i: (0, 0)),
                  pl.BlockSpec((tm, d), lambda i: (i, 0)),
                  _mod_spec(gt, tm, rows_per_group)],
        out_specs=pl.BlockSpec((tm, d), lambda i: (i, 0)),
        out_shape=jax.ShapeDtypeStruct((rows, d), F32),
        compiler_params=_params("parallel"),
        name="outproj",
    )(a, w, x, gt)


def _conv_kernel(*refs, tm, d, tiles_per_seq, has_override):
    if has_override:
        x_ref, g_ref, sh_ref, sc_ref, w_ref, k_ref, ov_ref, bv_ref, cu_ref, e_scr = refs
    else:
        x_ref, g_ref, sh_ref, sc_ref, w_ref, k_ref, bv_ref, cu_ref, e_scr = refs
    i = pl.program_id(0)
    h = _rms_mod(x_ref[...], g_ref[...], sh_ref[0], sc_ref[0]).astype(BF16)
    proj = _dot(h, w_ref[...])
    b = proj[:, :d]
    cu = proj[:, d:2 * d] * proj[:, 2 * d:]
    if has_override:
        row = lax.broadcasted_iota(jnp.int32, (tm, 1), 0)
        cu = jnp.where((row % CONV_GRP) >= CONV_GRP // 2, cu, ov_ref[...])
        e_scr[0:8, :] = jnp.zeros((8, d), F32)
    else:
        @pl.when(i % tiles_per_seq == 0)
        def _():
            e_scr[0:8, :] = jnp.zeros((8, d), F32)

        @pl.when(i % tiles_per_seq != 0)
        def _():
            e_scr[0:8, :] = e_scr[tm:tm + 8, :]
    e_scr[8:tm + 8, :] = cu
    cu1 = e_scr[7:tm + 7, :]
    cu2 = e_scr[6:tm + 6, :]
    k = k_ref[...]
    v = k[0:1, :] * cu2 + k[1:2, :] * cu1 + k[2:3, :] * cu
    bv_ref[...] = (b * v).astype(BF16)
    cu_ref[...] = cu


def _conv_call(x, g, sh, sc, w_in, k, override, rows_per_group):
    rows, d = x.shape
    tm = min(512, rows)
    has_override = override is not None
    ms = _mod_spec(sh, tm, rows_per_group)
    in_specs = [pl.BlockSpec((tm, d), lambda i: (i, 0)),
                pl.BlockSpec((1, d), lambda i: (0, 0)),
                ms, ms,
                pl.BlockSpec((d, 3 * d), lambda i: (0, 0)),
                pl.BlockSpec((CONV_WIDTH, d), lambda i: (0, 0))]
    args = [x, g, sh, sc, w_in, k]
    if has_override:
        in_specs.append(pl.BlockSpec((tm, d), lambda i: (i, 0)))
        args.append(override)
    kern = functools.partial(_conv_kernel, tm=tm, d=d, tiles_per_seq=max(rows_per_group // tm, 1),
                             has_override=has_override)
    return pl.pallas_call(
        kern,
        grid=(rows // tm,),
        in_specs=in_specs,
        out_specs=[pl.BlockSpec((tm, d), lambda i: (i, 0)), pl.BlockSpec((tm, d), lambda i: (i, 0))],
        out_shape=[jax.ShapeDtypeStruct((rows, d), BF16), jax.ShapeDtypeStruct((rows, d), F32)],
        scratch_shapes=[pltpu.VMEM((tm + 8, d), F32)],
        compiler_params=_params("arbitrary"),
        name="conv_in",
    )(*args)


def _pool_kernel(*refs, tm, d, tiles_per_seq, sample, pos0):
    x_ref, g_ref, sh_ref, sc_ref, gt_ref, w_ref, ps_ref, o_ref, st_ref, e_scr = refs
    i = pl.program_id(0)
    x = x_ref[...]
    h = _rms_mod(x, g_ref[...], sh_ref[0], sc_ref[0])
    row = lax.broadcasted_iota(jnp.int32, (tm, 1), 0)
    if sample:
        local = (row % POOL_GRP) - POOL_GRP // 2
        h = jnp.where(local >= 0, h, x)
        e_scr[0:16, :] = jnp.zeros((16, d), F32)
    else:
        local = (i % tiles_per_seq) * tm + row

        @pl.when(i % tiles_per_seq == 0)
        def _():
            e_scr[0:16, :] = jnp.zeros((16, d), F32)

        @pl.when(i % tiles_per_seq != 0)
        def _():
            e_scr[0:16, :] = e_scr[tm:tm + 16, :]
    e_scr[16:tm + 16, :] = h
    if sample:
        st_ref[...] = h
    else:
        st_ref[0] = e_scr[tm:tm + 16, :]
    gq = d // len(POOL_WINDOWS)
    for gi, w in enumerate(POOL_WINDOWS):
        lo = gi * gq
        win = h[:, lo:lo + gq]
        for j in range(1, w):
            win = win + e_scr[16 - j:16 - j + tm, lo:lo + gq]
        cnt = jnp.clip(pos0 + local + 1, 1, w).astype(F32)
        dd = (win / cnt - h[:, lo:lo + gq]).astype(BF16)
        y = _dot(dd, w_ref[gi]) * ps_ref[:, lo:lo + gq]
        o_ref[:, lo:lo + gq] = x[:, lo:lo + gq] + gt_ref[0][:, lo:lo + gq] * y


def _pool_call(x, g, sh, sc, gt, w, ps, rows_per_group, sample, pos0):
    rows, d = x.shape
    tm = min(512, rows)
    ms = _mod_spec(sh, tm, rows_per_group)
    gq = d // len(POOL_WINDOWS)
    if sample:
        st_spec = pl.BlockSpec((tm, d), lambda i: (i, 0))
        st_shape = jax.ShapeDtypeStruct((rows, d), F32)
    else:
        st_spec = pl.BlockSpec((1, 16, d), lambda i: (i, 0, 0))
        st_shape = jax.ShapeDtypeStruct((rows // tm, 16, d), F32)
    kern = functools.partial(_pool_kernel, tm=tm, d=d, tiles_per_seq=max(rows_per_group // tm, 1),
                             sample=sample, pos0=pos0)
    return pl.pallas_call(
        kern,
        grid=(rows // tm,),
        in_specs=[pl.BlockSpec((tm, d), lambda i: (i, 0)),
                  pl.BlockSpec((1, d), lambda i: (0, 0)),
                  ms, ms, ms,
                  pl.BlockSpec((len(POOL_WINDOWS), gq, gq), lambda i: (0, 0, 0)),
                  pl.BlockSpec((1, d), lambda i: (0, 0))],
        out_specs=[pl.BlockSpec((tm, d), lambda i: (i, 0)), st_spec],
        out_shape=[jax.ShapeDtypeStruct((rows, d), F32), st_shape],
        scratch_shapes=[pltpu.VMEM((tm + 16, d), F32)],
        compiler_params=_params("arbitrary"),
        name="pool",
    )(x, g, sh, sc, gt, w, ps)


def _swa_qkv_kernel(*refs, tm, dil):
    x_ref, g_ref, sh_ref, sc_ref, w_ref, gn_ref, seg_ref = refs[:7]
    outs = refs[7:16]
    z_scr = refs[16]
    w = SWA_WIDTH
    h = _rms_mod(x_ref[...], g_ref[...], sh_ref[0], sc_ref[0]).astype(BF16)
    for jb in range(9):
        which, g = divmod(jb, 3)
        z = _dot(h, w_ref[:, jb * w:(jb + 1) * w])
        if which < 2:
            ms = _dot((z * z).astype(BF16), seg_ref[...])
            z = z * lax.rsqrt(ms + EPS) * gn_ref[jb:jb + 1, :]
        r = dil[g]
        if r == 1:
            outs[jb][0, 0] = z
        else:
            for c in range(w // 128):
                z_scr[c] = z[:, c * 128:(c + 1) * 128]
            for rho in range(r):
                for c in range(w // 128):
                    outs[jb][0, rho, :, c * 128:(c + 1) * 128] = z_scr[c, pl.ds(rho, tm // r, stride=r), :]


def _swa_qkv_call(x, g, sh, sc, w, gn, seg, n_seq, seq_len, dil, rows_per_group):
    rows, d = x.shape
    wd = SWA_WIDTH
    tm = min(512, seq_len)
    tps = seq_len // tm
    ms = _mod_spec(sh, tm, rows_per_group)
    out_specs, out_shape = [], []
    for jb in range(9):
        r = dil[jb % 3]
        out_specs.append(pl.BlockSpec((1, r, tm // r, wd), lambda i: (i // tps, 0, i % tps, 0)))
        out_shape.append(jax.ShapeDtypeStruct((n_seq, r, seq_len // r, wd), F32))
    return pl.pallas_call(
        functools.partial(_swa_qkv_kernel, tm=tm, dil=dil),
        grid=(rows // tm,),
        in_specs=[pl.BlockSpec((tm, d), lambda i: (i, 0)),
                  pl.BlockSpec((1, d), lambda i: (0, 0)),
                  ms, ms,
                  pl.BlockSpec(w.shape, lambda i: (0, 0), pipeline_mode=pl.Buffered(1)),
                  pl.BlockSpec(gn.shape, lambda i: (0, 0)),
                  pl.BlockSpec(seg.shape, lambda i: (0, 0))],
        out_specs=out_specs,
        out_shape=out_shape,
        scratch_shapes=[pltpu.VMEM((wd // 128, tm, 128), F32)],
        compiler_params=_params("parallel"),
        name="swa_qkv",
    )(x, g, sh, sc, w, gn, seg)


def _swa_band_kernel(q_ref, kc_ref, kp_ref, vc_ref, vp_ref, o_ref, l_ref):
    b = pl.program_id(2)
    blk = SWA_BLK
    hd = SWA_HEAD_DIM
    q = q_ref[0, 0]
    k = jnp.concatenate([kp_ref[0, 0], kc_ref[0, 0]], axis=0).astype(BF16)
    vt = jnp.concatenate([vp_ref[0, 0], vc_ref[0, 0]], axis=0).T.astype(BF16)
    key = lax.broadcasted_iota(jnp.int32, (2 * blk, blk), 0)
    qi = lax.broadcasted_iota(jnp.int32, (2 * blk, blk), 1)
    dist = qi + blk - key
    ok = jnp.logical_and(dist >= 0, dist <= jnp.where(b > 0, blk, qi))
    lane_head = lax.broadcasted_iota(jnp.int32, (blk, 2 * hd), 1) // hd
    ots, lts = [], []
    for h in range(SWA_HEADS):
        slab = slice((h // 2) * 2 * hd, (h // 2 + 1) * 2 * hd)
        qm = jnp.where(lane_head == h % 2, q[:, slab], 0.0).astype(BF16)
        st = jnp.where(ok, _dot_nt(k[:, slab], qm) * SWA_SCALE, NEG)
        m = jnp.max(st, axis=0, keepdims=True)
        p = jnp.exp(st - m)
        l = jnp.sum(p, axis=0, keepdims=True)
        ots.append(_dot(vt[h * hd:(h + 1) * hd, :], p.astype(BF16)) * (1.0 / l))
        lts.append(jnp.broadcast_to(m + jnp.log(l), (hd, blk)))
    o_ref[0] = jnp.concatenate(ots, axis=0).T
    l_ref[0] = jnp.concatenate(lts, axis=0).T


def _swa_band_call(q, k, v):
    n_seq, r, lr, w = q.shape
    nb = lr // SWA_BLK
    blk = (1, 1, SWA_BLK, w)
    cur = pl.BlockSpec(blk, lambda n, rho, b: (n, rho, b, 0))
    prev = pl.BlockSpec(blk, lambda n, rho, b: (n, rho, jnp.maximum(b - 1, 0), 0))
    out_spec = pl.BlockSpec((1, SWA_BLK, w), lambda n, rho, b: (n, b, rho))
    o, l = pl.pallas_call(
        _swa_band_kernel,
        grid=(n_seq, r, nb),
        in_specs=[cur, cur, prev, cur, prev],
        out_specs=[out_spec, out_spec],
        out_shape=[jax.ShapeDtypeStruct((n_seq, lr, r * w), F32)] * 2,
        compiler_params=_params("parallel", "parallel", "arbitrary"),
        name="swa_band",
    )(q, k, k, v, v)
    return o.reshape(n_seq * lr * r, w), l.reshape(n_seq * lr * r, w)


def _swa_merge_kernel(o0, o1, o2, l0, l1, l2, w_ref, x_ref, gt_ref, out_ref):
    a0, a1, a2 = l0[...], l1[...], l2[...]
    m = jnp.maximum(jnp.maximum(a0, a1), a2)
    e0, e1, e2 = jnp.exp(a0 - m), jnp.exp(a1 - m), jnp.exp(a2 - m)
    inv = 1.0 / (e0 + e1 + e2)
    merged = (e0 * inv) * o0[...] + (e1 * inv) * o1[...] + (e2 * inv) * o2[...]
    out_ref[...] = x_ref[...] + gt_ref[0] * _dot(merged.astype(BF16), w_ref[...])


def _swa_merge_call(os_, ls_, w_o, x, gt, rows_per_group):
    rows, d = x.shape
    tm = min(512, rows)
    w = SWA_WIDTH
    sp = pl.BlockSpec((tm, w), lambda i: (i, 0))
    return pl.pallas_call(
        _swa_merge_kernel,
        grid=(rows // tm,),
        in_specs=[sp] * 6 + [pl.BlockSpec((w, d), lambda i: (0, 0)),
                             pl.BlockSpec((tm, d), lambda i: (i, 0)),
                             _mod_spec(gt, tm, rows_per_group)],
        out_specs=pl.BlockSpec((tm, d), lambda i: (i, 0)),
        out_shape=jax.ShapeDtypeStruct((rows, d), F32),
        compiler_params=_params("parallel"),
        name="swa_merge",
    )(*os_, *ls_, w_o, x, gt)


def _swa_step_kernel(q_ref, st_ref, new_ref, ns_ref, o_ref, l_ref, *, win, r, lb, s_new, n_blk):
    w = SWA_WIDTH
    hd = SWA_HEAD_DIM
    nrow = SWA_HEADS * s_new
    pad = 128 - s_new
    rowi = lax.broadcasted_iota(jnp.int32, (nrow, w), 0)
    lanei = lax.broadcasted_iota(jnp.int32, (nrow, w), 1)
    diag = (rowi // s_new) == (lanei // hd)
    s_idx = lax.broadcasted_iota(jnp.int32, (nrow, lb + 128), 0) % s_new
    col = lax.broadcasted_iota(jnp.int32, (nrow, lb + 128), 1)
    dist = lb + s_idx - jnp.where(col < lb, col, col - pad)
    ok = jnp.logical_and(jnp.logical_and(dist >= 0, dist % r == 0), dist <= win)
    ok = jnp.logical_and(ok, jnp.logical_or(col < lb, col >= lb + pad))
    for i in range(n_blk):
        q = q_ref[i]
        qbd = jnp.where(diag, jnp.concatenate([q] * SWA_HEADS, axis=0), 0.0).astype(BF16)
        kt = jnp.concatenate([st_ref[i, 0], new_ref[i, 0]], axis=1).astype(BF16)
        vt = jnp.concatenate([st_ref[i, 1], new_ref[i, 1]], axis=1).astype(BF16)
        sc = jnp.where(ok, _dot(qbd, kt) * SWA_SCALE, NEG)
        m = jnp.max(sc, axis=-1, keepdims=True)
        p = jnp.exp(sc - m)
        l = jnp.sum(p, axis=-1, keepdims=True)
        acc = jnp.where(diag, _dot_nt(p.astype(BF16), vt) / l, 0.0)
        lse = jnp.where(diag, m + jnp.log(l), 0.0)
        o = acc[0:s_new]
        ls = lse[0:s_new]
        for h in range(1, SWA_HEADS):
            o = o + acc[h * s_new:(h + 1) * s_new]
            ls = ls + lse[h * s_new:(h + 1) * s_new]
        o_ref[i] = o
        l_ref[i] = ls
        for kv in range(2):
            ns_ref[i, kv, :, 0:lb - s_new] = st_ref[i, kv, :, s_new:lb]
            ns_ref[i, kv, :, lb - s_new:lb] = new_ref[i, kv, :, pad:128]


def _swa_step_call(q, st_t, new_t, win, r):
    n_seq, s_new, w = q.shape
    lb = st_t.shape[-1]
    n_blk = max(1, min(n_seq, 1024 // lb))
    assert n_seq % n_blk == 0
    return pl.pallas_call(
        functools.partial(_swa_step_kernel, win=win, r=r, lb=lb, s_new=s_new, n_blk=n_blk),
        grid=(n_seq // n_blk,),
        in_specs=[pl.BlockSpec((n_blk, s_new, w), lambda n: (n, 0, 0)),
                  pl.BlockSpec((n_blk, 2, w, lb), lambda n: (n, 0, 0, 0)),
                  pl.BlockSpec((n_blk, 2, w, 128), lambda n: (n, 0, 0, 0))],
        out_specs=[pl.BlockSpec((n_blk, 2, w, lb), lambda n: (n, 0, 0, 0)),
                   pl.BlockSpec((n_blk, s_new, w), lambda n: (n, 0, 0)),
                   pl.BlockSpec((n_blk, s_new, w), lambda n: (n, 0, 0))],
        out_shape=[jax.ShapeDtypeStruct((n_seq, 2, w, lb), F32),
                   jax.ShapeDtypeStruct((n_seq, s_new, w), F32),
                   jax.ShapeDtypeStruct((n_seq, s_new, w), F32)],
        compiler_params=_params("parallel"),
        name="swa_step",
    )(q, st_t, new_t)


def _rope128(bn, cos_t, sin_t):
    return bn * cos_t + pltpu.roll(bn, 64, axis=1) * sin_t


def _mla_proj_kernel(*refs, decode):
    (x_ref, g_ref, sh_ref, sc_ref, w1_ref, gqa_ref, wqb_ref, gq_ref, gkv_ref, gkr_ref,
     cos_ref, sin_ref, wx_ref) = refs[:13]
    if decode:
        outs = refs[13:]
    else:
        wvt_ref = refs[13]
        outs = refs[14:]
    nh = MLA_HEADS
    h = _rms_mod(x_ref[...], g_ref[...], sh_ref[0], sc_ref[0]).astype(BF16)
    a = _dot(h, w1_ref[...])
    qa = a[:, :Q_LORA]
    qa = (qa * lax.rsqrt(jnp.mean(qa * qa, axis=-1, keepdims=True) + EPS) * gqa_ref[...]).astype(BF16)
    q = _dot(qa, wqb_ref[...])
    cos_t = cos_ref[...]
    sin_t = sin_ref[...]
    latc = a[:, Q_LORA:Q_LORA + KV_LORA]
    latc = latc * lax.rsqrt(jnp.mean(latc * latc, axis=-1, keepdims=True) + EPS) * gkv_ref[...]
    kb = a[:, Q_LORA + KV_LORA:]
    kb = kb * lax.rsqrt(jnp.sum(kb * kb, axis=-1, keepdims=True) * (1.0 / QK_ROPE) + EPS) * gkr_ref[...]
    kb = _rope128(kb, cos_t, sin_t)
    lane = lax.broadcasted_iota(jnp.int32, kb.shape, 1)
    kstd = jnp.where(lane < 32, kb, pltpu.roll(kb, 96, axis=1))
    if decode:
        lat_ref, qabs_ref, qpe_ref = outs
    else:
        lat_ref, qcat_ref, kcat_ref, vt_ref = outs
    lat_ref[:, 0:KV_LORA] = latc
    lat_ref[:, KV_LORA:KV_LORA + QK_ROPE] = kstd[:, 0:QK_ROPE]
    gq = gq_ref[...]
    for hh in range(nh):
        an = q[:, hh * 256:hh * 256 + 128]
        an = an * lax.rsqrt(jnp.mean(an * an, axis=-1, keepdims=True) + EPS) * gq[:, 0:128]
        bn = q[:, hh * 256 + 128:hh * 256 + 256]
        bn = bn * lax.rsqrt(jnp.sum(bn * bn, axis=-1, keepdims=True) * (1.0 / QK_ROPE) + EPS) * gq[:, 128:256]
        bn = _rope128(bn, cos_t, sin_t)
        if decode:
            ag = an * gq[:, 256:384]
            a_hi = ag.astype(BF16)
            a_lo = (ag - a_hi.astype(F32)).astype(BF16)
            qabs_ref[:, hh * 256:(hh + 1) * 256] = (_dot(a_hi, wx_ref[hh]) + _dot(a_lo, wx_ref[hh])).astype(BF16)
            qpe_ref[:, hh * 128:(hh + 1) * 128] = jnp.where(lane < 32, bn, pltpu.roll(bn, 96, axis=1)).astype(BF16)
        else:
            qcat_ref[:, hh * 256:hh * 256 + 128] = an.astype(BF16)
            qcat_ref[:, hh * 256 + 128:(hh + 1) * 256] = bn.astype(BF16)
    if not decode:
        gkn = gq_ref[:, 256:384]
        latb = latc.astype(BF16)
        kvx = _dot(latb, wx_ref[...])
        kbb = kb.astype(BF16)
        for hh in range(nh):
            kn = kvx[:, hh * 128:(hh + 1) * 128]
            kn = kn * lax.rsqrt(jnp.mean(kn * kn, axis=-1, keepdims=True) + EPS) * gkn
            kcat_ref[:, hh * 256:hh * 256 + 128] = kn.astype(BF16)
            kcat_ref[:, hh * 256 + 128:(hh + 1) * 256] = kbb
        vt_ref[0] = _dot_nt(wvt_ref[...], latb).astype(BF16)


def _mla_proj_call(x, g, sh, sc, wts, cos_t, sin_t, rows_per_group, decode):
    rows, d = x.shape
    tm = min(512, rows)
    nh = MLA_HEADS
    ms = _mod_spec(sh, tm, rows_per_group)
    tps = max(rows_per_group // tm, 1)
    if decode:
        tab_spec = pl.BlockSpec((tm, 128), lambda i: (0, 0))
        wx = wts["wk_t"]
        wx_spec = pl.BlockSpec(wx.shape, lambda i: (0, 0, 0))
        out_specs = [pl.BlockSpec((tm, KV_LORA + QK_ROPE), lambda i: (i, 0)),
                     pl.BlockSpec((tm, nh * 256), lambda i: (i, 0)),
                     pl.BlockSpec((tm, nh * 128), lambda i: (i, 0))]
        out_shape = [jax.ShapeDtypeStruct((rows, KV_LORA + QK_ROPE), F32),
                     jax.ShapeDtypeStruct((rows, nh * 256), BF16),
                     jax.ShapeDtypeStruct((rows, nh * 128), BF16)]
    else:
        tab_spec = pl.BlockSpec((tm, 128), lambda i: (i % tps, 0))
        wx = wts["w_kn"]
        wx_spec = pl.BlockSpec(wx.shape, lambda i: (0, 0))
        out_specs = [pl.BlockSpec((tm, KV_LORA + QK_ROPE), lambda i: (i, 0)),
                     pl.BlockSpec((tm, nh * 256), lambda i: (i, 0)),
                     pl.BlockSpec((tm, nh * 256), lambda i: (i, 0)),
                     pl.BlockSpec((1, nh * V_DIM, tm), lambda i: (i // tps, 0, i % tps))]
        out_shape = [jax.ShapeDtypeStruct((rows, KV_LORA + QK_ROPE), F32),
                     jax.ShapeDtypeStruct((rows, nh * 256), BF16),
                     jax.ShapeDtypeStruct((rows, nh * 256), BF16),
                     jax.ShapeDtypeStruct((rows // rows_per_group, nh * V_DIM, rows_per_group), BF16)]

    def full(a):
        return pl.BlockSpec(a.shape, lambda i: (0,) * a.ndim)

    extra = [] if decode else [wts["wv_t"]]
    return pl.pallas_call(
        functools.partial(_mla_proj_kernel, decode=decode),
        grid=(rows // tm,),
        in_specs=[pl.BlockSpec((tm, d), lambda i: (i, 0)),
                  pl.BlockSpec((1, d), lambda i: (0, 0)),
                  ms, ms,
                  full(wts["w1"]), full(wts["g_qa"]), full(wts["w_qb"]), full(wts["gq"]),
                  full(wts["g_kva"]), full(wts["g_kr"]),
                  tab_spec, tab_spec, wx_spec] + [full(a) for a in extra],
        out_specs=out_specs,
        out_shape=out_shape,
        compiler_params=_params("parallel"),
        name="mla_proj_decode" if decode else "mla_proj",
    )(x, g, sh, sc, wts["w1"], wts["g_qa"], wts["w_qb"], wts["gq"], wts["g_kva"], wts["g_kr"],
      cos_t, sin_t, wx, *extra)


def _mla_causal_kernel(q_ref, k_ref, vt_ref, o_ref, *, seq_len, tq):
    r = lax.broadcasted_iota(jnp.int32, (tq, tq), 0)
    c = lax.broadcasted_iota(jnp.int32, (tq, tq), 1)
    for qb in range(seq_len // tq):
        lo = qb * tq
        q = q_ref[0, lo:lo + tq, :]
        s_d = jnp.where(r <= c, _dot_nt(k_ref[0, lo:lo + tq, :], q) * MLA_SCALE, NEG)
        m = jnp.max(s_d, axis=0, keepdims=True)
        if qb > 0:
            s_p = _dot_nt(k_ref[0, 0:lo, :], q) * MLA_SCALE
            m = jnp.maximum(m, jnp.max(s_p, axis=0, keepdims=True))
        p_d = jnp.exp(s_d - m)
        l = jnp.sum(p_d, axis=0, keepdims=True)
        ot = _dot(vt_ref[0, :, lo:lo + tq], p_d.astype(BF16))
        if qb > 0:
            p_p = jnp.exp(s_p - m)
            l = l + jnp.sum(p_p, axis=0, keepdims=True)
            ot = ot + _dot(vt_ref[0, :, 0:lo], p_p.astype(BF16))
        o_ref[0, lo:lo + tq, :] = (ot * (1.0 / l)).T.astype(BF16)


def _mla_causal_call(qcat, kcat, vt, n_seq, seq_len):
    nh = MLA_HEADS
    tq = min(256, seq_len)
    q3 = qcat.reshape(n_seq, seq_len, nh * 256)
    k3 = kcat.reshape(n_seq, seq_len, nh * 256)
    o = pl.pallas_call(
        functools.partial(_mla_causal_kernel, seq_len=seq_len, tq=tq),
        grid=(n_seq, nh),
        in_specs=[pl.BlockSpec((1, seq_len, 256), lambda n, h: (n, 0, h)),
                  pl.BlockSpec((1, seq_len, 256), lambda n, h: (n, 0, h)),
                  pl.BlockSpec((1, V_DIM, seq_len), lambda n, h: (n, h, 0))],
        out_specs=pl.BlockSpec((1, seq_len, V_DIM), lambda n, h: (n, 0, h)),
        out_shape=jax.ShapeDtypeStruct((n_seq, seq_len, nh * V_DIM), BF16),
        compiler_params=_params("parallel", "parallel"),
        name="mla_causal",
    )(q3, k3, vt)
    return o.reshape(n_seq * seq_len, nh * V_DIM)


def _mla_decode_kernel(pt_ref, *refs, n_pg, s_new):
    del pt_ref
    page_refs = refs[:n_pg]
    (qabs_ref, qpe_ref, latn_ref, wkt_ref, wv_ref, o_ref, m_scr, l_scr, ctx_scr) = refs[n_pg:]
    c = pl.program_id(1)
    nh = MLA_HEADS
    nrow = nh * s_new

    @pl.when(c == 0)
    def _():
        m_scr[...] = jnp.full_like(m_scr, NEG)
        l_scr[...] = jnp.zeros_like(l_scr)
        ctx_scr[...] = jnp.zeros_like(ctx_scr)

    def process(pages, mask):
        latc = jnp.concatenate([pg[0:KV_LORA, :].astype(BF16) for pg in pages], axis=1)
        kpe = jnp.concatenate([pg[KV_LORA:KV_LORA + QK_ROPE, :].astype(BF16) for pg in pages], axis=1)
        r = _dot(wkt_ref[...], latc)
        rq = _dot(qabs_ref[0], latc)
        spe = _dot(qpe_ref[0][:, 0:QK_ROPE], kpe)
        parts = []
        for h in range(nh):
            blk = r[h * QK_NOPE:(h + 1) * QK_NOPE]
            rstd = lax.rsqrt(jnp.sum(blk * blk, axis=0, keepdims=True) * (1.0 / QK_NOPE) + EPS)
            parts.append(rq[h * s_new:(h + 1) * s_new] * rstd)
        s = (jnp.concatenate(parts, axis=0) + spe) * MLA_SCALE
        if mask is not None:
            s = jnp.where(mask, s, NEG)
        m_old = m_scr[...]
        m_new = jnp.maximum(m_old, jnp.max(s, axis=-1, keepdims=True))
        alpha = jnp.exp(m_old - m_new)
        p = jnp.exp(s - m_new)
        l_scr[...] = alpha * l_scr[...] + jnp.sum(p, axis=-1, keepdims=True)
        ctx_scr[...] = alpha * ctx_scr[...] + _dot_nt(p.astype(BF16), latc)
        m_scr[...] = m_new

    process([pr[0, 0] for pr in page_refs], None)

    @pl.when(c == pl.num_programs(1) - 1)
    def _():
        si = lax.broadcasted_iota(jnp.int32, (nrow, PAGE_SIZE), 0) % s_new
        kj = lax.broadcasted_iota(jnp.int32, (nrow, PAGE_SIZE), 1)
        process([latn_ref[0]], kj <= si)
        ctx = ctx_scr[...] / l_scr[...]
        for h in range(nh):
            ctx_h = ctx[h * s_new:(h + 1) * s_new].astype(BF16)
            o_ref[0, :, h * V_DIM:(h + 1) * V_DIM] = _dot(ctx_h, wv_ref[h]).astype(BF16)


def _mla_decode_call(cache_t, layer, page_table, qabs, qpe, lat_new, wkt, wv, n_seq, s_new):
    nh = MLA_HEADS
    n_pages = page_table.shape[1]
    n_pg = min(32, n_pages)
    lw = KV_LORA + QK_ROPE
    nrow = nh * s_new

    def page_spec(p):
        return pl.BlockSpec((1, 1, lw, PAGE_SIZE),
                            lambda n, c, pt: (layer, pt[n * n_pages + c * n_pg + p], 0, 0))

    latn_t = jnp.transpose(lat_new.reshape(n_seq, s_new, lw), (0, 2, 1))
    latn_t = jnp.pad(latn_t, ((0, 0), (0, 0), (0, PAGE_SIZE - s_new)))
    grid_spec = pltpu.PrefetchScalarGridSpec(
        num_scalar_prefetch=1,
        grid=(n_seq, n_pages // n_pg),
        in_specs=[page_spec(p) for p in range(n_pg)] + [
            pl.BlockSpec((1, nrow, 256), lambda n, c, pt: (n, 0, 0)),
            pl.BlockSpec((1, nrow, 128), lambda n, c, pt: (n, 0, 0)),
            pl.BlockSpec((1, lw, PAGE_SIZE), lambda n, c, pt: (n, 0, 0)),
            pl.BlockSpec(wkt.shape, lambda n, c, pt: (0, 0)),
            pl.BlockSpec(wv.shape, lambda n, c, pt: (0, 0, 0))],
        out_specs=pl.BlockSpec((1, s_new, nh * V_DIM), lambda n, c, pt: (n, 0, 0)),
        scratch_shapes=[pltpu.VMEM((nrow, 1), F32), pltpu.VMEM((nrow, 1), F32),
                        pltpu.VMEM((nrow, KV_LORA), F32)])
    o = pl.pallas_call(
        functools.partial(_mla_decode_kernel, n_pg=n_pg, s_new=s_new),
        grid_spec=grid_spec,
        out_shape=jax.ShapeDtypeStruct((n_seq, s_new, nh * V_DIM), BF16),
        compiler_params=_params("parallel", "arbitrary"),
        name="mla_decode",
    )(page_table.reshape(-1), *([cache_t] * n_pg),
      jnp.transpose(qabs.reshape(n_seq, s_new, nh, 256), (0, 2, 1, 3)).reshape(n_seq, nrow, 256),
      jnp.transpose(qpe.reshape(n_seq, s_new, nh, 128), (0, 2, 1, 3)).reshape(n_seq, nrow, 128),
      latn_t, wkt, wv)
    return o.reshape(n_seq * s_new, nh * V_DIM)


def _mla_weights(w_qa, g_qa, w_qb, w_kva, g_kva, w_kvb, g_qn, g_kn, g_qr, g_kr):
    nh = MLA_HEADS
    half = QK_ROPE // 2

    def pad_rope_cols(w):
        z = jnp.zeros(w.shape[:-1] + (half,), w.dtype)
        return jnp.concatenate([w[..., :half], z, w[..., half:], z], axis=-1)

    w_kva_p = jnp.concatenate([w_kva[:, :KV_LORA], pad_rope_cols(w_kva[:, KV_LORA:])], axis=1)
    w1 = jnp.concatenate([w_qa, w_kva_p], axis=1).astype(BF16)
    wq = w_qb.reshape(Q_LORA, nh, QK_NOPE + QK_ROPE)
    wq = jnp.concatenate([wq[..., :QK_NOPE], pad_rope_cols(wq[..., QK_NOPE:])], axis=-1)
    w_qb_p = wq.reshape(Q_LORA, nh * 256).astype(BF16)
    gq = jnp.concatenate([g_qn, pad_rope_cols(g_qr), g_kn])[None, :]
    wkv = w_kvb.reshape(KV_LORA, nh, QK_NOPE + V_DIM)
    w_kn = wkv[..., :QK_NOPE].reshape(KV_LORA, nh * QK_NOPE).astype(BF16)
    wv_t = jnp.transpose(wkv[..., QK_NOPE:], (1, 2, 0)).reshape(nh * V_DIM, KV_LORA).astype(BF16)
    wk_t = jnp.transpose(wkv[..., :QK_NOPE], (1, 2, 0)).astype(BF16)
    return dict(w1=w1, g_qa=g_qa[None, :], w_qb=w_qb_p, gq=gq, g_kva=g_kva[None, :],
                g_kr=pad_rope_cols(g_kr)[None, :], w_kn=w_kn, wv_t=wv_t, wk_t=wk_t,
                wkt_flat=wk_t.reshape(nh * QK_NOPE, KV_LORA),
                wv=jnp.transpose(wkv[..., QK_NOPE:], (1, 0, 2)).astype(BF16))


def _rope_tables(pos):
    half = QK_ROPE // 2
    inv = ROPE_BASE ** (-jnp.arange(half, dtype=F32) / half)
    ang = pos.astype(F32)[:, None] * inv[None, :]
    cos, sin = jnp.cos(ang), jnp.sin(ang)
    z = jnp.zeros_like(cos)
    return (jnp.concatenate([cos, z, cos, z], axis=1), jnp.concatenate([-sin, z, sin, z], axis=1))


def kernel(x_prompt, x_sample, state_conv, state_swa0, state_swa1, state_swa2, state_pool, cache_mla, page_table,
           c_prompt, c_sample, ada_w, ada_b, norm1_g, norm2_g, mlp_w1, mlp_w2, conv_w_in, conv_k, conv_w_out,
           swa_w_qkv, swa_qn_g, swa_kn_g, swa_w_o, pool_w, pool_scale, mla_w_qa, mla_g_qa, mla_w_qb, mla_w_kva,
           mla_g_kva, mla_w_kvb, mla_g_qn, mla_g_kn, mla_g_qr, mla_g_kr, mla_w_o):
    bp, t, d = x_prompt.shape
    bs, s, _ = x_sample.shape
    depth = ada_w.shape[0]
    past = page_table.shape[1] * PAGE_SIZE
    swa_states = (state_swa0, state_swa1, state_swa2)
    for (win, _), st in zip(SWA_GROUPS, swa_states):
        assert st.shape[2] == win, "sliding-window state must hold a full window"
    assert t % (SWA_GROUPS[-1][1] * SWA_BLK) == 0, "prompt length must split into full dilated blocks"
    rows_p, rows_s = bp * t, bs * s
    xp = x_prompt.reshape(rows_p, d)
    xs = x_sample.reshape(rows_s, d)

    mod = _ada_call(jnp.concatenate([c_prompt, c_sample], axis=0), ada_w, ada_b)

    def mods(i):
        out_p, out_s = [], []
        for j in range(6):
            m = mod[i, :, j * d:(j + 1) * d]
            out_p.append(m[:bp, None, :])
            out_s.append(m[bp:])
        return out_p, out_s

    def tok(m, tm):
        return _per_token(m, s, tm)

    tm_s512 = min(512, rows_s)
    tm_s1024 = min(1024, rows_s)
    conv_p = conv_s = pool_p = pool_s = mla_p = mla_s = None
    swa_p, swa_s = [None] * 3, [None] * 3

    for i in range(depth):
        kind, j = i % 4, i // 4
        (sh1p, sc1p, g1p, sh2p, sc2p, g2p), (sh1s, sc1s, g1s, sh2s, sc2s, g2s) = mods(i)
        n1 = norm1_g[i][None, :]
        if kind == 0:
            w_in = conv_w_in[j].astype(BF16)
            w_out = conv_w_out[j].astype(BF16)
            bv, cu = _conv_call(xp, n1, sh1p, sc1p, w_in, conv_k[j], None, t)
            conv_p = cu.reshape(bp, t, d)[:, t - (CONV_WIDTH - 1):]
            xp = _outproj_call(bv, w_out, xp, g1p, t)
            grp = CONV_GRP
            pad = jnp.zeros((bs, grp - s, d), F32)
            x_ext = jnp.concatenate([pad, xs.reshape(bs, s, d)], axis=1).reshape(bs * grp, d)
            ov = jnp.concatenate([jnp.zeros((bs, grp - s - 2, d), F32), state_conv[j], jnp.zeros((bs, s, d), F32)],
                                 axis=1).reshape(bs * grp, d)
            tme = min(512, bs * grp)
            she = _per_token(sh1s, grp, tme)
            sce = _per_token(sc1s, grp, tme)
            bv, cu = _conv_call(x_ext, n1, she, sce, w_in, conv_k[j], ov, grp)
            conv_s = cu.reshape(bs, grp, d)[:, grp - (CONV_WIDTH - 1):]
            bv = bv.reshape(bs, grp, d)[:, grp - s:].reshape(rows_s, d)
            xs = _outproj_call(bv, w_out, xs, tok(g1s, tm_s512), s)
        elif kind == 1:
            w_qkv = swa_w_qkv[j].astype(BF16)
            w_o = swa_w_o[j].astype(BF16)
            gn = jnp.concatenate([jnp.tile(swa_qn_g[j], (1, SWA_HEADS)), jnp.tile(swa_kn_g[j], (1, SWA_HEADS))],
                                 axis=0)
            seg = (jnp.arange(SWA_WIDTH)[:, None] // SWA_HEAD_DIM == jnp.arange(SWA_WIDTH)[None, :] // SWA_HEAD_DIM)
            seg = (seg.astype(F32) / SWA_HEAD_DIM).astype(BF16)
            dil = tuple(r for _, r in SWA_GROUPS)
            qkv = _swa_qkv_call(xp, n1, sh1p, sc1p, w_qkv, gn, seg, bp, t, dil, t)
            os_, ls_ = [], []
            for g, (win, r) in enumerate(SWA_GROUPS):
                o_g, l_g = _swa_band_call(qkv[g], qkv[3 + g], qkv[6 + g])
                os_.append(o_g)
                ls_.append(l_g)
                keep = min(win, t)
                kv = [jnp.transpose(a, (0, 2, 1, 3)).reshape(bp, t, SWA_HEADS, SWA_HEAD_DIM)[:, t - keep:]
                      for a in (qkv[3 + g], qkv[6 + g])]
                swa_p[g] = jnp.stack(kv, axis=2)[None]
            xp = _swa_merge_call(os_, ls_, w_o, xp, g1p, t)
            tm_q = min(512, rows_s)
            qkv_s = _swa_qkv_call(xs, n1, tok(sh1s, tm_q), tok(sc1s, tm_q), w_qkv, gn, seg, 1, rows_s,
                                  (1, 1, 1), s)
            os_, ls_ = [], []
            for g, (win, r) in enumerate(SWA_GROUPS):
                q_g, k_g, v_g = (qkv_s[c * 3 + g].reshape(bs, s, SWA_WIDTH) for c in range(3))
                st_t = jnp.transpose(swa_states[g][j], (0, 2, 3, 4, 1)).reshape(bs, 2, SWA_WIDTH, win)
                new_t = jnp.transpose(jnp.stack([k_g, v_g], axis=1), (0, 1, 3, 2))
                new_t = jnp.pad(new_t, ((0, 0), (0, 0), (0, 0), (128 - s, 0)))
                ns, o_g, l_g = _swa_step_call(q_g, st_t, new_t, win, r)
                swa_s[g] = jnp.transpose(ns.reshape(bs, 2, SWA_HEADS, SWA_HEAD_DIM, win), (0, 4, 1, 2, 3))[None]
                os_.append(o_g.reshape(rows_s, SWA_WIDTH))
                ls_.append(l_g.reshape(rows_s, SWA_WIDTH))
            xs = _swa_merge_call(os_, ls_, w_o, xs, tok(g1s, tm_s512), s)
        elif kind == 2:
            pw = pool_w[j].astype(BF16)
            ps = pool_scale[j][None, :]
            xp, tail = _pool_call(xp, n1, sh1p, sc1p, g1p, pw, ps, t, False, 0)
            tpt = t // min(512, rows_p)
            pool_p = tail.reshape(bp, tpt, 16, d)[:, -1, 1:]
            grp = POOL_GRP
            x_ext = jnp.concatenate([jnp.zeros((bs, 1, d), F32), state_pool[j], xs.reshape(bs, s, d),
                                     jnp.zeros((bs, grp - 16 - s, d), F32)], axis=1).reshape(bs * grp, d)
            tme = min(512, bs * grp)
            y_ext, ext = _pool_call(x_ext, n1, _per_token(sh1s, grp, tme), _per_token(sc1s, grp, tme),
                                    _per_token(g1s, grp, tme), pw, ps, grp, True, past)
            xs = y_ext.reshape(bs, grp, d)[:, 16:16 + s].reshape(rows_s, d)
            pool_s = ext.reshape(bs, grp, d)[:, 16 + s - POOL_HIST:16 + s]
        else:
            wts = _mla_weights(mla_w_qa[j], mla_g_qa[j], mla_w_qb[j], mla_w_kva[j], mla_g_kva[j], mla_w_kvb[j],
                               mla_g_qn[j], mla_g_kn[j], mla_g_qr[j], mla_g_kr[j])
            w_o = mla_w_o[j].astype(BF16)
            cos_p, sin_p = _rope_tables(jnp.arange(t))
            lat_p, qcat, kcat, vt = _mla_proj_call(xp, n1, sh1p, sc1p, wts, cos_p, sin_p, t, False)
            o_p = _mla_causal_call(qcat, kcat, vt, bp, t)
            xp = _outproj_call(o_p, w_o, xp, g1p, t)
            mla_p = lat_p.reshape(bp, t, KV_LORA + QK_ROPE)
            cos_s, sin_s = _rope_tables(past + jnp.arange(s))
            reps = tm_s512 // s
            cos_s, sin_s = jnp.tile(cos_s, (reps, 1)), jnp.tile(sin_s, (reps, 1))
            lat_s, qabs, qpe = _mla_proj_call(xs, n1, tok(sh1s, tm_s512), tok(sc1s, tm_s512), wts, cos_s, sin_s, s, True)
            cache_t = jnp.transpose(cache_mla, (0, 1, 3, 2))
            o_s = _mla_decode_call(cache_t, j, page_table, qabs, qpe, lat_s, wts["wkt_flat"], wts["wv"], bs, s)
            xs = _outproj_call(o_s, w_o, xs, tok(g1s, tm_s512), s)
            mla_s = lat_s.reshape(bs, s, KV_LORA + QK_ROPE)
        n2 = norm2_g[i][None, :]
        w1 = mlp_w1[i].astype(BF16)
        w2 = mlp_w2[i].astype(BF16)
        xp = _mlp_call(xp, n2, sh2p, sc2p, g2p, w1, w2, t)
        xs = _mlp_call(xs, n2, tok(sh2s, tm_s1024), tok(sc2s, tm_s1024), tok(g2s, tm_s1024), w1, w2, s)

    return (xp.reshape(bp, t, d), xs.reshape(bs, s, d), conv_p[None], conv_s[None],
            swa_p[0], swa_s[0], swa_p[1], swa_s[1], swa_p[2], swa_s[2],
            pool_p[None], pool_s[None], mla_p[None], mla_s[None])
```

```python
import functools

import jax
import jax.numpy as jnp
from jax import lax
from jax.experimental import pallas as pl
from jax.experimental.pallas import tpu as pltpu

F32 = jnp.float32
BF16 = jnp.bfloat16

EPS = 1e-6
NEG = -1e30

CONV_WIDTH = 3
SWA_GROUPS = ((128, 1), (512, 4), (2048, 16))
SWA_HEADS = 8
SWA_HEAD_DIM = 64
SWA_WIDTH = SWA_HEADS * SWA_HEAD_DIM
SWA_BLK = 128
SWA_SCALE = SWA_HEAD_DIM ** -0.5
POOL_WINDOWS = (2, 4, 8, 16)
POOL_HIST = 15
MLA_HEADS = 8
Q_LORA = 384
KV_LORA = 256
QK_NOPE = 128
QK_ROPE = 64
V_DIM = 128
MLA_SCALE = (QK_NOPE + QK_ROPE) ** -0.5
ROPE_BASE = 10000.0
PAGE_SIZE = 128

VMEM_LIMIT_BYTES = 56 * 1024 * 1024
CONV_GRP = 16
POOL_GRP = 32


def _params(*sem):
    return pltpu.CompilerParams(dimension_semantics=sem, vmem_limit_bytes=VMEM_LIMIT_BYTES)


def _rms_mod(x, g, shift, scale):
    y = x * lax.rsqrt(jnp.mean(x * x, axis=-1, keepdims=True) + EPS)
    return (y * g) * (1.0 + scale) + shift


def _dot(a, b):
    return jnp.dot(a, b, preferred_element_type=F32)


def _dot_nt(a, b):
    return lax.dot_general(a, b, (((1,), (1,)), ((), ())), preferred_element_type=F32)


def _mod_spec(m, tm, rows_per_group):
    d = m.shape[-1]
    if m.shape[1] == 1:
        tpg = rows_per_group // tm
        return pl.BlockSpec((1, 1, d), lambda i, *_: (i // tpg, 0, 0))
    return pl.BlockSpec((1, tm, d), lambda i, *_: (i, 0, 0))


def _per_token(m, reps, tm):
    e = jnp.repeat(m, reps, axis=0)
    return e.reshape(e.shape[0] // tm, tm, e.shape[1])


def _ada_kernel(c_ref, w_ref, b_ref, o_ref):
    c = c_ref[...]
    s = (c * (1.0 / (1.0 + jnp.exp(-c)))).astype(BF16)
    o_ref[0] = _dot(s, w_ref[0].astype(BF16)) + b_ref[0]


def _ada_call(c, ada_w, ada_b):
    depth, d, n6 = ada_w.shape
    n = c.shape[0]
    tn = 1536
    return pl.pallas_call(
        _ada_kernel,
        grid=(depth, n6 // tn),
        in_specs=[pl.BlockSpec((n, d), lambda i, j: (0, 0)),
                  pl.BlockSpec((1, d, tn), lambda i, j: (i, 0, j)),
                  pl.BlockSpec((1, 1, tn), lambda i, j: (i, 0, j))],
        out_specs=pl.BlockSpec((1, n, tn), lambda i, j: (i, 0, j)),
        out_shape=jax.ShapeDtypeStruct((depth, n, n6), F32),
        compiler_params=_params("parallel", "parallel"),
        name="adaln",
    )(c, ada_w, ada_b.reshape(depth, 1, n6))


def _mlp_kernel(x_ref, g_ref, sh_ref, sc_ref, gt_ref, w1_ref, w2_ref, o_ref, h_scr, acc_scr):
    j = pl.program_id(1)

    @pl.when(j == 0)
    def _():
        h_scr[...] = _rms_mod(x_ref[...], g_ref[...], sh_ref[0], sc_ref[0]).astype(BF16)
        acc_scr[...] = jnp.zeros_like(acc_scr)

    a = _dot(h_scr[...], w1_ref[...])
    a = jnp.square(jnp.maximum(a, 0.0)).astype(BF16)
    acc_scr[...] += _dot(a, w2_ref[...])

    @pl.when(j == pl.num_programs(1) - 1)
    def _():
        o_ref[...] = x_ref[...] + gt_ref[0] * acc_scr[...]


def _mlp_call(x, g, sh, sc, gt, w1, w2, rows_per_group):
    rows, d = x.shape
    f = w1.shape[1]
    tm = min(1024, rows)
    tf = 512
    ms = _mod_spec(sh, tm, rows_per_group)
    return pl.pallas_call(
        _mlp_kernel,
        grid=(rows // tm, f // tf),
        in_specs=[pl.BlockSpec((tm, d), lambda i, j: (i, 0)),
                  pl.BlockSpec((1, d), lambda i, j: (0, 0)),
                  ms, ms, ms,
                  pl.BlockSpec((d, tf), lambda i, j: (0, j)),
                  pl.BlockSpec((tf, d), lambda i, j: (j, 0))],
        out_specs=pl.BlockSpec((tm, d), lambda i, j: (i, 0)),
        out_shape=jax.ShapeDtypeStruct((rows, d), F32),
        scratch_shapes=[pltpu.VMEM((tm, d), BF16), pltpu.VMEM((tm, d), F32)],
        compiler_params=_params("parallel", "arbitrary"),
        name="mlp",
    )(x, g, sh, sc, gt, w1, w2)


def _outproj_kernel(a_ref, w_ref, x_ref, gt_ref, o_ref):
    o_ref[...] = x_ref[...] + gt_ref[0] * _dot(a_ref[...], w_ref[...])


def _outproj_call(a, w, x, gt, rows_per_group):
    rows, d = x.shape
    k = a.shape[1]
    tm = min(512, rows)
    return pl.pallas_call(
        _outproj_kernel,
        grid=(rows // tm,),
        in_specs=[pl.BlockSpec((tm, k), lambda i: (i, 0)),
                  pl.BlockSpec((k, d), lambda i: (0, 0)),
                  pl.BlockSpec((tm, d), lambda i: (i, 0)),
                  _mod_spec(gt, tm, rows_per_group)],
        out_specs=pl.BlockSpec((tm, d), lambda i: (i, 0)),
        out_shape=jax.ShapeDtypeStruct((rows, d), F32),
        compiler_params=_params("parallel"),
        name="outproj",
    )(a, w, x, gt)


def _conv_kernel(*refs, tm, d, tiles_per_seq, has_override):
    if has_override:
        x_ref, g_ref, sh_ref, sc_ref, w_ref, k_ref, ov_ref, bv_ref, cu_ref, e_scr = refs
    else:
        x_ref, g_ref, sh_ref, sc_ref, w_ref, k_ref, bv_ref, cu_ref, e_scr = refs
    i = pl.program_id(0)
    h = _rms_mod(x_ref[...], g_ref[...], sh_ref[0], sc_ref[0]).astype(BF16)
    proj = _dot(h, w_ref[...])
    b = proj[:, :d]
    cu = proj[:, d:2 * d] * proj[:, 2 * d:]
    if has_override:
        row = lax.broadcasted_iota(jnp.int32, (tm, 1), 0)
        cu = jnp.where((row % CONV_GRP) >= CONV_GRP // 2, cu, ov_ref[...])
        e_scr[0:8, :] = jnp.zeros((8, d), F32)
    else:
        @pl.when(i % tiles_per_seq == 0)
        def _():
            e_scr[0:8, :] = jnp.zeros((8, d), F32)

        @pl.when(i % tiles_per_seq != 0)
        def _():
            e_scr[0:8, :] = e_scr[tm:tm + 8, :]
    e_scr[8:tm + 8, :] = cu
    cu1 = e_scr[7:tm + 7, :]
    cu2 = e_scr[6:tm + 6, :]
    k = k_ref[...]
    v = k[0:1, :] * cu2 + k[1:2, :] * cu1 + k[2:3, :] * cu
    bv_ref[...] = (b * v).astype(BF16)
    cu_ref[...] = cu


def _conv_call(x, g, sh, sc, w_in, k, override, rows_per_group):
    rows, d = x.shape
    tm = min(512, rows)
    has_override = override is not None
    ms = _mod_spec(sh, tm, rows_per_group)
    in_specs = [pl.BlockSpec((tm, d), lambda i: (i, 0)),
                pl.BlockSpec((1, d), lambda i: (0, 0)),
                ms, ms,
                pl.BlockSpec((d, 3 * d), lambda i: (0, 0)),
                pl.BlockSpec((CONV_WIDTH, d), lambda i: (0, 0))]
    args = [x, g, sh, sc, w_in, k]
    if has_override:
        in_specs.append(pl.BlockSpec((tm, d), lambda i: (i, 0)))
        args.append(override)
    kern = functools.partial(_conv_kernel, tm=tm, d=d, tiles_per_seq=max(rows_per_group // tm, 1),
                             has_override=has_override)
    return pl.pallas_call(
        kern,
        grid=(rows // tm,),
        in_specs=in_specs,
        out_specs=[pl.BlockSpec((tm, d), lambda i: (i, 0)), pl.BlockSpec((tm, d), lambda i: (i, 0))],
        out_shape=[jax.ShapeDtypeStruct((rows, d), BF16), jax.ShapeDtypeStruct((rows, d), F32)],
        scratch_shapes=[pltpu.VMEM((tm + 8, d), F32)],
        compiler_params=_params("arbitrary"),
        name="conv_in",
    )(*args)


def _pool_kernel(*refs, tm, d, tiles_per_seq, sample, pos0):
    x_ref, g_ref, sh_ref, sc_ref, gt_ref, w_ref, ps_ref, o_ref, st_ref, e_scr = refs
    i = pl.program_id(0)
    x = x_ref[...]
    h = _rms_mod(x, g_ref[...], sh_ref[0], sc_ref[0])
    row = lax.broadcasted_iota(jnp.int32, (tm, 1), 0)
    if sample:
        local = (row % POOL_GRP) - POOL_GRP // 2
        h = jnp.where(local >= 0, h, x)
        e_scr[0:16, :] = jnp.zeros((16, d), F32)
    else:
        local = (i % tiles_per_seq) * tm + row

        @pl.when(i % tiles_per_seq == 0)
        def _():
            e_scr[0:16, :] = jnp.zeros((16, d), F32)

        @pl.when(i % tiles_per_seq != 0)
        def _():
            e_scr[0:16, :] = e_scr[tm:tm + 16, :]
    e_scr[16:tm + 16, :] = h
    if sample:
        st_ref[...] = h
    else:
        st_ref[0] = e_scr[tm:tm + 16, :]
    gq = d // len(POOL_WINDOWS)
    for gi, w in enumerate(POOL_WINDOWS):
        lo = gi * gq
        win = h[:, lo:lo + gq]
        for j in range(1, w):
            win = win + e_scr[16 - j:16 - j + tm, lo:lo + gq]
        cnt = jnp.clip(pos0 + local + 1, 1, w).astype(F32)
        dd = (win / cnt - h[:, lo:lo + gq]).astype(BF16)
        y = _dot(dd, w_ref[gi]) * ps_ref[:, lo:lo + gq]
        o_ref[:, lo:lo + gq] = x[:, lo:lo + gq] + gt_ref[0][:, lo:lo + gq] * y


def _pool_call(x, g, sh, sc, gt, w, ps, rows_per_group, sample, pos0):
    rows, d = x.shape
    tm = min(512, rows)
    ms = _mod_spec(sh, tm, rows_per_group)
    gq = d // len(POOL_WINDOWS)
    if sample:
        st_spec = pl.BlockSpec((tm, d), lambda i: (i, 0))
        st_shape = jax.ShapeDtypeStruct((rows, d), F32)
    else:
        st_spec = pl.BlockSpec((1, 16, d), lambda i: (i, 0, 0))
        st_shape = jax.ShapeDtypeStruct((rows // tm, 16, d), F32)
    kern = functools.partial(_pool_kernel, tm=tm, d=d, tiles_per_seq=max(rows_per_group // tm, 1),
                             sample=sample, pos0=pos0)
    return pl.pallas_call(
        kern,
        grid=(rows // tm,),
        in_specs=[pl.BlockSpec((tm, d), lambda i: (i, 0)),
                  pl.BlockSpec((1, d), lambda i: (0, 0)),
                  ms, ms, ms,
                  pl.BlockSpec((len(POOL_WINDOWS), gq, gq), lambda i: (0, 0, 0)),
                  pl.BlockSpec((1, d), lambda i: (0, 0))],
        out_specs=[pl.BlockSpec((tm, d), lambda i: (i, 0)), st_spec],
        out_shape=[jax.ShapeDtypeStruct((rows, d), F32), st_shape],
        scratch_shapes=[pltpu.VMEM((tm + 16, d), F32)],
        compiler_params=_params("arbitrary"),
        name="pool",
    )(x, g, sh, sc, gt, w, ps)


def _swa_qkv_kernel(*refs, tm, dil, keeps, tps):
    x_ref, g_ref, sh_ref, sc_ref, w_ref, gn_ref, seg_ref = refs[:7]
    outs = refs[7:16]
    states = refs[16:16 + len(keeps)]
    z_scr = refs[16 + len(keeps)]
    w = SWA_WIDTH
    tile = pl.program_id(0) % tps
    h = _rms_mod(x_ref[...], g_ref[...], sh_ref[0], sc_ref[0]).astype(BF16)
    for jb in range(9):
        which, g = divmod(jb, 3)
        z = _dot(h, w_ref[:, jb * w:(jb + 1) * w])
        if which < 2:
            ms = _dot((z * z).astype(BF16), seg_ref[...])
            z = z * lax.rsqrt(ms + EPS) * gn_ref[jb:jb + 1, :]
        if keeps and which > 0:
            keep = keeps[g]
            zk = z if keep >= tm else z[tm - keep:, :]

            def put(st_ref=states[g], kv=which - 1, zk=zk):
                st_ref[0, kv] = zk.T

            first_kept = tps - max(keep // tm, 1)
            if first_kept == 0:
                put()
            else:
                pl.when(tile >= first_kept)(put)
        r = dil[g]
        if r == 1:
            outs[jb][0, 0] = z
        else:
            for c in range(w // 128):
                z_scr[c] = z[:, c * 128:(c + 1) * 128]
            for rho in range(r):
                for c in range(w // 128):
                    outs[jb][0, rho, :, c * 128:(c + 1) * 128] = z_scr[c, pl.ds(rho, tm // r, stride=r), :]


def _swa_qkv_call(x, g, sh, sc, w, gn, seg, n_seq, seq_len, dil, rows_per_group, keeps=()):
    rows, d = x.shape
    wd = SWA_WIDTH
    tm = min(512, seq_len)
    tps = seq_len // tm
    ms = _mod_spec(sh, tm, rows_per_group)
    out_specs, out_shape = [], []
    for jb in range(9):
        r = dil[jb % 3]
        out_specs.append(pl.BlockSpec((1, r, tm // r, wd), lambda i: (i // tps, 0, i % tps, 0)))
        out_shape.append(jax.ShapeDtypeStruct((n_seq, r, seq_len // r, wd), F32))
    for keep in keeps:
        assert keep % tm == 0 or tm % keep == 0
        lanes = min(keep, tm)
        first_kept = tps - max(keep // tm, 1)
        out_specs.append(pl.BlockSpec(
            (1, 2, wd, lanes),
            lambda i, first_kept=first_kept: (i // tps, 0, 0, jnp.maximum(i % tps - first_kept, 0))))
        out_shape.append(jax.ShapeDtypeStruct((n_seq, 2, wd, keep), F32))
    return pl.pallas_call(
        functools.partial(_swa_qkv_kernel, tm=tm, dil=dil, keeps=tuple(keeps), tps=tps),
        grid=(rows // tm,),
        in_specs=[pl.BlockSpec((tm, d), lambda i: (i, 0)),
                  pl.BlockSpec((1, d), lambda i: (0, 0)),
                  ms, ms,
                  pl.BlockSpec(w.shape, lambda i: (0, 0), pipeline_mode=pl.Buffered(1)),
                  pl.BlockSpec(gn.shape, lambda i: (0, 0)),
                  pl.BlockSpec(seg.shape, lambda i: (0, 0))],
        out_specs=out_specs,
        out_shape=out_shape,
        scratch_shapes=[pltpu.VMEM((wd // 128, tm, 128), F32)],
        compiler_params=_params("arbitrary"),
        name="swa_qkv",
    )(x, g, sh, sc, w, gn, seg)


def _swa_band_kernel(q_ref, kc_ref, kp_ref, vc_ref, vp_ref, o_ref, l_ref):
    b = pl.program_id(2)
    blk = SWA_BLK
    hd = SWA_HEAD_DIM
    q = q_ref[0, 0]
    k = jnp.concatenate([kp_ref[0, 0], kc_ref[0, 0]], axis=0).astype(BF16)
    vt = jnp.concatenate([vp_ref[0, 0], vc_ref[0, 0]], axis=0).T.astype(BF16)
    key = lax.broadcasted_iota(jnp.int32, (2 * blk, blk), 0)
    qi = lax.broadcasted_iota(jnp.int32, (2 * blk, blk), 1)
    dist = qi + blk - key
    ok = jnp.logical_and(dist >= 0, dist <= jnp.where(b > 0, blk, qi))
    lane_head = lax.broadcasted_iota(jnp.int32, (blk, 2 * hd), 1) // hd
    ots, lts = [], []
    for h in range(SWA_HEADS):
        slab = slice((h // 2) * 2 * hd, (h // 2 + 1) * 2 * hd)
        qm = jnp.where(lane_head == h % 2, q[:, slab], 0.0).astype(BF16)
        st = jnp.where(ok, _dot_nt(k[:, slab], qm) * SWA_SCALE, NEG)
        m = jnp.max(st, axis=0, keepdims=True)
        p = jnp.exp(st - m)
        l = jnp.sum(p, axis=0, keepdims=True)
        ots.append(_dot(vt[h * hd:(h + 1) * hd, :], p.astype(BF16)) * (1.0 / l))
        lts.append(jnp.broadcast_to(m + jnp.log(l), (hd, blk)))
    o_ref[0] = jnp.concatenate(ots, axis=0).T
    l_ref[0] = jnp.concatenate(lts, axis=0).T


def _swa_band_call(q, k, v):
    n_seq, r, lr, w = q.shape
    nb = lr // SWA_BLK
    blk = (1, 1, SWA_BLK, w)
    cur = pl.BlockSpec(blk, lambda n, rho, b: (n, rho, b, 0))
    prev = pl.BlockSpec(blk, lambda n, rho, b: (n, rho, jnp.maximum(b - 1, 0), 0))
    out_spec = pl.BlockSpec((1, SWA_BLK, w), lambda n, rho, b: (n, b, rho))
    o, l = pl.pallas_call(
        _swa_band_kernel,
        grid=(n_seq, r, nb),
        in_specs=[cur, cur, prev, cur, prev],
        out_specs=[out_spec, out_spec],
        out_shape=[jax.ShapeDtypeStruct((n_seq, lr, r * w), F32)] * 2,
        compiler_params=_params("parallel", "parallel", "arbitrary"),
        name="swa_band",
    )(q, k, k, v, v)
    return o, l


def _swa_merge_kernel(*refs, dil, tm):
    o_refs, l_refs = refs[0:3], refs[3:6]
    w_ref, x_ref, gt_ref, out_ref = refs[6:10]
    scrs = list(refs[10:])
    wd = SWA_WIDTH

    def token_order(ref, r):
        if r == 1:
            return ref[0]
        scr = scrs.pop()
        for rho in range(r):
            for c in range(wd // 128):
                lo = rho * wd + c * 128
                scr[c, pl.ds(rho, tm // r, stride=r), :] = ref[0, :, lo:lo + 128]
        return jnp.concatenate([scr[c] for c in range(wd // 128)], axis=1)

    a = [token_order(l_refs[g], dil[g]) for g in range(3)]
    o = [token_order(o_refs[g], dil[g]) for g in range(3)]
    m = jnp.maximum(jnp.maximum(a[0], a[1]), a[2])
    e0, e1, e2 = jnp.exp(a[0] - m), jnp.exp(a[1] - m), jnp.exp(a[2] - m)
    inv = 1.0 / (e0 + e1 + e2)
    merged = (e0 * inv) * o[0] + (e1 * inv) * o[1] + (e2 * inv) * o[2]
    out_ref[...] = x_ref[...] + gt_ref[0] * _dot(merged.astype(BF16), w_ref[...])


def _swa_merge_call(os_, ls_, dil, w_o, x, gt, seq_len, rows_per_group):
    rows, d = x.shape
    tm = min(512, seq_len)
    tps = seq_len // tm
    w = SWA_WIDTH
    specs = [pl.BlockSpec((1, tm // r, r * w), lambda i: (i // tps, i % tps, 0)) for r in dil]
    n_scr = 2 * sum(1 for r in dil if r > 1)
    return pl.pallas_call(
        functools.partial(_swa_merge_kernel, dil=dil, tm=tm),
        grid=(rows // tm,),
        in_specs=specs + specs + [pl.BlockSpec((w, d), lambda i: (0, 0)),
                                  pl.BlockSpec((tm, d), lambda i: (i, 0)),
                                  _mod_spec(gt, tm, rows_per_group)],
        out_specs=pl.BlockSpec((tm, d), lambda i: (i, 0)),
        out_shape=jax.ShapeDtypeStruct((rows, d), F32),
        scratch_shapes=[pltpu.VMEM((w // 128, tm, 128), F32)] * n_scr,
        compiler_params=_params("parallel"),
        name="swa_merge",
    )(*os_, *ls_, w_o, x, gt)


def _swa_step_kernel(q_ref, st_ref, new_ref, ns_ref, o_ref, l_ref, *, win, r, lb, s_new, n_blk):
    w = SWA_WIDTH
    hd = SWA_HEAD_DIM
    nrow = SWA_HEADS * s_new
    pad = 128 - s_new
    rowi = lax.broadcasted_iota(jnp.int32, (nrow, w), 0)
    lanei = lax.broadcasted_iota(jnp.int32, (nrow, w), 1)
    diag = (rowi // s_new) == (lanei // hd)
    s_idx = lax.broadcasted_iota(jnp.int32, (nrow, lb + 128), 0) % s_new
    col = lax.broadcasted_iota(jnp.int32, (nrow, lb + 128), 1)
    dist = lb + s_idx - jnp.where(col < lb, col, col - pad)
    ok = jnp.logical_and(jnp.logical_and(dist >= 0, dist % r == 0), dist <= win)
    ok = jnp.logical_and(ok, jnp.logical_or(col < lb, col >= lb + pad))
    per_slab = 128 // s_new
    for i in range(n_blk):
        if n_blk % per_slab == 0:
            slab, shift = i // per_slab, pad - (i % per_slab) * s_new
        else:
            slab, shift = 0, pad - ((pl.program_id(0) * n_blk + i) % per_slab) * s_new
        new_k = pltpu.roll(new_ref[slab, 0], shift, axis=1)
        new_v = pltpu.roll(new_ref[slab, 1], shift, axis=1)
        q = q_ref[i]
        qbd = jnp.where(diag, jnp.concatenate([q] * SWA_HEADS, axis=0), 0.0).astype(BF16)
        kt = jnp.concatenate([st_ref[i, 0], new_k], axis=1).astype(BF16)
        vt = jnp.concatenate([st_ref[i, 1], new_v], axis=1).astype(BF16)
        sc = jnp.where(ok, _dot(qbd, kt) * SWA_SCALE, NEG)
        m = jnp.max(sc, axis=-1, keepdims=True)
        p = jnp.exp(sc - m)
        l = jnp.sum(p, axis=-1, keepdims=True)
        acc = jnp.where(diag, _dot_nt(p.astype(BF16), vt) / l, 0.0)
        lse = jnp.where(diag, m + jnp.log(l), 0.0)
        o = acc[0:s_new]
        ls = lse[0:s_new]
        for h in range(1, SWA_HEADS):
            o = o + acc[h * s_new:(h + 1) * s_new]
            ls = ls + lse[h * s_new:(h + 1) * s_new]
        o_ref[i] = o
        l_ref[i] = ls
        for kv, new in enumerate((new_k, new_v)):
            ns_ref[i, kv, :, 0:lb - s_new] = st_ref[i, kv, :, s_new:lb]
            ns_ref[i, kv, :, lb - s_new:lb] = new[:, pad:128]


def _swa_step_call(q, st_t, new_t, win, r):
    n_seq, s_new, w = q.shape
    lb = st_t.shape[-1]
    n_blk = max(1, min(n_seq, SWA_GROUPS[-1][0] // lb))
    per_slab = 128 // s_new
    assert n_seq % n_blk == 0 and (n_blk % per_slab == 0 or per_slab % n_blk == 0)
    if n_blk % per_slab == 0:
        new_spec = pl.BlockSpec((n_blk // per_slab, 2, w, 128), lambda n: (n, 0, 0, 0))
    else:
        new_spec = pl.BlockSpec((1, 2, w, 128), lambda n: ((n * n_blk) // per_slab, 0, 0, 0))
    return pl.pallas_call(
        functools.partial(_swa_step_kernel, win=win, r=r, lb=lb, s_new=s_new, n_blk=n_blk),
        grid=(n_seq // n_blk,),
        in_specs=[pl.BlockSpec((n_blk, s_new, w), lambda n: (n, 0, 0)),
                  pl.BlockSpec((n_blk, 2, w, lb), lambda n: (n, 0, 0, 0)),
                  new_spec],
        out_specs=[pl.BlockSpec((n_blk, 2, w, lb), lambda n: (n, 0, 0, 0)),
                   pl.BlockSpec((n_blk, s_new, w), lambda n: (n, 0, 0)),
                   pl.BlockSpec((n_blk, s_new, w), lambda n: (n, 0, 0))],
        out_shape=[jax.ShapeDtypeStruct((n_seq, 2, w, lb), F32),
                   jax.ShapeDtypeStruct((n_seq, s_new, w), F32),
                   jax.ShapeDtypeStruct((n_seq, s_new, w), F32)],
        compiler_params=_params("parallel"),
        name="swa_step",
    )(q, st_t, new_t)


def _rope128(bn, cos_t, sin_t):
    return bn * cos_t + pltpu.roll(bn, 64, axis=1) * sin_t


def _mla_proj_kernel(*refs, decode):
    (x_ref, g_ref, sh_ref, sc_ref, w1_ref, gqa_ref, wqb_ref, gq_ref, gkv_ref, gkr_ref,
     cos_ref, sin_ref, wx_ref) = refs[:13]
    if decode:
        outs = refs[13:]
    else:
        wvt_ref = refs[13]
        outs = refs[14:]
    nh = MLA_HEADS
    if decode:
        lat_ref, qabs_ref, qpe_ref = outs
    else:
        lat_ref, qcat_ref, kcat_ref, vt_ref = outs
    gq = gq_ref[...]
    tm = x_ref.shape[0]
    n_part = 2 if tm % 256 == 0 else 1

    def rows_of(ref, rs):
        v = ref[0]
        return v if v.shape[0] == 1 else v[rs]

    for part in range(n_part):
        rs = slice(part * (tm // n_part), (part + 1) * (tm // n_part))
        h = _rms_mod(x_ref[rs, :], g_ref[...], rows_of(sh_ref, rs), rows_of(sc_ref, rs)).astype(BF16)
        a = _dot(h, w1_ref[...])
        qa = a[:, :Q_LORA]
        qa = (qa * lax.rsqrt(jnp.mean(qa * qa, axis=-1, keepdims=True) + EPS) * gqa_ref[...]).astype(BF16)
        q = _dot(qa, wqb_ref[...])
        cos_t = cos_ref[rs, :]
        sin_t = sin_ref[rs, :]
        latc = a[:, Q_LORA:Q_LORA + KV_LORA]
        latc = latc * lax.rsqrt(jnp.mean(latc * latc, axis=-1, keepdims=True) + EPS) * gkv_ref[...]
        kb = a[:, Q_LORA + KV_LORA:]
        kb = kb * lax.rsqrt(jnp.sum(kb * kb, axis=-1, keepdims=True) * (1.0 / QK_ROPE) + EPS) * gkr_ref[...]
        kb = _rope128(kb, cos_t, sin_t)
        lane = lax.broadcasted_iota(jnp.int32, kb.shape, 1)
        kstd = jnp.where(lane < 32, kb, pltpu.roll(kb, 96, axis=1))
        lat_ref[rs, 0:KV_LORA] = latc
        lat_ref[rs, KV_LORA:KV_LORA + QK_ROPE] = kstd[:, 0:QK_ROPE]
        for hh in range(nh):
            an = q[:, hh * 256:hh * 256 + 128]
            an = an * lax.rsqrt(jnp.mean(an * an, axis=-1, keepdims=True) + EPS) * gq[:, 0:128]
            bn = q[:, hh * 256 + 128:hh * 256 + 256]
            bn = bn * lax.rsqrt(jnp.sum(bn * bn, axis=-1, keepdims=True) * (1.0 / QK_ROPE) + EPS) * gq[:, 128:256]
            bn = _rope128(bn, cos_t, sin_t)
            if decode:
                ag = an * gq[:, 256:384]
                a_hi = ag.astype(BF16)
                a_lo = (ag - a_hi.astype(F32)).astype(BF16)
                qabs_ref[rs, hh * 256:(hh + 1) * 256] = (_dot(a_hi, wx_ref[hh]) + _dot(a_lo, wx_ref[hh])).astype(BF16)
                qpe_ref[rs, hh * 128:(hh + 1) * 128] = jnp.where(lane < 32, bn, pltpu.roll(bn, 96, axis=1)).astype(BF16)
            else:
                qcat_ref[rs, hh * 256:hh * 256 + 128] = an.astype(BF16)
                qcat_ref[rs, hh * 256 + 128:(hh + 1) * 256] = bn.astype(BF16)
        if not decode:
            gkn = gq_ref[:, 256:384]
            latb = latc.astype(BF16)
            kvx = _dot(latb, wx_ref[...])
            kbb = kb.astype(BF16)
            for hh in range(nh):
                kn = kvx[:, hh * 128:(hh + 1) * 128]
                kn = kn * lax.rsqrt(jnp.mean(kn * kn, axis=-1, keepdims=True) + EPS) * gkn
                kcat_ref[rs, hh * 256:hh * 256 + 128] = kn.astype(BF16)
                kcat_ref[rs, hh * 256 + 128:(hh + 1) * 256] = kbb
            vt_ref[0, :, rs] = _dot_nt(wvt_ref[...], latb).astype(BF16)


def _mla_proj_call(x, g, sh, sc, wts, cos_t, sin_t, rows_per_group, decode):
    rows, d = x.shape
    tm = min(512, rows)
    nh = MLA_HEADS
    ms = _mod_spec(sh, tm, rows_per_group)
    tps = max(rows_per_group // tm, 1)
    if decode:
        tab_spec = pl.BlockSpec((tm, 128), lambda i: (0, 0))
        wx = wts["wk_t"]
        wx_spec = pl.BlockSpec(wx.shape, lambda i: (0, 0, 0))
        out_specs = [pl.BlockSpec((tm, KV_LORA + QK_ROPE), lambda i: (i, 0)),
                     pl.BlockSpec((tm, nh * 256), lambda i: (i, 0)),
                     pl.BlockSpec((tm, nh * 128), lambda i: (i, 0))]
        out_shape = [jax.ShapeDtypeStruct((rows, KV_LORA + QK_ROPE), F32),
                     jax.ShapeDtypeStruct((rows, nh * 256), BF16),
                     jax.ShapeDtypeStruct((rows, nh * 128), BF16)]
    else:
        tab_spec = pl.BlockSpec((tm, 128), lambda i: (i % tps, 0))
        wx = wts["w_kn"]
        wx_spec = pl.BlockSpec(wx.shape, lambda i: (0, 0))
        out_specs = [pl.BlockSpec((tm, KV_LORA + QK_ROPE), lambda i: (i, 0)),
                     pl.BlockSpec((tm, nh * 256), lambda i: (i, 0)),
                     pl.BlockSpec((tm, nh * 256), lambda i: (i, 0)),
                     pl.BlockSpec((1, nh * V_DIM, tm), lambda i: (i // tps, 0, i % tps))]
        out_shape = [jax.ShapeDtypeStruct((rows, KV_LORA + QK_ROPE), F32),
                     jax.ShapeDtypeStruct((rows, nh * 256), BF16),
                     jax.ShapeDtypeStruct((rows, nh * 256), BF16),
                     jax.ShapeDtypeStruct((rows // rows_per_group, nh * V_DIM, rows_per_group), BF16)]

    def full(a):
        return pl.BlockSpec(a.shape, lambda i: (0,) * a.ndim)

    extra = [] if decode else [wts["wv_t"]]
    return pl.pallas_call(
        functools.partial(_mla_proj_kernel, decode=decode),
        grid=(rows // tm,),
        in_specs=[pl.BlockSpec((tm, d), lambda i: (i, 0)),
                  pl.BlockSpec((1, d), lambda i: (0, 0)),
                  ms, ms,
                  full(wts["w1"]), full(wts["g_qa"]), full(wts["w_qb"]), full(wts["gq"]),
                  full(wts["g_kva"]), full(wts["g_kr"]),
                  tab_spec, tab_spec, wx_spec] + [full(a) for a in extra],
        out_specs=out_specs,
        out_shape=out_shape,
        compiler_params=_params("parallel"),
        name="mla_proj_decode" if decode else "mla_proj",
    )(x, g, sh, sc, wts["w1"], wts["g_qa"], wts["w_qb"], wts["gq"], wts["g_kva"], wts["g_kr"],
      cos_t, sin_t, wx, *extra)


def _mla_causal_kernel(q_ref, k_ref, vt_ref, o_ref, *, seq_len, tq):
    r = lax.broadcasted_iota(jnp.int32, (tq, tq), 0)
    c = lax.broadcasted_iota(jnp.int32, (tq, tq), 1)
    for qb in range(seq_len // tq):
        lo = qb * tq
        q = q_ref[0, lo:lo + tq, :]
        s_d = jnp.where(r <= c, _dot_nt(k_ref[0, lo:lo + tq, :], q) * MLA_SCALE, NEG)
        m = jnp.max(s_d, axis=0, keepdims=True)
        if qb > 0:
            s_p = _dot_nt(k_ref[0, 0:lo, :], q) * MLA_SCALE
            m = jnp.maximum(m, jnp.max(s_p, axis=0, keepdims=True))
        p_d = jnp.exp(s_d - m)
        l = jnp.sum(p_d, axis=0, keepdims=True)
        ot = _dot(vt_ref[0, :, lo:lo + tq], p_d.astype(BF16))
        if qb > 0:
            p_p = jnp.exp(s_p - m)
            l = l + jnp.sum(p_p, axis=0, keepdims=True)
            ot = ot + _dot(vt_ref[0, :, 0:lo], p_p.astype(BF16))
        o_ref[0, lo:lo + tq, :] = (ot * (1.0 / l)).T.astype(BF16)


def _mla_causal_call(qcat, kcat, vt, n_seq, seq_len):
    nh = MLA_HEADS
    tq = min(256, seq_len)
    q3 = qcat.reshape(n_seq, seq_len, nh * 256)
    k3 = kcat.reshape(n_seq, seq_len, nh * 256)
    o = pl.pallas_call(
        functools.partial(_mla_causal_kernel, seq_len=seq_len, tq=tq),
        grid=(n_seq, nh),
        in_specs=[pl.BlockSpec((1, seq_len, 256), lambda n, h: (n, 0, h)),
                  pl.BlockSpec((1, seq_len, 256), lambda n, h: (n, 0, h)),
                  pl.BlockSpec((1, V_DIM, seq_len), lambda n, h: (n, h, 0))],
        out_specs=pl.BlockSpec((1, seq_len, V_DIM), lambda n, h: (n, 0, h)),
        out_shape=jax.ShapeDtypeStruct((n_seq, seq_len, nh * V_DIM), BF16),
        compiler_params=_params("parallel", "parallel"),
        name="mla_causal",
    )(q3, k3, vt)
    return o.reshape(n_seq * seq_len, nh * V_DIM)


def _mla_decode_kernel(pt_ref, *refs, n_pg, s_new):
    del pt_ref
    page_refs = refs[:n_pg]
    (qabs_ref, qpe_ref, latn_ref, wkt_ref, wv_ref, o_ref, m_scr, l_scr, ctx_scr) = refs[n_pg:]
    c = pl.program_id(1)
    nh = MLA_HEADS
    nrow = nh * s_new

    @pl.when(c == 0)
    def _():
        m_scr[...] = jnp.full_like(m_scr, NEG)
        l_scr[...] = jnp.zeros_like(l_scr)
        ctx_scr[...] = jnp.zeros_like(ctx_scr)

    def process(pages, mask):
        latc = jnp.concatenate([pg[0:KV_LORA, :].astype(BF16) for pg in pages], axis=1)
        kpe = jnp.concatenate([pg[KV_LORA:KV_LORA + QK_ROPE, :].astype(BF16) for pg in pages], axis=1)
        r = _dot(wkt_ref[...], latc)
        rq = _dot(qabs_ref[0], latc)
        spe = _dot(qpe_ref[0][:, 0:QK_ROPE], kpe)
        parts = []
        for h in range(nh):
            blk = r[h * QK_NOPE:(h + 1) * QK_NOPE]
            rstd = lax.rsqrt(jnp.sum(blk * blk, axis=0, keepdims=True) * (1.0 / QK_NOPE) + EPS)
            parts.append(rq[h * s_new:(h + 1) * s_new] * rstd)
        s = (jnp.concatenate(parts, axis=0) + spe) * MLA_SCALE
        if mask is not None:
            s = jnp.where(mask, s, NEG)
        m_old = m_scr[...]
        m_new = jnp.maximum(m_old, jnp.max(s, axis=-1, keepdims=True))
        alpha = jnp.exp(m_old - m_new)
        p = jnp.exp(s - m_new)
        l_scr[...] = alpha * l_scr[...] + jnp.sum(p, axis=-1, keepdims=True)
        ctx_scr[...] = alpha * ctx_scr[...] + _dot_nt(p.astype(BF16), latc)
        m_scr[...] = m_new

    for lo in range(0, n_pg, 8):
        process([pr[0, 0] for pr in page_refs[lo:lo + 8]], None)

    @pl.when(c == pl.num_programs(1) - 1)
    def _():
        si = lax.broadcasted_iota(jnp.int32, (nrow, PAGE_SIZE), 0) % s_new
        kj = lax.broadcasted_iota(jnp.int32, (nrow, PAGE_SIZE), 1)
        process([latn_ref[0]], kj <= si)
        ctx = ctx_scr[...] / l_scr[...]
        for h in range(nh):
            ctx_h = ctx[h * s_new:(h + 1) * s_new].astype(BF16)
            o_ref[0, :, h * V_DIM:(h + 1) * V_DIM] = _dot(ctx_h, wv_ref[h]).astype(BF16)


def _mla_decode_call(cache_t, layer, page_table, qabs, qpe, lat_new, wkt, wv, n_seq, s_new):
    nh = MLA_HEADS
    n_pages = page_table.shape[1]
    n_pg = min(32, n_pages)
    lw = KV_LORA + QK_ROPE
    nrow = nh * s_new

    def page_spec(p):
        return pl.BlockSpec((1, 1, lw, PAGE_SIZE),
                            lambda n, c, pt: (layer, pt[n * n_pages + c * n_pg + p], 0, 0))

    latn_t = jnp.transpose(lat_new.reshape(n_seq, s_new, lw), (0, 2, 1))
    latn_t = jnp.pad(latn_t, ((0, 0), (0, 0), (0, PAGE_SIZE - s_new)))
    grid_spec = pltpu.PrefetchScalarGridSpec(
        num_scalar_prefetch=1,
        grid=(n_seq, n_pages // n_pg),
        in_specs=[page_spec(p) for p in range(n_pg)] + [
            pl.BlockSpec((1, nrow, 256), lambda n, c, pt: (n, 0, 0)),
            pl.BlockSpec((1, nrow, 128), lambda n, c, pt: (n, 0, 0)),
            pl.BlockSpec((1, lw, PAGE_SIZE), lambda n, c, pt: (n, 0, 0)),
            pl.BlockSpec(wkt.shape, lambda n, c, pt: (0, 0)),
            pl.BlockSpec(wv.shape, lambda n, c, pt: (0, 0, 0))],
        out_specs=pl.BlockSpec((1, s_new, nh * V_DIM), lambda n, c, pt: (n, 0, 0)),
        scratch_shapes=[pltpu.VMEM((nrow, 1), F32), pltpu.VMEM((nrow, 1), F32),
                        pltpu.VMEM((nrow, KV_LORA), F32)])
    o = pl.pallas_call(
        functools.partial(_mla_decode_kernel, n_pg=n_pg, s_new=s_new),
        grid_spec=grid_spec,
        out_shape=jax.ShapeDtypeStruct((n_seq, s_new, nh * V_DIM), BF16),
        compiler_params=_params("parallel", "arbitrary"),
        name="mla_decode",
    )(page_table.reshape(-1), *([cache_t] * n_pg),
      jnp.transpose(qabs.reshape(n_seq, s_new, nh, 256), (0, 2, 1, 3)).reshape(n_seq, nrow, 256),
      jnp.transpose(qpe.reshape(n_seq, s_new, nh, 128), (0, 2, 1, 3)).reshape(n_seq, nrow, 128),
      latn_t, wkt, wv)
    return o.reshape(n_seq * s_new, nh * V_DIM)


def _mla_weights(w_qa, g_qa, w_qb, w_kva, g_kva, w_kvb, g_qn, g_kn, g_qr, g_kr):
    nh = MLA_HEADS
    half = QK_ROPE // 2

    def pad_rope_cols(w):
        z = jnp.zeros(w.shape[:-1] + (half,), w.dtype)
        return jnp.concatenate([w[..., :half], z, w[..., half:], z], axis=-1)

    w_kva_p = jnp.concatenate([w_kva[:, :KV_LORA], pad_rope_cols(w_kva[:, KV_LORA:])], axis=1)
    w1 = jnp.concatenate([w_qa, w_kva_p], axis=1).astype(BF16)
    wq = w_qb.reshape(Q_LORA, nh, QK_NOPE + QK_ROPE)
    wq = jnp.concatenate([wq[..., :QK_NOPE], pad_rope_cols(wq[..., QK_NOPE:])], axis=-1)
    w_qb_p = wq.reshape(Q_LORA, nh * 256).astype(BF16)
    gq = jnp.concatenate([g_qn, pad_rope_cols(g_qr), g_kn])[None, :]
    wkv = w_kvb.reshape(KV_LORA, nh, QK_NOPE + V_DIM)
    w_kn = wkv[..., :QK_NOPE].reshape(KV_LORA, nh * QK_NOPE).astype(BF16)
    wv_t = jnp.transpose(wkv[..., QK_NOPE:], (1, 2, 0)).reshape(nh * V_DIM, KV_LORA).astype(BF16)
    wk_t = jnp.transpose(wkv[..., :QK_NOPE], (1, 2, 0)).astype(BF16)
    return dict(w1=w1, g_qa=g_qa[None, :], w_qb=w_qb_p, gq=gq, g_kva=g_kva[None, :],
                g_kr=pad_rope_cols(g_kr)[None, :], w_kn=w_kn, wv_t=wv_t, wk_t=wk_t,
                wkt_flat=wk_t.reshape(nh * QK_NOPE, KV_LORA),
                wv=jnp.transpose(wkv[..., QK_NOPE:], (1, 0, 2)).astype(BF16))


def _rope_tables(pos):
    half = QK_ROPE // 2
    inv = ROPE_BASE ** (-jnp.arange(half, dtype=F32) / half)
    ang = pos.astype(F32)[:, None] * inv[None, :]
    cos, sin = jnp.cos(ang), jnp.sin(ang)
    z = jnp.zeros_like(cos)
    return (jnp.concatenate([cos, z, cos, z], axis=1), jnp.concatenate([-sin, z, sin, z], axis=1))


def kernel(x_prompt, x_sample, state_conv, state_swa0, state_swa1, state_swa2, state_pool, cache_mla, page_table,
           c_prompt, c_sample, ada_w, ada_b, norm1_g, norm2_g, mlp_w1, mlp_w2, conv_w_in, conv_k, conv_w_out,
           swa_w_qkv, swa_qn_g, swa_kn_g, swa_w_o, pool_w, pool_scale, mla_w_qa, mla_g_qa, mla_w_qb, mla_w_kva,
           mla_g_kva, mla_w_kvb, mla_g_qn, mla_g_kn, mla_g_qr, mla_g_kr, mla_w_o):
    bp, t, d = x_prompt.shape
    bs, s, _ = x_sample.shape
    depth = ada_w.shape[0]
    past = page_table.shape[1] * PAGE_SIZE
    swa_states = (state_swa0, state_swa1, state_swa2)
    for (win, _), st in zip(SWA_GROUPS, swa_states):
        assert st.shape[2] == win, "sliding-window state must hold a full window"
    assert t % (SWA_GROUPS[-1][1] * SWA_BLK) == 0, "prompt length must split into full dilated blocks"
    rows_p, rows_s = bp * t, bs * s
    xp = x_prompt.reshape(rows_p, d)
    xs = x_sample.reshape(rows_s, d)

    mod = _ada_call(jnp.concatenate([c_prompt, c_sample], axis=0), ada_w, ada_b)

    def mods(i):
        out_p, out_s = [], []
        for j in range(6):
            m = mod[i, :, j * d:(j + 1) * d]
            out_p.append(m[:bp, None, :])
            out_s.append(m[bp:])
        return out_p, out_s

    def tok(m, tm):
        return _per_token(m, s, tm)

    tm_s512 = min(512, rows_s)
    tm_s1024 = min(1024, rows_s)
    conv_p = conv_s = pool_p = pool_s = mla_p = mla_s = None
    swa_p, swa_s = [None] * 3, [None] * 3

    for i in range(depth):
        kind, j = i % 4, i // 4
        (sh1p, sc1p, g1p, sh2p, sc2p, g2p), (sh1s, sc1s, g1s, sh2s, sc2s, g2s) = mods(i)
        n1 = norm1_g[i][None, :]
        if kind == 0:
            w_in = conv_w_in[j].astype(BF16)
            w_out = conv_w_out[j].astype(BF16)
            bv, cu = _conv_call(xp, n1, sh1p, sc1p, w_in, conv_k[j], None, t)
            conv_p = cu.reshape(bp, t, d)[:, t - (CONV_WIDTH - 1):]
            xp = _outproj_call(bv, w_out, xp, g1p, t)
            grp = CONV_GRP
            pad = jnp.zeros((bs, grp - s, d), F32)
            x_ext = jnp.concatenate([pad, xs.reshape(bs, s, d)], axis=1).reshape(bs * grp, d)
            ov = jnp.concatenate([jnp.zeros((bs, grp - s - 2, d), F32), state_conv[j], jnp.zeros((bs, s, d), F32)],
                                 axis=1).reshape(bs * grp, d)
            tme = min(512, bs * grp)
            she = _per_token(sh1s, grp, tme)
            sce = _per_token(sc1s, grp, tme)
            bv, cu = _conv_call(x_ext, n1, she, sce, w_in, conv_k[j], ov, grp)
            conv_s = cu.reshape(bs, grp, d)[:, grp - (CONV_WIDTH - 1):]
            bv = bv.reshape(bs, grp, d)[:, grp - s:].reshape(rows_s, d)
            xs = _outproj_call(bv, w_out, xs, tok(g1s, tm_s512), s)
        elif kind == 1:
            w_qkv = swa_w_qkv[j].astype(BF16)
            w_o = swa_w_o[j].astype(BF16)
            gn = jnp.concatenate([jnp.tile(swa_qn_g[j], (1, SWA_HEADS)), jnp.tile(swa_kn_g[j], (1, SWA_HEADS))],
                                 axis=0)
            seg = (jnp.arange(SWA_WIDTH)[:, None] // SWA_HEAD_DIM == jnp.arange(SWA_WIDTH)[None, :] // SWA_HEAD_DIM)
            seg = (seg.astype(F32) / SWA_HEAD_DIM).astype(BF16)
            dil = tuple(r for _, r in SWA_GROUPS)
            keeps = tuple(min(win, t) for win, _ in SWA_GROUPS)
            qkv = _swa_qkv_call(xp, n1, sh1p, sc1p, w_qkv, gn, seg, bp, t, dil, t, keeps)
            os_, ls_ = [], []
            for g, (win, r) in enumerate(SWA_GROUPS):
                o_g, l_g = _swa_band_call(qkv[g], qkv[3 + g], qkv[6 + g])
                os_.append(o_g)
                ls_.append(l_g)
                st = qkv[9 + g].reshape(bp, 2, SWA_HEADS, SWA_HEAD_DIM, keeps[g])
                swa_p[g] = jnp.transpose(st, (0, 4, 1, 2, 3))[None]
            xp = _swa_merge_call(os_, ls_, dil, w_o, xp, g1p, t, t)
            tm_q = min(512, rows_s)
            qkv_s = _swa_qkv_call(xs, n1, tok(sh1s, tm_q), tok(sc1s, tm_q), w_qkv, gn, seg, 1, rows_s,
                                  (1, 1, 1), s)
            os_, ls_ = [], []
            for g, (win, r) in enumerate(SWA_GROUPS):
                q_g, k_g, v_g = (qkv_s[c * 3 + g].reshape(bs, s, SWA_WIDTH) for c in range(3))
                st_t = jnp.transpose(swa_states[g][j], (0, 2, 3, 4, 1)).reshape(bs, 2, SWA_WIDTH, win)
                per_slab = 128 // s
                new_t = jnp.stack([k_g, v_g], axis=1).reshape(bs // per_slab, per_slab, 2, s, SWA_WIDTH)
                new_t = jnp.transpose(new_t, (0, 2, 4, 1, 3)).reshape(bs // per_slab, 2, SWA_WIDTH, 128)
                ns, o_g, l_g = _swa_step_call(q_g, st_t, new_t, win, r)
                swa_s[g] = jnp.transpose(ns.reshape(bs, 2, SWA_HEADS, SWA_HEAD_DIM, win), (0, 4, 1, 2, 3))[None]
                os_.append(o_g.reshape(1, rows_s, SWA_WIDTH))
                ls_.append(l_g.reshape(1, rows_s, SWA_WIDTH))
            xs = _swa_merge_call(os_, ls_, (1, 1, 1), w_o, xs, tok(g1s, tm_s512), rows_s, s)
        elif kind == 2:
            pw = pool_w[j].astype(BF16)
            ps = pool_scale[j][None, :]
            xp, tail = _pool_call(xp, n1, sh1p, sc1p, g1p, pw, ps, t, False, 0)
            tpt = t // min(512, rows_p)
            pool_p = tail.reshape(bp, tpt, 16, d)[:, -1, 1:]
            grp = POOL_GRP
            x_ext = jnp.concatenate([jnp.zeros((bs, 1, d), F32), state_pool[j], xs.reshape(bs, s, d),
                                     jnp.zeros((bs, grp - 16 - s, d), F32)], axis=1).reshape(bs * grp, d)
            tme = min(512, bs * grp)
            y_ext, ext = _pool_call(x_ext, n1, _per_token(sh1s, grp, tme), _per_token(sc1s, grp, tme),
                                    _per_token(g1s, grp, tme), pw, ps, grp, True, past)
            xs = y_ext.reshape(bs, grp, d)[:, 16:16 + s].reshape(rows_s, d)
            pool_s = ext.reshape(bs, grp, d)[:, 16 + s - POOL_HIST:16 + s]
        else:
            wts = _mla_weights(mla_w_qa[j], mla_g_qa[j], mla_w_qb[j], mla_w_kva[j], mla_g_kva[j], mla_w_kvb[j],
                               mla_g_qn[j], mla_g_kn[j], mla_g_qr[j], mla_g_kr[j])
            w_o = mla_w_o[j].astype(BF16)
            cos_p, sin_p = _rope_tables(jnp.arange(t))
            lat_p, qcat, kcat, vt = _mla_proj_call(xp, n1, sh1p, sc1p, wts, cos_p, sin_p, t, False)
            o_p = _mla_causal_call(qcat, kcat, vt, bp, t)
            xp = _outproj_call(o_p, w_o, xp, g1p, t)
            mla_p = lat_p.reshape(bp, t, KV_LORA + QK_ROPE)
            cos_s, sin_s = _rope_tables(past + jnp.arange(s))
            reps = tm_s512 // s
            cos_s, sin_s = jnp.tile(cos_s, (reps, 1)), jnp.tile(sin_s, (reps, 1))
            lat_s, qabs, qpe = _mla_proj_call(xs, n1, tok(sh1s, tm_s512), tok(sc1s, tm_s512), wts, cos_s, sin_s, s, True)
            cache_t = jnp.transpose(cache_mla, (0, 1, 3, 2))
            o_s = _mla_decode_call(cache_t, j, page_table, qabs, qpe, lat_s, wts["wkt_flat"], wts["wv"], bs, s)
            xs = _outproj_call(o_s, w_o, xs, tok(g1s, tm_s512), s)
            mla_s = lat_s.reshape(bs, s, KV_LORA + QK_ROPE)
        n2 = norm2_g[i][None, :]
        w1 = mlp_w1[i].astype(BF16)
        w2 = mlp_w2[i].astype(BF16)
        xp = _mlp_call(xp, n2, sh2p, sc2p, g2p, w1, w2, t)
        xs = _mlp_call(xs, n2, tok(sh2s, tm_s1024), tok(sc2s, tm_s1024), tok(g2s, tm_s1024), w1, w2, s)

    return (xp.reshape(bp, t, d), xs.reshape(bs, s, d), conv_p[None], conv_s[None],
            swa_p[0], swa_s[0], swa_p[1], swa_s[1], swa_p[2], swa_s[2],
            pool_p[None], pool_s[None], mla_p[None], mla_s[None])
```

```python
import functools

import jax
import jax.numpy as jnp
from jax import lax
from jax.experimental import pallas as pl
from jax.experimental.pallas import tpu as pltpu

F32 = jnp.float32
BF16 = jnp.bfloat16

EPS = 1e-6
NEG = -1e30

CONV_WIDTH = 3
SWA_GROUPS = ((128, 1), (512, 4), (2048, 16))
SWA_HEADS = 8
SWA_HEAD_DIM = 64
SWA_WIDTH = SWA_HEADS * SWA_HEAD_DIM
SWA_BLK = 128
SWA_SCALE = SWA_HEAD_DIM ** -0.5
POOL_WINDOWS = (2, 4, 8, 16)
POOL_HIST = 15
MLA_HEADS = 8
Q_LORA = 384
KV_LORA = 256
QK_NOPE = 128
QK_ROPE = 64
V_DIM = 128
MLA_SCALE = (QK_NOPE + QK_ROPE) ** -0.5
ROPE_BASE = 10000.0
PAGE_SIZE = 128

VMEM_LIMIT_BYTES = 56 * 1024 * 1024
CONV_GRP = 16
POOL_GRP = 32


def _params(*sem):
    return pltpu.CompilerParams(dimension_semantics=sem, vmem_limit_bytes=VMEM_LIMIT_BYTES)


def _rms_mod(x, g, shift, scale):
    y = x * lax.rsqrt(jnp.mean(x * x, axis=-1, keepdims=True) + EPS)
    return (y * g) * (1.0 + scale) + shift


def _dot(a, b):
    return jnp.dot(a, b, preferred_element_type=F32)


def _dot_nt(a, b):
    return lax.dot_general(a, b, (((1,), (1,)), ((), ())), preferred_element_type=F32)


def _mod_spec(m, tm, rows_per_group):
    d = m.shape[-1]
    if m.shape[1] == 1:
        tpg = rows_per_group // tm
        return pl.BlockSpec((1, 1, d), lambda i, *_: (i // tpg, 0, 0))
    return pl.BlockSpec((1, tm, d), lambda i, *_: (i, 0, 0))


def _per_token(m, reps, tm):
    e = jnp.repeat(m, reps, axis=0)
    return e.reshape(e.shape[0] // tm, tm, e.shape[1])


def _ada_kernel(c_ref, w_ref, b_ref, o_ref):
    c = c_ref[...]
    s = (c * (1.0 / (1.0 + jnp.exp(-c)))).astype(BF16)
    o_ref[0] = _dot(s, w_ref[0].astype(BF16)) + b_ref[0]


def _ada_call(c, ada_w, ada_b):
    depth, d, n6 = ada_w.shape
    n = c.shape[0]
    tn = 1536
    return pl.pallas_call(
        _ada_kernel,
        grid=(depth, n6 // tn),
        in_specs=[pl.BlockSpec((n, d), lambda i, j: (0, 0)),
                  pl.BlockSpec((1, d, tn), lambda i, j: (i, 0, j)),
                  pl.BlockSpec((1, 1, tn), lambda i, j: (i, 0, j))],
        out_specs=pl.BlockSpec((1, n, tn), lambda i, j: (i, 0, j)),
        out_shape=jax.ShapeDtypeStruct((depth, n, n6), F32),
        compiler_params=_params("parallel", "parallel"),
        name="adaln",
    )(c, ada_w, ada_b.reshape(depth, 1, n6))


def _mlp_kernel(*refs, mixer_proj):
    if mixer_proj:
        x_ref, a_ref, wm_ref, gt1_ref = refs[:4]
        refs = refs[:1] + refs[4:]
    x_ref, g_ref, sh_ref, sc_ref, gt_ref, w1_ref, w2_ref, o_ref, h_scr, acc_scr = refs[:10]
    j = pl.program_id(1)

    @pl.when(j == 0)
    def _():
        x = x_ref[...]
        if mixer_proj:
            x = x + gt1_ref[0] * _dot(a_ref[...], wm_ref[...])
            refs[10][...] = x
        h_scr[...] = _rms_mod(x, g_ref[...], sh_ref[0], sc_ref[0]).astype(BF16)
        acc_scr[...] = jnp.zeros_like(acc_scr)

    a = _dot(h_scr[...], w1_ref[...])
    a = jnp.square(jnp.maximum(a, 0.0)).astype(BF16)
    acc_scr[...] += _dot(a, w2_ref[...])

    @pl.when(j == pl.num_programs(1) - 1)
    def _():
        x1 = refs[10][...] if mixer_proj else x_ref[...]
        o_ref[...] = x1 + gt_ref[0] * acc_scr[...]


def _mlp_call(x, g, sh, sc, gt, w1, w2, rows_per_group, mixer_proj=None):
    rows, d = x.shape
    f = w1.shape[1]
    tm = min(1024, rows)
    tf = 512
    ms = _mod_spec(sh, tm, rows_per_group)
    in_specs = [pl.BlockSpec((tm, d), lambda i, j: (i, 0))]
    args = [x]
    scratch = [pltpu.VMEM((tm, d), BF16), pltpu.VMEM((tm, d), F32)]
    if mixer_proj is not None:
        a, w_mix, gt1 = mixer_proj
        in_specs += [pl.BlockSpec((tm, a.shape[1]), lambda i, j: (i, 0)),
                     pl.BlockSpec(w_mix.shape, lambda i, j: (0, 0)),
                     ms]
        args += [a, w_mix, gt1]
        scratch.append(pltpu.VMEM((tm, d), F32))
    in_specs += [pl.BlockSpec((1, d), lambda i, j: (0, 0)),
                 ms, ms, ms,
                 pl.BlockSpec((d, tf), lambda i, j: (0, j)),
                 pl.BlockSpec((tf, d), lambda i, j: (j, 0))]
    return pl.pallas_call(
        functools.partial(_mlp_kernel, mixer_proj=mixer_proj is not None),
        grid=(rows // tm, f // tf),
        in_specs=in_specs,
        out_specs=pl.BlockSpec((tm, d), lambda i, j: (i, 0)),
        out_shape=jax.ShapeDtypeStruct((rows, d), F32),
        scratch_shapes=scratch,
        compiler_params=_params("parallel", "arbitrary"),
        name="mlp",
    )(*args, g, sh, sc, gt, w1, w2)


def _conv_kernel(*refs, tm, d, tiles_per_seq, has_override):
    if has_override:
        x_ref, g_ref, sh_ref, sc_ref, w_ref, k_ref, ov_ref, bv_ref, cu_ref, e_scr = refs
    else:
        x_ref, g_ref, sh_ref, sc_ref, w_ref, k_ref, bv_ref, cu_ref, e_scr = refs
    i = pl.program_id(0)
    h = _rms_mod(x_ref[...], g_ref[...], sh_ref[0], sc_ref[0]).astype(BF16)
    proj = _dot(h, w_ref[...])
    b = proj[:, :d]
    cu = proj[:, d:2 * d] * proj[:, 2 * d:]
    if has_override:
        row = lax.broadcasted_iota(jnp.int32, (tm, 1), 0)
        cu = jnp.where((row % CONV_GRP) >= CONV_GRP // 2, cu, ov_ref[...])
        e_scr[0:8, :] = jnp.zeros((8, d), F32)
    else:
        @pl.when(i % tiles_per_seq == 0)
        def _():
            e_scr[0:8, :] = jnp.zeros((8, d), F32)

        @pl.when(i % tiles_per_seq != 0)
        def _():
            e_scr[0:8, :] = e_scr[tm:tm + 8, :]
    e_scr[8:tm + 8, :] = cu
    cu1 = e_scr[7:tm + 7, :]
    cu2 = e_scr[6:tm + 6, :]
    k = k_ref[...]
    v = k[0:1, :] * cu2 + k[1:2, :] * cu1 + k[2:3, :] * cu
    bv_ref[...] = (b * v).astype(BF16)
    cu_ref[...] = cu


def _conv_call(x, g, sh, sc, w_in, k, override, rows_per_group):
    rows, d = x.shape
    tm = min(512, rows)
    has_override = override is not None
    ms = _mod_spec(sh, tm, rows_per_group)
    in_specs = [pl.BlockSpec((tm, d), lambda i: (i, 0)),
                pl.BlockSpec((1, d), lambda i: (0, 0)),
                ms, ms,
                pl.BlockSpec((d, 3 * d), lambda i: (0, 0)),
                pl.BlockSpec((CONV_WIDTH, d), lambda i: (0, 0))]
    args = [x, g, sh, sc, w_in, k]
    if has_override:
        in_specs.append(pl.BlockSpec((tm, d), lambda i: (i, 0)))
        args.append(override)
    kern = functools.partial(_conv_kernel, tm=tm, d=d, tiles_per_seq=max(rows_per_group // tm, 1),
                             has_override=has_override)
    return pl.pallas_call(
        kern,
        grid=(rows // tm,),
        in_specs=in_specs,
        out_specs=[pl.BlockSpec((tm, d), lambda i: (i, 0)), pl.BlockSpec((tm, d), lambda i: (i, 0))],
        out_shape=[jax.ShapeDtypeStruct((rows, d), BF16), jax.ShapeDtypeStruct((rows, d), F32)],
        scratch_shapes=[pltpu.VMEM((tm + 8, d), F32)],
        compiler_params=_params("arbitrary"),
        name="conv_in",
    )(*args)


def _pool_kernel(*refs, tm, d, tiles_per_seq, sample, pos0):
    x_ref, g_ref, sh_ref, sc_ref, gt_ref, w_ref, ps_ref, o_ref, st_ref, e_scr = refs
    i = pl.program_id(0)
    x = x_ref[...]
    h = _rms_mod(x, g_ref[...], sh_ref[0], sc_ref[0])
    row = lax.broadcasted_iota(jnp.int32, (tm, 1), 0)
    if sample:
        local = (row % POOL_GRP) - POOL_GRP // 2
        h = jnp.where(local >= 0, h, x)
        e_scr[0:16, :] = jnp.zeros((16, d), F32)
    else:
        local = (i % tiles_per_seq) * tm + row

        @pl.when(i % tiles_per_seq == 0)
        def _():
            e_scr[0:16, :] = jnp.zeros((16, d), F32)

        @pl.when(i % tiles_per_seq != 0)
        def _():
            e_scr[0:16, :] = e_scr[tm:tm + 16, :]
    e_scr[16:tm + 16, :] = h
    if sample:
        st_ref[...] = h
    else:
        st_ref[0] = e_scr[tm:tm + 16, :]
    gq = d // len(POOL_WINDOWS)
    for gi, w in enumerate(POOL_WINDOWS):
        lo = gi * gq
        win = h[:, lo:lo + gq]
        for j in range(1, w):
            win = win + e_scr[16 - j:16 - j + tm, lo:lo + gq]
        cnt = jnp.clip(pos0 + local + 1, 1, w).astype(F32)
        dd = (win / cnt - h[:, lo:lo + gq]).astype(BF16)
        y = _dot(dd, w_ref[gi]) * ps_ref[:, lo:lo + gq]
        o_ref[:, lo:lo + gq] = x[:, lo:lo + gq] + gt_ref[0][:, lo:lo + gq] * y


def _pool_call(x, g, sh, sc, gt, w, ps, rows_per_group, sample, pos0):
    rows, d = x.shape
    tm = min(512, rows)
    ms = _mod_spec(sh, tm, rows_per_group)
    gq = d // len(POOL_WINDOWS)
    if sample:
        st_spec = pl.BlockSpec((tm, d), lambda i: (i, 0))
        st_shape = jax.ShapeDtypeStruct((rows, d), F32)
    else:
        st_spec = pl.BlockSpec((1, 16, d), lambda i: (i, 0, 0))
        st_shape = jax.ShapeDtypeStruct((rows // tm, 16, d), F32)
    kern = functools.partial(_pool_kernel, tm=tm, d=d, tiles_per_seq=max(rows_per_group // tm, 1),
                             sample=sample, pos0=pos0)
    return pl.pallas_call(
        kern,
        grid=(rows // tm,),
        in_specs=[pl.BlockSpec((tm, d), lambda i: (i, 0)),
                  pl.BlockSpec((1, d), lambda i: (0, 0)),
                  ms, ms, ms,
                  pl.BlockSpec((len(POOL_WINDOWS), gq, gq), lambda i: (0, 0, 0)),
                  pl.BlockSpec((1, d), lambda i: (0, 0))],
        out_specs=[pl.BlockSpec((tm, d), lambda i: (i, 0)), st_spec],
        out_shape=[jax.ShapeDtypeStruct((rows, d), F32), st_shape],
        scratch_shapes=[pltpu.VMEM((tm + 16, d), F32)],
        compiler_params=_params("arbitrary"),
        name="pool",
    )(x, g, sh, sc, gt, w, ps)


def _swa_qkv_kernel(*refs, tm, dil, keeps, tps):
    x_ref, g_ref, sh_ref, sc_ref, w_ref, gn_ref, seg_ref = refs[:7]
    outs = refs[7:16]
    states = refs[16:16 + len(keeps)]
    z_scr = refs[16 + len(keeps)]
    w = SWA_WIDTH
    tile = pl.program_id(0) % tps
    h = _rms_mod(x_ref[...], g_ref[...], sh_ref[0], sc_ref[0]).astype(BF16)
    for jb in range(9):
        which, g = divmod(jb, 3)
        z = _dot(h, w_ref[:, jb * w:(jb + 1) * w])
        if which < 2:
            ms = _dot((z * z).astype(BF16), seg_ref[...])
            z = z * lax.rsqrt(ms + EPS) * gn_ref[jb:jb + 1, :]
        if keeps and which > 0:
            keep = keeps[g]
            zk = z if keep >= tm else z[tm - keep:, :]

            def put(st_ref=states[g], kv=which - 1, zk=zk):
                st_ref[0, kv] = zk.T

            first_kept = tps - max(keep // tm, 1)
            if first_kept == 0:
                put()
            else:
                pl.when(tile >= first_kept)(put)
        r = dil[g]
        if r == 1:
            outs[jb][0, 0] = z
        else:
            for c in range(w // 128):
                z_scr[c] = z[:, c * 128:(c + 1) * 128]
            for rho in range(r):
                for c in range(w // 128):
                    outs[jb][0, rho, :, c * 128:(c + 1) * 128] = z_scr[c, pl.ds(rho, tm // r, stride=r), :]


def _swa_qkv_call(x, g, sh, sc, w, gn, seg, n_seq, seq_len, dil, rows_per_group, keeps=()):
    rows, d = x.shape
    wd = SWA_WIDTH
    tm = min(512, seq_len)
    tps = seq_len // tm
    ms = _mod_spec(sh, tm, rows_per_group)
    out_specs, out_shape = [], []
    for jb in range(9):
        r = dil[jb % 3]
        out_specs.append(pl.BlockSpec((1, r, tm // r, wd), lambda i: (i // tps, 0, i % tps, 0)))
        out_shape.append(jax.ShapeDtypeStruct((n_seq, r, seq_len // r, wd), F32))
    for keep in keeps:
        assert keep % tm == 0 or tm % keep == 0
        lanes = min(keep, tm)
        first_kept = tps - max(keep // tm, 1)
        out_specs.append(pl.BlockSpec(
            (1, 2, wd, lanes),
            lambda i, first_kept=first_kept: (i // tps, 0, 0, jnp.maximum(i % tps - first_kept, 0))))
        out_shape.append(jax.ShapeDtypeStruct((n_seq, 2, wd, keep), F32))
    return pl.pallas_call(
        functools.partial(_swa_qkv_kernel, tm=tm, dil=dil, keeps=tuple(keeps), tps=tps),
        grid=(rows // tm,),
        in_specs=[pl.BlockSpec((tm, d), lambda i: (i, 0)),
                  pl.BlockSpec((1, d), lambda i: (0, 0)),
                  ms, ms,
                  pl.BlockSpec(w.shape, lambda i: (0, 0), pipeline_mode=pl.Buffered(1)),
                  pl.BlockSpec(gn.shape, lambda i: (0, 0)),
                  pl.BlockSpec(seg.shape, lambda i: (0, 0))],
        out_specs=out_specs,
        out_shape=out_shape,
        scratch_shapes=[pltpu.VMEM((wd // 128, tm, 128), F32)],
        compiler_params=_params("arbitrary"),
        name="swa_qkv",
    )(x, g, sh, sc, w, gn, seg)


def _swa_band_kernel(q_ref, kc_ref, kp_ref, vc_ref, vp_ref, o_ref, l_ref):
    b = pl.program_id(2)
    blk = SWA_BLK
    hd = SWA_HEAD_DIM
    key = lax.broadcasted_iota(jnp.int32, (2 * blk, blk), 0)
    qi = lax.broadcasted_iota(jnp.int32, (2 * blk, blk), 1)
    dist = qi + blk - key
    ok = jnp.logical_and(dist >= 0, dist <= jnp.where(b > 0, blk, qi))
    lane_head = lax.broadcasted_iota(jnp.int32, (blk, 2 * hd), 1) // hd
    for i in range(q_ref.shape[0]):
        q = q_ref[i, 0]
        k = jnp.concatenate([kp_ref[i, 0], kc_ref[i, 0]], axis=0).astype(BF16)
        vt = jnp.concatenate([vp_ref[i, 0], vc_ref[i, 0]], axis=0).T.astype(BF16)
        ots, lts = [], []
        for h in range(SWA_HEADS):
            slab = slice((h // 2) * 2 * hd, (h // 2 + 1) * 2 * hd)
            qm = jnp.where(lane_head == h % 2, q[:, slab], 0.0).astype(BF16)
            st = jnp.where(ok, _dot_nt(k[:, slab], qm) * SWA_SCALE, NEG)
            m = jnp.max(st, axis=0, keepdims=True)
            p = jnp.exp(st - m)
            l = jnp.sum(p, axis=0, keepdims=True)
            ots.append(_dot(vt[h * hd:(h + 1) * hd, :], p.astype(BF16)) * (1.0 / l))
            lts.append(jnp.broadcast_to(m + jnp.log(l), (hd, blk)))
        o_ref[i] = jnp.concatenate(ots, axis=0).T
        l_ref[i] = jnp.concatenate(lts, axis=0).T


def _swa_band_call(q, k, v):
    n_seq, r, lr, w = q.shape
    nb = lr // SWA_BLK
    ns = next(c for c in (8, 4, 2, 1) if n_seq % c == 0)
    blk = (ns, 1, SWA_BLK, w)
    cur = pl.BlockSpec(blk, lambda n, rho, b: (n, rho, b, 0))
    prev = pl.BlockSpec(blk, lambda n, rho, b: (n, rho, jnp.maximum(b - 1, 0), 0))
    out_spec = pl.BlockSpec((ns, SWA_BLK, w), lambda n, rho, b: (n, b, rho))
    o, l = pl.pallas_call(
        _swa_band_kernel,
        grid=(n_seq // ns, r, nb),
        in_specs=[cur, cur, prev, cur, prev],
        out_specs=[out_spec, out_spec],
        out_shape=[jax.ShapeDtypeStruct((n_seq, lr, r * w), F32)] * 2,
        compiler_params=_params("parallel", "parallel", "arbitrary"),
        name="swa_band",
    )(q, k, k, v, v)
    return o, l


def _swa_merge_kernel(*refs, dil, tm):
    o_refs, l_refs = refs[0:3], refs[3:6]
    w_ref, x_ref, gt_ref, out_ref = refs[6:10]
    scrs = list(refs[10:])
    wd = SWA_WIDTH

    def token_order(ref, r):
        if r == 1:
            return ref[0]
        scr = scrs.pop()
        for rho in range(r):
            for c in range(wd // 128):
                lo = rho * wd + c * 128
                scr[c, pl.ds(rho, tm // r, stride=r), :] = ref[0, :, lo:lo + 128]
        return jnp.concatenate([scr[c] for c in range(wd // 128)], axis=1)

    a = [token_order(l_refs[g], dil[g]) for g in range(3)]
    o = [token_order(o_refs[g], dil[g]) for g in range(3)]
    m = jnp.maximum(jnp.maximum(a[0], a[1]), a[2])
    e0, e1, e2 = jnp.exp(a[0] - m), jnp.exp(a[1] - m), jnp.exp(a[2] - m)
    inv = 1.0 / (e0 + e1 + e2)
    merged = (e0 * inv) * o[0] + (e1 * inv) * o[1] + (e2 * inv) * o[2]
    out_ref[...] = x_ref[...] + gt_ref[0] * _dot(merged.astype(BF16), w_ref[...])


def _swa_merge_call(os_, ls_, dil, w_o, x, gt, seq_len, rows_per_group):
    rows, d = x.shape
    tm = min(512, seq_len)
    tps = seq_len // tm
    w = SWA_WIDTH
    specs = [pl.BlockSpec((1, tm // r, r * w), lambda i: (i // tps, i % tps, 0)) for r in dil]
    n_scr = 2 * sum(1 for r in dil if r > 1)
    return pl.pallas_call(
        functools.partial(_swa_merge_kernel, dil=dil, tm=tm),
        grid=(rows // tm,),
        in_specs=specs + specs + [pl.BlockSpec((w, d), lambda i: (0, 0)),
                                  pl.BlockSpec((tm, d), lambda i: (i, 0)),
                                  _mod_spec(gt, tm, rows_per_group)],
        out_specs=pl.BlockSpec((tm, d), lambda i: (i, 0)),
        out_shape=jax.ShapeDtypeStruct((rows, d), F32),
        scratch_shapes=[pltpu.VMEM((w // 128, tm, 128), F32)] * n_scr,
        compiler_params=_params("parallel"),
        name="swa_merge",
    )(*os_, *ls_, w_o, x, gt)


def _swa_step_kernel(q_ref, st_ref, new_ref, ns_ref, o_ref, l_ref, *, win, r, lb, s_new, n_blk):
    w = SWA_WIDTH
    hd = SWA_HEAD_DIM
    nrow = SWA_HEADS * s_new
    pad = 128 - s_new
    rowi = lax.broadcasted_iota(jnp.int32, (nrow, w), 0)
    lanei = lax.broadcasted_iota(jnp.int32, (nrow, w), 1)
    diag = (rowi // s_new) == (lanei // hd)
    s_idx = lax.broadcasted_iota(jnp.int32, (nrow, lb + 128), 0) % s_new
    col = lax.broadcasted_iota(jnp.int32, (nrow, lb + 128), 1)
    dist = lb + s_idx - jnp.where(col < lb, col, col - pad)
    ok = jnp.logical_and(jnp.logical_and(dist >= 0, dist % r == 0), dist <= win)
    ok = jnp.logical_and(ok, jnp.logical_or(col < lb, col >= lb + pad))
    per_slab = 128 // s_new
    for i in range(n_blk):
        if n_blk % per_slab == 0:
            slab, shift = i // per_slab, pad - (i % per_slab) * s_new
        else:
            slab, shift = 0, pad - ((pl.program_id(0) * n_blk + i) % per_slab) * s_new
        new_k = pltpu.roll(new_ref[slab, 0], shift, axis=1)
        new_v = pltpu.roll(new_ref[slab, 1], shift, axis=1)
        q = q_ref[i]
        qbd = jnp.where(diag, jnp.concatenate([q] * SWA_HEADS, axis=0), 0.0).astype(BF16)
        kt = jnp.concatenate([st_ref[i, 0], new_k], axis=1).astype(BF16)
        vt = jnp.concatenate([st_ref[i, 1], new_v], axis=1).astype(BF16)
        sc = jnp.where(ok, _dot(qbd, kt) * SWA_SCALE, NEG)
        m = jnp.max(sc, axis=-1, keepdims=True)
        p = jnp.exp(sc - m)
        l = jnp.sum(p, axis=-1, keepdims=True)
        acc = jnp.where(diag, _dot_nt(p.astype(BF16), vt) / l, 0.0)
        lse = jnp.where(diag, m + jnp.log(l), 0.0)
        o = acc[0:s_new]
        ls = lse[0:s_new]
        for h in range(1, SWA_HEADS):
            o = o + acc[h * s_new:(h + 1) * s_new]
            ls = ls + lse[h * s_new:(h + 1) * s_new]
        o_ref[i] = o
        l_ref[i] = ls
        for kv, new in enumerate((new_k, new_v)):
            ns_ref[i, kv, :, 0:lb - s_new] = st_ref[i, kv, :, s_new:lb]
            ns_ref[i, kv, :, lb - s_new:lb] = new[:, pad:128]


def _swa_step_call(q, st_t, new_t, win, r):
    n_seq, s_new, w = q.shape
    lb = st_t.shape[-1]
    n_blk = max(1, min(n_seq, SWA_GROUPS[-1][0] // lb))
    per_slab = 128 // s_new
    assert n_seq % n_blk == 0 and (n_blk % per_slab == 0 or per_slab % n_blk == 0)
    if n_blk % per_slab == 0:
        new_spec = pl.BlockSpec((n_blk // per_slab, 2, w, 128), lambda n: (n, 0, 0, 0))
    else:
        new_spec = pl.BlockSpec((1, 2, w, 128), lambda n: ((n * n_blk) // per_slab, 0, 0, 0))
    return pl.pallas_call(
        functools.partial(_swa_step_kernel, win=win, r=r, lb=lb, s_new=s_new, n_blk=n_blk),
        grid=(n_seq // n_blk,),
        in_specs=[pl.BlockSpec((n_blk, s_new, w), lambda n: (n, 0, 0)),
                  pl.BlockSpec((n_blk, 2, w, lb), lambda n: (n, 0, 0, 0)),
                  new_spec],
        out_specs=[pl.BlockSpec((n_blk, 2, w, lb), lambda n: (n, 0, 0, 0)),
                   pl.BlockSpec((n_blk, s_new, w), lambda n: (n, 0, 0)),
                   pl.BlockSpec((n_blk, s_new, w), lambda n: (n, 0, 0))],
        out_shape=[jax.ShapeDtypeStruct((n_seq, 2, w, lb), F32),
                   jax.ShapeDtypeStruct((n_seq, s_new, w), F32),
                   jax.ShapeDtypeStruct((n_seq, s_new, w), F32)],
        compiler_params=_params("parallel"),
        name="swa_step",
    )(q, st_t, new_t)


def _rope128(bn, cos_t, sin_t):
    return bn * cos_t + pltpu.roll(bn, 64, axis=1) * sin_t


def _mla_proj_kernel(*refs, decode):
    (x_ref, g_ref, sh_ref, sc_ref, w1_ref, gqa_ref, wqb_ref, gq_ref, gkv_ref, gkr_ref,
     cos_ref, sin_ref, wx_ref) = refs[:13]
    if decode:
        outs = refs[13:]
    else:
        wvt_ref = refs[13]
        outs = refs[14:]
    nh = MLA_HEADS
    if decode:
        lat_ref, qabs_ref, qpe_ref = outs
    else:
        lat_ref, qcat_ref, kcat_ref, vt_ref = outs
    gq = gq_ref[...]
    tm = x_ref.shape[0]
    n_part = 2 if tm % 256 == 0 else 1

    def rows_of(ref, rs):
        v = ref[0]
        return v if v.shape[0] == 1 else v[rs]

    for part in range(n_part):
        rs = slice(part * (tm // n_part), (part + 1) * (tm // n_part))
        h = _rms_mod(x_ref[rs, :], g_ref[...], rows_of(sh_ref, rs), rows_of(sc_ref, rs)).astype(BF16)
        a = _dot(h, w1_ref[...])
        qa = a[:, :Q_LORA]
        qa = (qa * lax.rsqrt(jnp.mean(qa * qa, axis=-1, keepdims=True) + EPS) * gqa_ref[...]).astype(BF16)
        q = _dot(qa, wqb_ref[...])
        cos_t = cos_ref[rs, :]
        sin_t = sin_ref[rs, :]
        latc = a[:, Q_LORA:Q_LORA + KV_LORA]
        latc = latc * lax.rsqrt(jnp.mean(latc * latc, axis=-1, keepdims=True) + EPS) * gkv_ref[...]
        kb = a[:, Q_LORA + KV_LORA:]
        kb = kb * lax.rsqrt(jnp.sum(kb * kb, axis=-1, keepdims=True) * (1.0 / QK_ROPE) + EPS) * gkr_ref[...]
        kb = _rope128(kb, cos_t, sin_t)
        lane = lax.broadcasted_iota(jnp.int32, kb.shape, 1)
        kstd = jnp.where(lane < 32, kb, pltpu.roll(kb, 96, axis=1))
        lat_ref[rs, 0:KV_LORA] = latc
        lat_ref[rs, KV_LORA:KV_LORA + QK_ROPE] = kstd[:, 0:QK_ROPE]
        for hh in range(nh):
            an = q[:, hh * 256:hh * 256 + 128]
            an = an * lax.rsqrt(jnp.mean(an * an, axis=-1, keepdims=True) + EPS) * gq[:, 0:128]
            bn = q[:, hh * 256 + 128:hh * 256 + 256]
            bn = bn * lax.rsqrt(jnp.sum(bn * bn, axis=-1, keepdims=True) * (1.0 / QK_ROPE) + EPS) * gq[:, 128:256]
            bn = _rope128(bn, cos_t, sin_t)
            if decode:
                ag = an * gq[:, 256:384]
                a_hi = ag.astype(BF16)
                a_lo = (ag - a_hi.astype(F32)).astype(BF16)
                qabs_ref[rs, hh * 256:(hh + 1) * 256] = (_dot(a_hi, wx_ref[hh]) + _dot(a_lo, wx_ref[hh])).astype(BF16)
                qpe_ref[rs, hh * 128:(hh + 1) * 128] = jnp.where(lane < 32, bn, pltpu.roll(bn, 96, axis=1)).astype(BF16)
            else:
                qcat_ref[rs, hh * 256:hh * 256 + 128] = an.astype(BF16)
                qcat_ref[rs, hh * 256 + 128:(hh + 1) * 256] = bn.astype(BF16)
        if not decode:
            gkn = gq_ref[:, 256:384]
            latb = latc.astype(BF16)
            kvx = _dot(latb, wx_ref[...])
            kbb = kb.astype(BF16)
            for hh in range(nh):
                kn = kvx[:, hh * 128:(hh + 1) * 128]
                kn = kn * lax.rsqrt(jnp.mean(kn * kn, axis=-1, keepdims=True) + EPS) * gkn
                kcat_ref[rs, hh * 256:hh * 256 + 128] = kn.astype(BF16)
                kcat_ref[rs, hh * 256 + 128:(hh + 1) * 256] = kbb
            vt_ref[0, :, rs] = _dot_nt(wvt_ref[...], latb).astype(BF16)


def _mla_proj_call(x, g, sh, sc, wts, cos_t, sin_t, rows_per_group, decode):
    rows, d = x.shape
    tm = min(512, rows)
    nh = MLA_HEADS
    ms = _mod_spec(sh, tm, rows_per_group)
    tps = max(rows_per_group // tm, 1)
    if decode:
        tab_spec = pl.BlockSpec((tm, 128), lambda i: (0, 0))
        wx = wts["wk_t"]
        wx_spec = pl.BlockSpec(wx.shape, lambda i: (0, 0, 0))
        out_specs = [pl.BlockSpec((tm, KV_LORA + QK_ROPE), lambda i: (i, 0)),
                     pl.BlockSpec((tm, nh * 256), lambda i: (i, 0)),
                     pl.BlockSpec((tm, nh * 128), lambda i: (i, 0))]
        out_shape = [jax.ShapeDtypeStruct((rows, KV_LORA + QK_ROPE), F32),
                     jax.ShapeDtypeStruct((rows, nh * 256), BF16),
                     jax.ShapeDtypeStruct((rows, nh * 128), BF16)]
    else:
        tab_spec = pl.BlockSpec((tm, 128), lambda i: (i % tps, 0))
        wx = wts["w_kn"]
        wx_spec = pl.BlockSpec(wx.shape, lambda i: (0, 0))
        out_specs = [pl.BlockSpec((tm, KV_LORA + QK_ROPE), lambda i: (i, 0)),
                     pl.BlockSpec((tm, nh * 256), lambda i: (i, 0)),
                     pl.BlockSpec((tm, nh * 256), lambda i: (i, 0)),
                     pl.BlockSpec((1, nh * V_DIM, tm), lambda i: (i // tps, 0, i % tps))]
        out_shape = [jax.ShapeDtypeStruct((rows, KV_LORA + QK_ROPE), F32),
                     jax.ShapeDtypeStruct((rows, nh * 256), BF16),
                     jax.ShapeDtypeStruct((rows, nh * 256), BF16),
                     jax.ShapeDtypeStruct((rows // rows_per_group, nh * V_DIM, rows_per_group), BF16)]

    def full(a):
        return pl.BlockSpec(a.shape, lambda i: (0,) * a.ndim)

    extra = [] if decode else [wts["wv_t"]]
    return pl.pallas_call(
        functools.partial(_mla_proj_kernel, decode=decode),
        grid=(rows // tm,),
        in_specs=[pl.BlockSpec((tm, d), lambda i: (i, 0)),
                  pl.BlockSpec((1, d), lambda i: (0, 0)),
                  ms, ms,
                  full(wts["w1"]), full(wts["g_qa"]), full(wts["w_qb"]), full(wts["gq"]),
                  full(wts["g_kva"]), full(wts["g_kr"]),
                  tab_spec, tab_spec, wx_spec] + [full(a) for a in extra],
        out_specs=out_specs,
        out_shape=out_shape,
        compiler_params=_params("parallel"),
        name="mla_proj_decode" if decode else "mla_proj",
    )(x, g, sh, sc, wts["w1"], wts["g_qa"], wts["w_qb"], wts["gq"], wts["g_kva"], wts["g_kr"],
      cos_t, sin_t, wx, *extra)


def _mla_causal_kernel(q_ref, k_ref, vt_ref, o_ref, *, seq_len, tq):
    r = lax.broadcasted_iota(jnp.int32, (tq, tq), 0)
    c = lax.broadcasted_iota(jnp.int32, (tq, tq), 1)
    for qb in range(seq_len // tq):
        lo = qb * tq
        q = q_ref[0, lo:lo + tq, :]
        s_d = jnp.where(r <= c, _dot_nt(k_ref[0, lo:lo + tq, :], q) * MLA_SCALE, NEG)
        m = jnp.max(s_d, axis=0, keepdims=True)
        if qb > 0:
            s_p = _dot_nt(k_ref[0, 0:lo, :], q) * MLA_SCALE
            m = jnp.maximum(m, jnp.max(s_p, axis=0, keepdims=True))
        p_d = jnp.exp(s_d - m)
        l = jnp.sum(p_d, axis=0, keepdims=True)
        ot = _dot(vt_ref[0, :, lo:lo + tq], p_d.astype(BF16))
        if qb > 0:
            p_p = jnp.exp(s_p - m)
            l = l + jnp.sum(p_p, axis=0, keepdims=True)
            ot = ot + _dot(vt_ref[0, :, 0:lo], p_p.astype(BF16))
        o_ref[0, lo:lo + tq, :] = (ot * (1.0 / l)).T.astype(BF16)


def _mla_causal_call(qcat, kcat, vt, n_seq, seq_len):
    nh = MLA_HEADS
    tq = min(512, seq_len)
    q3 = qcat.reshape(n_seq, seq_len, nh * 256)
    k3 = kcat.reshape(n_seq, seq_len, nh * 256)
    o = pl.pallas_call(
        functools.partial(_mla_causal_kernel, seq_len=seq_len, tq=tq),
        grid=(n_seq, nh),
        in_specs=[pl.BlockSpec((1, seq_len, 256), lambda n, h: (n, 0, h)),
                  pl.BlockSpec((1, seq_len, 256), lambda n, h: (n, 0, h)),
                  pl.BlockSpec((1, V_DIM, seq_len), lambda n, h: (n, h, 0))],
        out_specs=pl.BlockSpec((1, seq_len, V_DIM), lambda n, h: (n, 0, h)),
        out_shape=jax.ShapeDtypeStruct((n_seq, seq_len, nh * V_DIM), BF16),
        compiler_params=_params("parallel", "parallel"),
        name="mla_causal",
    )(q3, k3, vt)
    return o.reshape(n_seq * seq_len, nh * V_DIM)


def _mla_decode_kernel(pt_ref, *refs, n_pg, s_new):
    del pt_ref
    page_refs = refs[:n_pg]
    (qabs_ref, qpe_ref, latn_ref, wkt_ref, wv_ref, o_ref, m_scr, l_scr, ctx_scr) = refs[n_pg:]
    c = pl.program_id(1)
    nh = MLA_HEADS
    nrow = nh * s_new

    @pl.when(c == 0)
    def _():
        m_scr[...] = jnp.full_like(m_scr, NEG)
        l_scr[...] = jnp.zeros_like(l_scr)
        ctx_scr[...] = jnp.zeros_like(ctx_scr)

    def process(pages, mask):
        latc = jnp.concatenate([pg[0:KV_LORA, :].astype(BF16) for pg in pages], axis=1)
        kpe = jnp.concatenate([pg[KV_LORA:KV_LORA + QK_ROPE, :].astype(BF16) for pg in pages], axis=1)
        r = _dot(wkt_ref[...], latc)
        rq = _dot(qabs_ref[0], latc)
        spe = _dot(qpe_ref[0][:, 0:QK_ROPE], kpe)
        parts = []
        for h in range(nh):
            blk = r[h * QK_NOPE:(h + 1) * QK_NOPE]
            rstd = lax.rsqrt(jnp.sum(blk * blk, axis=0, keepdims=True) * (1.0 / QK_NOPE) + EPS)
            parts.append(rq[h * s_new:(h + 1) * s_new] * rstd)
        s = (jnp.concatenate(parts, axis=0) + spe) * MLA_SCALE
        if mask is not None:
            s = jnp.where(mask, s, NEG)
        m_old = m_scr[...]
        m_new = jnp.maximum(m_old, jnp.max(s, axis=-1, keepdims=True))
        alpha = jnp.exp(m_old - m_new)
        p = jnp.exp(s - m_new)
        l_scr[...] = alpha * l_scr[...] + jnp.sum(p, axis=-1, keepdims=True)
        ctx_scr[...] = alpha * ctx_scr[...] + _dot_nt(p.astype(BF16), latc)
        m_scr[...] = m_new

    for lo in range(0, n_pg, 8):
        process([pr[0, 0] for pr in page_refs[lo:lo + 8]], None)

    @pl.when(c == pl.num_programs(1) - 1)
    def _():
        si = lax.broadcasted_iota(jnp.int32, (nrow, PAGE_SIZE), 0) % s_new
        kj = lax.broadcasted_iota(jnp.int32, (nrow, PAGE_SIZE), 1)
        process([latn_ref[0]], kj <= si)
        ctx = ctx_scr[...] / l_scr[...]
        for h in range(nh):
            ctx_h = ctx[h * s_new:(h + 1) * s_new].astype(BF16)
            o_ref[0, :, h * V_DIM:(h + 1) * V_DIM] = _dot(ctx_h, wv_ref[h]).astype(BF16)


def _mla_decode_call(cache_t, layer, page_table, qabs, qpe, lat_new, wkt, wv, n_seq, s_new):
    nh = MLA_HEADS
    n_pages = page_table.shape[1]
    n_pg = min(32, n_pages)
    lw = KV_LORA + QK_ROPE
    nrow = nh * s_new

    def page_spec(p):
        return pl.BlockSpec((1, 1, lw, PAGE_SIZE),
                            lambda n, c, pt: (layer, pt[n * n_pages + c * n_pg + p], 0, 0))

    latn_t = jnp.transpose(lat_new.reshape(n_seq, s_new, lw), (0, 2, 1))
    latn_t = jnp.pad(latn_t, ((0, 0), (0, 0), (0, PAGE_SIZE - s_new)))
    grid_spec = pltpu.PrefetchScalarGridSpec(
        num_scalar_prefetch=1,
        grid=(n_seq, n_pages // n_pg),
        in_specs=[page_spec(p) for p in range(n_pg)] + [
            pl.BlockSpec((1, nrow, 256), lambda n, c, pt: (n, 0, 0)),
            pl.BlockSpec((1, nrow, 128), lambda n, c, pt: (n, 0, 0)),
            pl.BlockSpec((1, lw, PAGE_SIZE), lambda n, c, pt: (n, 0, 0)),
            pl.BlockSpec(wkt.shape, lambda n, c, pt: (0, 0)),
            pl.BlockSpec(wv.shape, lambda n, c, pt: (0, 0, 0))],
        out_specs=pl.BlockSpec((1, s_new, nh * V_DIM), lambda n, c, pt: (n, 0, 0)),
        scratch_shapes=[pltpu.VMEM((nrow, 1), F32), pltpu.VMEM((nrow, 1), F32),
                        pltpu.VMEM((nrow, KV_LORA), F32)])
    o = pl.pallas_call(
        functools.partial(_mla_decode_kernel, n_pg=n_pg, s_new=s_new),
        grid_spec=grid_spec,
        out_shape=jax.ShapeDtypeStruct((n_seq, s_new, nh * V_DIM), BF16),
        compiler_params=_params("parallel", "arbitrary"),
        name="mla_decode",
    )(page_table.reshape(-1), *([cache_t] * n_pg),
      jnp.transpose(qabs.reshape(n_seq, s_new, nh, 256), (0, 2, 1, 3)).reshape(n_seq, nrow, 256),
      jnp.transpose(qpe.reshape(n_seq, s_new, nh, 128), (0, 2, 1, 3)).reshape(n_seq, nrow, 128),
      latn_t, wkt, wv)
    return o.reshape(n_seq * s_new, nh * V_DIM)


def _mla_weights(w_qa, g_qa, w_qb, w_kva, g_kva, w_kvb, g_qn, g_kn, g_qr, g_kr):
    nh = MLA_HEADS
    half = QK_ROPE // 2

    def pad_rope_cols(w):
        z = jnp.zeros(w.shape[:-1] + (half,), w.dtype)
        return jnp.concatenate([w[..., :half], z, w[..., half:], z], axis=-1)

    w_kva_p = jnp.concatenate([w_kva[:, :KV_LORA], pad_rope_cols(w_kva[:, KV_LORA:])], axis=1)
    w1 = jnp.concatenate([w_qa, w_kva_p], axis=1).astype(BF16)
    wq = w_qb.reshape(Q_LORA, nh, QK_NOPE + QK_ROPE)
    wq = jnp.concatenate([wq[..., :QK_NOPE], pad_rope_cols(wq[..., QK_NOPE:])], axis=-1)
    w_qb_p = wq.reshape(Q_LORA, nh * 256).astype(BF16)
    gq = jnp.concatenate([g_qn, pad_rope_cols(g_qr), g_kn])[None, :]
    wkv = w_kvb.reshape(KV_LORA, nh, QK_NOPE + V_DIM)
    w_kn = wkv[..., :QK_NOPE].reshape(KV_LORA, nh * QK_NOPE).astype(BF16)
    wv_t = jnp.transpose(wkv[..., QK_NOPE:], (1, 2, 0)).reshape(nh * V_DIM, KV_LORA).astype(BF16)
    wk_t = jnp.transpose(wkv[..., :QK_NOPE], (1, 2, 0)).astype(BF16)
    return dict(w1=w1, g_qa=g_qa[None, :], w_qb=w_qb_p, gq=gq, g_kva=g_kva[None, :],
                g_kr=pad_rope_cols(g_kr)[None, :], w_kn=w_kn, wv_t=wv_t, wk_t=wk_t,
                wkt_flat=wk_t.reshape(nh * QK_NOPE, KV_LORA),
                wv=jnp.transpose(wkv[..., QK_NOPE:], (1, 0, 2)).astype(BF16))


def _rope_tables(pos):
    half = QK_ROPE // 2
    inv = ROPE_BASE ** (-jnp.arange(half, dtype=F32) / half)
    ang = pos.astype(F32)[:, None] * inv[None, :]
    cos, sin = jnp.cos(ang), jnp.sin(ang)
    z = jnp.zeros_like(cos)
    return (jnp.concatenate([cos, z, cos, z], axis=1), jnp.concatenate([-sin, z, sin, z], axis=1))


def kernel(x_prompt, x_sample, state_conv, state_swa0, state_swa1, state_swa2, state_pool, cache_mla, page_table,
           c_prompt, c_sample, ada_w, ada_b, norm1_g, norm2_g, mlp_w1, mlp_w2, conv_w_in, conv_k, conv_w_out,
           swa_w_qkv, swa_qn_g, swa_kn_g, swa_w_o, pool_w, pool_scale, mla_w_qa, mla_g_qa, mla_w_qb, mla_w_kva,
           mla_g_kva, mla_w_kvb, mla_g_qn, mla_g_kn, mla_g_qr, mla_g_kr, mla_w_o):
    bp, t, d = x_prompt.shape
    bs, s, _ = x_sample.shape
    depth = ada_w.shape[0]
    past = page_table.shape[1] * PAGE_SIZE
    swa_states = (state_swa0, state_swa1, state_swa2)
    for (win, _), st in zip(SWA_GROUPS, swa_states):
        assert st.shape[2] == win, "sliding-window state must hold a full window"
    assert t % (SWA_GROUPS[-1][1] * SWA_BLK) == 0, "prompt length must split into full dilated blocks"
    rows_p, rows_s = bp * t, bs * s
    xp = x_prompt.reshape(rows_p, d)
    xs = x_sample.reshape(rows_s, d)

    mod = _ada_call(jnp.concatenate([c_prompt, c_sample], axis=0), ada_w, ada_b)

    def mods(i):
        out_p, out_s = [], []
        for j in range(6):
            m = mod[i, :, j * d:(j + 1) * d]
            out_p.append(m[:bp, None, :])
            out_s.append(m[bp:])
        return out_p, out_s

    def tok(m, tm):
        return _per_token(m, s, tm)

    tm_s512 = min(512, rows_s)
    tm_s1024 = min(1024, rows_s)
    conv_p = conv_s = pool_p = pool_s = mla_p = mla_s = None
    swa_p, swa_s = [None] * 3, [None] * 3

    for i in range(depth):
        kind, j = i % 4, i // 4
        (sh1p, sc1p, g1p, sh2p, sc2p, g2p), (sh1s, sc1s, g1s, sh2s, sc2s, g2s) = mods(i)
        n1 = norm1_g[i][None, :]
        proj_p = proj_s = None
        if kind == 0:
            w_in = conv_w_in[j].astype(BF16)
            w_out = conv_w_out[j].astype(BF16)
            bv, cu = _conv_call(xp, n1, sh1p, sc1p, w_in, conv_k[j], None, t)
            conv_p = cu.reshape(bp, t, d)[:, t - (CONV_WIDTH - 1):]
            proj_p = (bv, w_out, g1p)
            grp = CONV_GRP
            pad = jnp.zeros((bs, grp - s, d), F32)
            x_ext = jnp.concatenate([pad, xs.reshape(bs, s, d)], axis=1).reshape(bs * grp, d)
            ov = jnp.concatenate([jnp.zeros((bs, grp - s - 2, d), F32), state_conv[j], jnp.zeros((bs, s, d), F32)],
                                 axis=1).reshape(bs * grp, d)
            tme = min(512, bs * grp)
            she = _per_token(sh1s, grp, tme)
            sce = _per_token(sc1s, grp, tme)
            bv, cu = _conv_call(x_ext, n1, she, sce, w_in, conv_k[j], ov, grp)
            conv_s = cu.reshape(bs, grp, d)[:, grp - (CONV_WIDTH - 1):]
            bv = bv.reshape(bs, grp, d)[:, grp - s:].reshape(rows_s, d)
            proj_s = (bv, w_out, tok(g1s, tm_s1024))
        elif kind == 1:
            w_qkv = swa_w_qkv[j].astype(BF16)
            w_o = swa_w_o[j].astype(BF16)
            gn = jnp.concatenate([jnp.tile(swa_qn_g[j], (1, SWA_HEADS)), jnp.tile(swa_kn_g[j], (1, SWA_HEADS))],
                                 axis=0)
            seg = (jnp.arange(SWA_WIDTH)[:, None] // SWA_HEAD_DIM == jnp.arange(SWA_WIDTH)[None, :] // SWA_HEAD_DIM)
            seg = (seg.astype(F32) / SWA_HEAD_DIM).astype(BF16)
            dil = tuple(r for _, r in SWA_GROUPS)
            keeps = tuple(min(win, t) for win, _ in SWA_GROUPS)
            qkv = _swa_qkv_call(xp, n1, sh1p, sc1p, w_qkv, gn, seg, bp, t, dil, t, keeps)
            os_, ls_ = [], []
            for g, (win, r) in enumerate(SWA_GROUPS):
                o_g, l_g = _swa_band_call(qkv[g], qkv[3 + g], qkv[6 + g])
                os_.append(o_g)
                ls_.append(l_g)
                st = qkv[9 + g].reshape(bp, 2, SWA_HEADS, SWA_HEAD_DIM, keeps[g])
                swa_p[g] = jnp.transpose(st, (0, 4, 1, 2, 3))[None]
            xp = _swa_merge_call(os_, ls_, dil, w_o, xp, g1p, t, t)
            tm_q = min(512, rows_s)
            qkv_s = _swa_qkv_call(xs, n1, tok(sh1s, tm_q), tok(sc1s, tm_q), w_qkv, gn, seg, 1, rows_s,
                                  (1, 1, 1), s)
            os_, ls_ = [], []
            for g, (win, r) in enumerate(SWA_GROUPS):
                q_g, k_g, v_g = (qkv_s[c * 3 + g].reshape(bs, s, SWA_WIDTH) for c in range(3))
                st_t = jnp.transpose(swa_states[g][j], (0, 2, 3, 4, 1)).reshape(bs, 2, SWA_WIDTH, win)
                per_slab = 128 // s
                new_t = jnp.stack([k_g, v_g], axis=1).reshape(bs // per_slab, per_slab, 2, s, SWA_WIDTH)
                new_t = jnp.transpose(new_t, (0, 2, 4, 1, 3)).reshape(bs // per_slab, 2, SWA_WIDTH, 128)
                ns, o_g, l_g = _swa_step_call(q_g, st_t, new_t, win, r)
                swa_s[g] = jnp.transpose(ns.reshape(bs, 2, SWA_HEADS, SWA_HEAD_DIM, win), (0, 4, 1, 2, 3))[None]
                os_.append(o_g.reshape(1, rows_s, SWA_WIDTH))
                ls_.append(l_g.reshape(1, rows_s, SWA_WIDTH))
            xs = _swa_merge_call(os_, ls_, (1, 1, 1), w_o, xs, tok(g1s, tm_s512), rows_s, s)
        elif kind == 2:
            pw = pool_w[j].astype(BF16)
            ps = pool_scale[j][None, :]
            xp, tail = _pool_call(xp, n1, sh1p, sc1p, g1p, pw, ps, t, False, 0)
            tpt = t // min(512, rows_p)
            pool_p = tail.reshape(bp, tpt, 16, d)[:, -1, 1:]
            grp = POOL_GRP
            x_ext = jnp.concatenate([jnp.zeros((bs, 1, d), F32), state_pool[j], xs.reshape(bs, s, d),
                                     jnp.zeros((bs, grp - 16 - s, d), F32)], axis=1).reshape(bs * grp, d)
            tme = min(512, bs * grp)
            y_ext, ext = _pool_call(x_ext, n1, _per_token(sh1s, grp, tme), _per_token(sc1s, grp, tme),
                                    _per_token(g1s, grp, tme), pw, ps, grp, True, past)
            xs = y_ext.reshape(bs, grp, d)[:, 16:16 + s].reshape(rows_s, d)
            pool_s = ext.reshape(bs, grp, d)[:, 16 + s - POOL_HIST:16 + s]
        else:
            wts = _mla_weights(mla_w_qa[j], mla_g_qa[j], mla_w_qb[j], mla_w_kva[j], mla_g_kva[j], mla_w_kvb[j],
                               mla_g_qn[j], mla_g_kn[j], mla_g_qr[j], mla_g_kr[j])
            w_o = mla_w_o[j].astype(BF16)
            cos_p, sin_p = _rope_tables(jnp.arange(t))
            lat_p, qcat, kcat, vt = _mla_proj_call(xp, n1, sh1p, sc1p, wts, cos_p, sin_p, t, False)
            o_p = _mla_causal_call(qcat, kcat, vt, bp, t)
            proj_p = (o_p, w_o, g1p)
            mla_p = lat_p.reshape(bp, t, KV_LORA + QK_ROPE)
            cos_s, sin_s = _rope_tables(past + jnp.arange(s))
            reps = tm_s512 // s
            cos_s, sin_s = jnp.tile(cos_s, (reps, 1)), jnp.tile(sin_s, (reps, 1))
            lat_s, qabs, qpe = _mla_proj_call(xs, n1, tok(sh1s, tm_s512), tok(sc1s, tm_s512), wts, cos_s, sin_s, s, True)
            cache_t = jnp.transpose(cache_mla, (0, 1, 3, 2))
            o_s = _mla_decode_call(cache_t, j, page_table, qabs, qpe, lat_s, wts["wkt_flat"], wts["wv"], bs, s)
            proj_s = (o_s, w_o, tok(g1s, tm_s1024))
            mla_s = lat_s.reshape(bs, s, KV_LORA + QK_ROPE)
        n2 = norm2_g[i][None, :]
        w1 = mlp_w1[i].astype(BF16)
        w2 = mlp_w2[i].astype(BF16)
        xp = _mlp_call(xp, n2, sh2p, sc2p, g2p, w1, w2, t, proj_p)
        xs = _mlp_call(xs, n2, tok(sh2s, tm_s1024), tok(sc2s, tm_s1024), tok(g2s, tm_s1024), w1, w2, s, proj_s)

    return (xp.reshape(bp, t, d), xs.reshape(bs, s, d), conv_p[None], conv_s[None],
            swa_p[0], swa_s[0], swa_p[1], swa_s[1], swa_p[2], swa_s[2],
            pool_p[None], pool_s[None], mla_p[None], mla_s[None])
```

```python
import functools

import jax
import jax.numpy as jnp
from jax import lax
from jax.experimental import pallas as pl
from jax.experimental.pallas import tpu as pltpu

F32 = jnp.float32
BF16 = jnp.bfloat16

EPS = 1e-6
NEG = -1e30

CONV_WIDTH = 3
SWA_GROUPS = ((128, 1), (512, 4), (2048, 16))
SWA_HEADS = 8
SWA_HEAD_DIM = 64
SWA_WIDTH = SWA_HEADS * SWA_HEAD_DIM
SWA_BLK = 128
SWA_SCALE = SWA_HEAD_DIM ** -0.5
POOL_WINDOWS = (2, 4, 8, 16)
POOL_HIST = 15
MLA_HEADS = 8
Q_LORA = 384
KV_LORA = 256
QK_NOPE = 128
QK_ROPE = 64
V_DIM = 128
MLA_SCALE = (QK_NOPE + QK_ROPE) ** -0.5
LOG2_E = 1.4426950408889634
ROPE_BASE = 10000.0
PAGE_SIZE = 128

VMEM_LIMIT_BYTES = 56 * 1024 * 1024
CONV_GRP = 16
POOL_GRP = 32


def _params(*sem):
    return pltpu.CompilerParams(dimension_semantics=sem, vmem_limit_bytes=VMEM_LIMIT_BYTES)


def _rms_mod(x, g, shift, scale):
    y = x * lax.rsqrt(jnp.mean(x * x, axis=-1, keepdims=True) + EPS)
    return (y * g) * (1.0 + scale) + shift


def _dot(a, b):
    return jnp.dot(a, b, preferred_element_type=F32)


def _dot_nt(a, b):
    return lax.dot_general(a, b, (((1,), (1,)), ((), ())), preferred_element_type=F32)


def _mod_spec(m, tm, rows_per_group):
    d = m.shape[-1]
    if m.shape[1] == 1:
        tpg = rows_per_group // tm
        return pl.BlockSpec((1, 1, d), lambda i, *_: (i // tpg, 0, 0))
    return pl.BlockSpec((1, tm, d), lambda i, *_: (i, 0, 0))


def _per_token(m, reps, tm):
    e = jnp.repeat(m, reps, axis=0)
    return e.reshape(e.shape[0] // tm, tm, e.shape[1])


def _ada_kernel(c_ref, w_ref, b_ref, o_ref):
    c = c_ref[...]
    s = (c * (1.0 / (1.0 + jnp.exp(-c)))).astype(BF16)
    o_ref[0] = _dot(s, w_ref[0].astype(BF16)) + b_ref[0]


def _ada_call(c, ada_w, ada_b):
    depth, d, n6 = ada_w.shape
    n = c.shape[0]
    tn = 1536
    return pl.pallas_call(
        _ada_kernel,
        grid=(depth, n6 // tn),
        in_specs=[pl.BlockSpec((n, d), lambda i, j: (0, 0)),
                  pl.BlockSpec((1, d, tn), lambda i, j: (i, 0, j)),
                  pl.BlockSpec((1, 1, tn), lambda i, j: (i, 0, j))],
        out_specs=pl.BlockSpec((1, n, tn), lambda i, j: (i, 0, j)),
        out_shape=jax.ShapeDtypeStruct((depth, n, n6), F32),
        compiler_params=_params("parallel", "parallel"),
        name="adaln",
    )(c, ada_w, ada_b.reshape(depth, 1, n6))


def _mlp_kernel(*refs, mixer_proj):
    if mixer_proj:
        x_ref, a_ref, wm_ref, gt1_ref = refs[:4]
        refs = refs[:1] + refs[4:]
    x_ref, g_ref, sh_ref, sc_ref, gt_ref, w1_ref, w2_ref, o_ref, h_scr, acc_scr = refs[:10]
    j = pl.program_id(1)

    @pl.when(j == 0)
    def _():
        x = x_ref[...]
        if mixer_proj:
            x = x + gt1_ref[0] * _dot(a_ref[...], wm_ref[...])
            refs[10][...] = x
        h_scr[...] = _rms_mod(x, g_ref[...], sh_ref[0], sc_ref[0]).astype(BF16)
        acc_scr[...] = jnp.zeros_like(acc_scr)

    a = _dot(h_scr[...], w1_ref[...])
    a = jnp.square(jnp.maximum(a, 0.0)).astype(BF16)
    acc_scr[...] += _dot(a, w2_ref[...])

    @pl.when(j == pl.num_programs(1) - 1)
    def _():
        x1 = refs[10][...] if mixer_proj else x_ref[...]
        o_ref[...] = x1 + gt_ref[0] * acc_scr[...]


def _mlp_call(x, g, sh, sc, gt, w1, w2, rows_per_group, mixer_proj=None):
    rows, d = x.shape
    f = w1.shape[1]
    tm = min(1024, rows)
    tf = 1024
    ms = _mod_spec(sh, tm, rows_per_group)
    in_specs = [pl.BlockSpec((tm, d), lambda i, j: (i, 0))]
    args = [x]
    scratch = [pltpu.VMEM((tm, d), BF16), pltpu.VMEM((tm, d), F32)]
    if mixer_proj is not None:
        a, w_mix, gt1 = mixer_proj
        in_specs += [pl.BlockSpec((tm, a.shape[1]), lambda i, j: (i, 0)),
                     pl.BlockSpec(w_mix.shape, lambda i, j: (0, 0)),
                     ms]
        args += [a, w_mix, gt1]
        scratch.append(pltpu.VMEM((tm, d), F32))
    in_specs += [pl.BlockSpec((1, d), lambda i, j: (0, 0)),
                 ms, ms, ms,
                 pl.BlockSpec((d, tf), lambda i, j: (0, j)),
                 pl.BlockSpec((tf, d), lambda i, j: (j, 0))]
    return pl.pallas_call(
        functools.partial(_mlp_kernel, mixer_proj=mixer_proj is not None),
        grid=(rows // tm, f // tf),
        in_specs=in_specs,
        out_specs=pl.BlockSpec((tm, d), lambda i, j: (i, 0)),
        out_shape=jax.ShapeDtypeStruct((rows, d), F32),
        scratch_shapes=scratch,
        compiler_params=_params("parallel", "arbitrary"),
        name="mlp",
    )(*args, g, sh, sc, gt, w1, w2)


def _conv_kernel(*refs, tm, d, tiles_per_seq, has_override):
    if has_override:
        x_ref, g_ref, sh_ref, sc_ref, w_ref, k_ref, ov_ref, bv_ref, cu_ref, e_scr = refs
    else:
        x_ref, g_ref, sh_ref, sc_ref, w_ref, k_ref, bv_ref, cu_ref, e_scr = refs
    i = pl.program_id(0)
    if has_override:
        e_scr[0:8, :] = jnp.zeros((8, d), F32)
    else:
        @pl.when(i % tiles_per_seq == 0)
        def _():
            e_scr[0:8, :] = jnp.zeros((8, d), F32)

        @pl.when(i % tiles_per_seq != 0)
        def _():
            e_scr[0:8, :] = e_scr[tm:tm + 8, :]
    n_part = 2 if tm % 256 == 0 else 1
    hr = tm // n_part
    gates = []
    for part in range(n_part):
        rs = slice(part * hr, (part + 1) * hr)
        sh, sc = sh_ref[0], sc_ref[0]
        sh = sh if sh.shape[0] == 1 else sh[rs]
        sc = sc if sc.shape[0] == 1 else sc[rs]
        proj = _dot(_rms_mod(x_ref[rs, :], g_ref[...], sh, sc).astype(BF16), w_ref[...])
        cu = proj[:, d:2 * d] * proj[:, 2 * d:]
        if has_override:
            row = lax.broadcasted_iota(jnp.int32, (hr, 1), 0)
            cu = jnp.where((row % CONV_GRP) >= CONV_GRP // 2, cu, ov_ref[rs, :])
        e_scr[8 + part * hr:8 + (part + 1) * hr, :] = cu
        cu_ref[rs, :] = cu
        gates.append(proj[:, :d])
    k = k_ref[...]
    for part in range(n_part):
        lo = part * hr
        v = (k[0:1, :] * e_scr[6 + lo:6 + lo + hr, :] + k[1:2, :] * e_scr[7 + lo:7 + lo + hr, :]
             + k[2:3, :] * e_scr[8 + lo:8 + lo + hr, :])
        bv_ref[lo:lo + hr, :] = (gates[part] * v).astype(BF16)


def _conv_call(x, g, sh, sc, w_in, k, override, rows_per_group):
    rows, d = x.shape
    tm = min(512, rows)
    has_override = override is not None
    ms = _mod_spec(sh, tm, rows_per_group)
    in_specs = [pl.BlockSpec((tm, d), lambda i: (i, 0)),
                pl.BlockSpec((1, d), lambda i: (0, 0)),
                ms, ms,
                pl.BlockSpec((d, 3 * d), lambda i: (0, 0)),
                pl.BlockSpec((CONV_WIDTH, d), lambda i: (0, 0))]
    args = [x, g, sh, sc, w_in, k]
    if has_override:
        in_specs.append(pl.BlockSpec((tm, d), lambda i: (i, 0)))
        args.append(override)
    kern = functools.partial(_conv_kernel, tm=tm, d=d, tiles_per_seq=max(rows_per_group // tm, 1),
                             has_override=has_override)
    return pl.pallas_call(
        kern,
        grid=(rows // tm,),
        in_specs=in_specs,
        out_specs=[pl.BlockSpec((tm, d), lambda i: (i, 0)), pl.BlockSpec((tm, d), lambda i: (i, 0))],
        out_shape=[jax.ShapeDtypeStruct((rows, d), BF16), jax.ShapeDtypeStruct((rows, d), F32)],
        scratch_shapes=[pltpu.VMEM((tm + 8, d), F32)],
        compiler_params=_params("arbitrary"),
        name="conv_in",
    )(*args)


def _pool_kernel(*refs, tm, d, tiles_per_seq, sample, pos0):
    x_ref, g_ref, sh_ref, sc_ref, gt_ref, w_ref, ps_ref, o_ref, st_ref, e_scr = refs
    i = pl.program_id(0)
    x = x_ref[...]
    h = _rms_mod(x, g_ref[...], sh_ref[0], sc_ref[0])
    row = lax.broadcasted_iota(jnp.int32, (tm, 1), 0)
    if sample:
        local = (row % POOL_GRP) - POOL_GRP // 2
        h = jnp.where(local >= 0, h, x)
        e_scr[0:16, :] = jnp.zeros((16, d), F32)
    else:
        local = (i % tiles_per_seq) * tm + row

        @pl.when(i % tiles_per_seq == 0)
        def _():
            e_scr[0:16, :] = jnp.zeros((16, d), F32)

        @pl.when(i % tiles_per_seq != 0)
        def _():
            e_scr[0:16, :] = e_scr[tm:tm + 16, :]
    e_scr[16:tm + 16, :] = h
    if sample:
        st_ref[...] = h
    else:
        st_ref[0] = e_scr[tm:tm + 16, :]
    gq = d // len(POOL_WINDOWS)
    for gi, w in enumerate(POOL_WINDOWS):
        lo = gi * gq
        win = h[:, lo:lo + gq]
        for j in range(1, w):
            win = win + e_scr[16 - j:16 - j + tm, lo:lo + gq]
        cnt = jnp.clip(pos0 + local + 1, 1, w).astype(F32)
        dd = (win / cnt - h[:, lo:lo + gq]).astype(BF16)
        y = _dot(dd, w_ref[gi]) * ps_ref[:, lo:lo + gq]
        o_ref[:, lo:lo + gq] = x[:, lo:lo + gq] + gt_ref[0][:, lo:lo + gq] * y


def _pool_call(x, g, sh, sc, gt, w, ps, rows_per_group, sample, pos0):
    rows, d = x.shape
    tm = min(512, rows)
    ms = _mod_spec(sh, tm, rows_per_group)
    gq = d // len(POOL_WINDOWS)
    if sample:
        st_spec = pl.BlockSpec((tm, d), lambda i: (i, 0))
        st_shape = jax.ShapeDtypeStruct((rows, d), F32)
    else:
        st_spec = pl.BlockSpec((1, 16, d), lambda i: (i, 0, 0))
        st_shape = jax.ShapeDtypeStruct((rows // tm, 16, d), F32)
    kern = functools.partial(_pool_kernel, tm=tm, d=d, tiles_per_seq=max(rows_per_group // tm, 1),
                             sample=sample, pos0=pos0)
    return pl.pallas_call(
        kern,
        grid=(rows // tm,),
        in_specs=[pl.BlockSpec((tm, d), lambda i: (i, 0)),
                  pl.BlockSpec((1, d), lambda i: (0, 0)),
                  ms, ms, ms,
                  pl.BlockSpec((len(POOL_WINDOWS), gq, gq), lambda i: (0, 0, 0)),
                  pl.BlockSpec((1, d), lambda i: (0, 0))],
        out_specs=[pl.BlockSpec((tm, d), lambda i: (i, 0)), st_spec],
        out_shape=[jax.ShapeDtypeStruct((rows, d), F32), st_shape],
        scratch_shapes=[pltpu.VMEM((tm + 16, d), F32)],
        compiler_params=_params("arbitrary"),
        name="pool",
    )(x, g, sh, sc, gt, w, ps)


def _swa_qkv_kernel(*refs, tm, dil, keeps, tps):
    x_ref, g_ref, sh_ref, sc_ref, w_ref, gn_ref, seg_ref = refs[:7]
    outs = refs[7:16]
    states = refs[16:16 + len(keeps)]
    z_scr, h_scr = refs[16 + len(keeps):]
    w = SWA_WIDTH
    tile = pl.program_id(0) % tps
    h_scr[...] = _rms_mod(x_ref[...], g_ref[...], sh_ref[0], sc_ref[0]).astype(BF16)
    for jb in range(9):
        which, g = divmod(jb, 3)
        z = _dot(h_scr[...], w_ref[:, jb * w:(jb + 1) * w])
        if which < 2:
            ms = _dot((z * z).astype(BF16), seg_ref[...])
            z = z * lax.rsqrt(ms + EPS) * gn_ref[jb:jb + 1, :]
        if keeps and which > 0:
            keep = keeps[g]
            zk = z if keep >= tm else z[tm - keep:, :]

            def put(st_ref=states[g], kv=which - 1, zk=zk):
                st_ref[0, kv] = zk.T

            first_kept = tps - max(keep // tm, 1)
            if first_kept == 0:
                put()
            else:
                pl.when(tile >= first_kept)(put)
        r = dil[g]
        if r == 1:
            outs[jb][0, 0] = z
        else:
            for c in range(w // 128):
                z_scr[c] = z[:, c * 128:(c + 1) * 128]
            for rho in range(r):
                for c in range(w // 128):
                    outs[jb][0, rho, :, c * 128:(c + 1) * 128] = z_scr[c, pl.ds(rho, tm // r, stride=r), :]


def _swa_qkv_call(x, g, sh, sc, w, gn, seg, n_seq, seq_len, dil, rows_per_group, keeps=()):
    rows, d = x.shape
    wd = SWA_WIDTH
    tm = min(512, seq_len)
    tps = seq_len // tm
    ms = _mod_spec(sh, tm, rows_per_group)
    out_specs, out_shape = [], []
    for jb in range(9):
        r = dil[jb % 3]
        out_specs.append(pl.BlockSpec((1, r, tm // r, wd), lambda i: (i // tps, 0, i % tps, 0)))
        out_shape.append(jax.ShapeDtypeStruct((n_seq, r, seq_len // r, wd), F32))
    for keep in keeps:
        assert keep % tm == 0 or tm % keep == 0
        lanes = min(keep, tm)
        first_kept = tps - max(keep // tm, 1)
        out_specs.append(pl.BlockSpec(
            (1, 2, wd, lanes),
            lambda i, first_kept=first_kept: (i // tps, 0, 0, jnp.maximum(i % tps - first_kept, 0))))
        out_shape.append(jax.ShapeDtypeStruct((n_seq, 2, wd, keep), F32))
    return pl.pallas_call(
        functools.partial(_swa_qkv_kernel, tm=tm, dil=dil, keeps=tuple(keeps), tps=tps),
        grid=(rows // tm,),
        in_specs=[pl.BlockSpec((tm, d), lambda i: (i, 0)),
                  pl.BlockSpec((1, d), lambda i: (0, 0)),
                  ms, ms,
                  pl.BlockSpec(w.shape, lambda i: (0, 0), pipeline_mode=pl.Buffered(1)),
                  pl.BlockSpec(gn.shape, lambda i: (0, 0)),
                  pl.BlockSpec(seg.shape, lambda i: (0, 0))],
        out_specs=out_specs,
        out_shape=out_shape,
        scratch_shapes=[pltpu.VMEM((wd // 128, tm, 128), F32), pltpu.VMEM((tm, d), BF16)],
        compiler_params=_params("arbitrary"),
        name="swa_qkv",
    )(x, g, sh, sc, w, gn, seg)


def _swa_band_kernel(q_ref, kc_ref, kp_ref, vc_ref, vp_ref, o_ref, l_ref):
    b = pl.program_id(2)
    blk = SWA_BLK
    hd = SWA_HEAD_DIM
    key = lax.broadcasted_iota(jnp.int32, (2 * blk, blk), 0)
    qi = lax.broadcasted_iota(jnp.int32, (2 * blk, blk), 1)
    dist = qi + blk - key
    ok = jnp.logical_and(dist >= 0, dist <= jnp.where(b > 0, blk, qi))
    lane_head = lax.broadcasted_iota(jnp.int32, (blk, 2 * hd), 1) // hd
    for i in range(q_ref.shape[0]):
        q = q_ref[i, 0]
        k = jnp.concatenate([kp_ref[i, 0], kc_ref[i, 0]], axis=0).astype(BF16)
        vt = jnp.concatenate([vp_ref[i, 0], vc_ref[i, 0]], axis=0).T.astype(BF16)
        ots, lts = [], []
        for h in range(SWA_HEADS):
            slab = slice((h // 2) * 2 * hd, (h // 2 + 1) * 2 * hd)
            qm = jnp.where(lane_head == h % 2, q[:, slab], 0.0).astype(BF16)
            st = jnp.where(ok, _dot_nt(k[:, slab], qm) * SWA_SCALE, NEG)
            m = jnp.max(st, axis=0, keepdims=True)
            p = jnp.exp(st - m)
            l = jnp.sum(p, axis=0, keepdims=True)
            ots.append(_dot(vt[h * hd:(h + 1) * hd, :], p.astype(BF16)) * (1.0 / l))
            lts.append(jnp.broadcast_to(m + jnp.log(l), (hd, blk)))
        o_ref[i] = jnp.concatenate(ots, axis=0).T.astype(o_ref.dtype)
        l_ref[i] = jnp.concatenate(lts, axis=0).T


def _swa_band_call(q, k, v):
    n_seq, r, lr, w = q.shape
    nb = lr // SWA_BLK
    ns = next(c for c in (8, 4, 2, 1) if n_seq % c == 0)
    blk = (ns, 1, SWA_BLK, w)
    cur = pl.BlockSpec(blk, lambda n, rho, b: (n, rho, b, 0))
    prev = pl.BlockSpec(blk, lambda n, rho, b: (n, rho, jnp.maximum(b - 1, 0), 0))
    out_spec = pl.BlockSpec((ns, SWA_BLK, w), lambda n, rho, b: (n, b, rho))
    o, l = pl.pallas_call(
        _swa_band_kernel,
        grid=(n_seq // ns, r, nb),
        in_specs=[cur, cur, prev, cur, prev],
        out_specs=[out_spec, out_spec],
        out_shape=[jax.ShapeDtypeStruct((n_seq, lr, r * w), BF16), jax.ShapeDtypeStruct((n_seq, lr, r * w), F32)],
        compiler_params=_params("parallel", "parallel", "arbitrary"),
        name="swa_band",
    )(q, k, k, v, v)
    return o, l


def _swa_merge_kernel(*refs, dil, tm):
    o_refs, l_refs = refs[0:3], refs[3:6]
    w_ref, x_ref, gt_ref, out_ref = refs[6:10]
    scrs = list(refs[10:])
    wd = SWA_WIDTH

    def token_order(ref, r):
        if r == 1:
            return ref[0].astype(F32)
        scr = scrs.pop()
        for rho in range(r):
            for c in range(wd // 128):
                lo = rho * wd + c * 128
                scr[c, pl.ds(rho, tm // r, stride=r), :] = ref[0, :, lo:lo + 128].astype(F32)
        return jnp.concatenate([scr[c] for c in range(wd // 128)], axis=1)

    a = [token_order(l_refs[g], dil[g]) for g in range(3)]
    o = [token_order(o_refs[g], dil[g]) for g in range(3)]
    m = jnp.maximum(jnp.maximum(a[0], a[1]), a[2])
    e0, e1, e2 = jnp.exp(a[0] - m), jnp.exp(a[1] - m), jnp.exp(a[2] - m)
    inv = 1.0 / (e0 + e1 + e2)
    merged = (e0 * inv) * o[0] + (e1 * inv) * o[1] + (e2 * inv) * o[2]
    out_ref[...] = x_ref[...] + gt_ref[0] * _dot(merged.astype(BF16), w_ref[...])


def _swa_merge_call(os_, ls_, dil, w_o, x, gt, seq_len, rows_per_group):
    rows, d = x.shape
    tm = min(512, seq_len)
    tps = seq_len // tm
    w = SWA_WIDTH
    specs = [pl.BlockSpec((1, tm // r, r * w), lambda i: (i // tps, i % tps, 0)) for r in dil]
    n_scr = 2 * sum(1 for r in dil if r > 1)
    return pl.pallas_call(
        functools.partial(_swa_merge_kernel, dil=dil, tm=tm),
        grid=(rows // tm,),
        in_specs=specs + specs + [pl.BlockSpec((w, d), lambda i: (0, 0)),
                                  pl.BlockSpec((tm, d), lambda i: (i, 0)),
                                  _mod_spec(gt, tm, rows_per_group)],
        out_specs=pl.BlockSpec((tm, d), lambda i: (i, 0)),
        out_shape=jax.ShapeDtypeStruct((rows, d), F32),
        scratch_shapes=[pltpu.VMEM((w // 128, tm, 128), F32)] * n_scr,
        compiler_params=_params("parallel"),
        name="swa_merge",
    )(*os_, *ls_, w_o, x, gt)


def _swa_step_kernel(q_ref, st_ref, new_ref, ns_ref, o_ref, l_ref, *, win, r, lb, s_new, n_blk):
    w = SWA_WIDTH
    hd = SWA_HEAD_DIM
    nrow = SWA_HEADS * s_new
    pad = 128 - s_new
    rowi = lax.broadcasted_iota(jnp.int32, (nrow, w), 0)
    lanei = lax.broadcasted_iota(jnp.int32, (nrow, w), 1)
    diag = (rowi // s_new) == (lanei // hd)
    s_idx = lax.broadcasted_iota(jnp.int32, (nrow, lb + 128), 0) % s_new
    col = lax.broadcasted_iota(jnp.int32, (nrow, lb + 128), 1)
    dist = lb + s_idx - jnp.where(col < lb, col, col - pad)
    ok = jnp.logical_and(jnp.logical_and(dist >= 0, dist % r == 0), dist <= win)
    ok = jnp.logical_and(ok, jnp.logical_or(col < lb, col >= lb + pad))
    per_slab = 128 // s_new
    for i in range(n_blk):
        if n_blk % per_slab == 0:
            slab, shift = i // per_slab, pad - (i % per_slab) * s_new
        else:
            slab, shift = 0, pad - ((pl.program_id(0) * n_blk + i) % per_slab) * s_new
        new_k = pltpu.roll(new_ref[slab, 0], shift, axis=1)
        new_v = pltpu.roll(new_ref[slab, 1], shift, axis=1)
        q = q_ref[i]
        qbd = jnp.where(diag, jnp.concatenate([q] * SWA_HEADS, axis=0), 0.0).astype(BF16)
        kt = jnp.concatenate([st_ref[i, 0], new_k], axis=1).astype(BF16)
        vt = jnp.concatenate([st_ref[i, 1], new_v], axis=1).astype(BF16)
        sc = jnp.where(ok, _dot(qbd, kt) * SWA_SCALE, NEG)
        m = jnp.max(sc, axis=-1, keepdims=True)
        p = jnp.exp(sc - m)
        l = jnp.sum(p, axis=-1, keepdims=True)
        acc = jnp.where(diag, _dot_nt(p.astype(BF16), vt) / l, 0.0)
        lse = jnp.where(diag, m + jnp.log(l), 0.0)
        o = acc[0:s_new]
        ls = lse[0:s_new]
        for h in range(1, SWA_HEADS):
            o = o + acc[h * s_new:(h + 1) * s_new]
            ls = ls + lse[h * s_new:(h + 1) * s_new]
        o_ref[i] = o
        l_ref[i] = ls
        for kv, new in enumerate((new_k, new_v)):
            ns_ref[i, kv, :, 0:lb - s_new] = st_ref[i, kv, :, s_new:lb]
            ns_ref[i, kv, :, lb - s_new:lb] = new[:, pad:128]


def _swa_step_call(q, st_t, new_t, win, r):
    n_seq, s_new, w = q.shape
    lb = st_t.shape[-1]
    n_blk = max(1, min(n_seq, SWA_GROUPS[-1][0] // lb))
    per_slab = 128 // s_new
    assert n_seq % n_blk == 0 and (n_blk % per_slab == 0 or per_slab % n_blk == 0)
    if n_blk % per_slab == 0:
        new_spec = pl.BlockSpec((n_blk // per_slab, 2, w, 128), lambda n: (n, 0, 0, 0))
    else:
        new_spec = pl.BlockSpec((1, 2, w, 128), lambda n: ((n * n_blk) // per_slab, 0, 0, 0))
    return pl.pallas_call(
        functools.partial(_swa_step_kernel, win=win, r=r, lb=lb, s_new=s_new, n_blk=n_blk),
        grid=(n_seq // n_blk,),
        in_specs=[pl.BlockSpec((n_blk, s_new, w), lambda n: (n, 0, 0)),
                  pl.BlockSpec((n_blk, 2, w, lb), lambda n: (n, 0, 0, 0)),
                  new_spec],
        out_specs=[pl.BlockSpec((n_blk, 2, w, lb), lambda n: (n, 0, 0, 0)),
                   pl.BlockSpec((n_blk, s_new, w), lambda n: (n, 0, 0)),
                   pl.BlockSpec((n_blk, s_new, w), lambda n: (n, 0, 0))],
        out_shape=[jax.ShapeDtypeStruct((n_seq, 2, w, lb), F32),
                   jax.ShapeDtypeStruct((n_seq, s_new, w), F32),
                   jax.ShapeDtypeStruct((n_seq, s_new, w), F32)],
        compiler_params=_params("parallel"),
        name="swa_step",
    )(q, st_t, new_t)


def _rope128(bn, cos_t, sin_t):
    return bn * cos_t + pltpu.roll(bn, 64, axis=1) * sin_t


def _mla_proj_kernel(*refs, decode):
    (x_ref, g_ref, sh_ref, sc_ref, w1_ref, gqa_ref, wqb_ref, gq_ref, gkv_ref, gkr_ref,
     cos_ref, sin_ref, wx_ref) = refs[:13]
    if decode:
        outs = refs[13:]
    else:
        wvt_ref = refs[13]
        outs = refs[14:]
    nh = MLA_HEADS
    if decode:
        lat_ref, qabs_ref, qpe_ref = outs
    else:
        lat_ref, qcat_ref, kcat_ref, vt_ref = outs
    gq = gq_ref[...]
    tm = x_ref.shape[0]
    n_part = 2 if tm % 256 == 0 else 1

    def rows_of(ref, rs):
        v = ref[0]
        return v if v.shape[0] == 1 else v[rs]

    for part in range(n_part):
        rs = slice(part * (tm // n_part), (part + 1) * (tm // n_part))
        h = _rms_mod(x_ref[rs, :], g_ref[...], rows_of(sh_ref, rs), rows_of(sc_ref, rs)).astype(BF16)
        a = _dot(h, w1_ref[...])
        qa = a[:, :Q_LORA]
        qa = (qa * lax.rsqrt(jnp.mean(qa * qa, axis=-1, keepdims=True) + EPS) * gqa_ref[...]).astype(BF16)
        q = _dot(qa, wqb_ref[...])
        cos_t = cos_ref[rs, :]
        sin_t = sin_ref[rs, :]
        latc = a[:, Q_LORA:Q_LORA + KV_LORA]
        latc = latc * lax.rsqrt(jnp.mean(latc * latc, axis=-1, keepdims=True) + EPS) * gkv_ref[...]
        kb = a[:, Q_LORA + KV_LORA:]
        kb = kb * lax.rsqrt(jnp.sum(kb * kb, axis=-1, keepdims=True) * (1.0 / QK_ROPE) + EPS) * gkr_ref[...]
        kb = _rope128(kb, cos_t, sin_t)
        lane = lax.broadcasted_iota(jnp.int32, kb.shape, 1)
        kstd = jnp.where(lane < 32, kb, pltpu.roll(kb, 96, axis=1))
        lat_ref[rs, 0:KV_LORA] = latc
        lat_ref[rs, KV_LORA:KV_LORA + QK_ROPE] = kstd[:, 0:QK_ROPE]
        for hh in range(nh):
            an = q[:, hh * 256:hh * 256 + 128]
            an = an * lax.rsqrt(jnp.mean(an * an, axis=-1, keepdims=True) + EPS) * gq[:, 0:128]
            bn = q[:, hh * 256 + 128:hh * 256 + 256]
            bn = bn * lax.rsqrt(jnp.sum(bn * bn, axis=-1, keepdims=True) * (1.0 / QK_ROPE) + EPS) * gq[:, 128:256]
            bn = _rope128(bn, cos_t, sin_t)
            if decode:
                ag = an * gq[:, 256:384]
                a_hi = ag.astype(BF16)
                a_lo = (ag - a_hi.astype(F32)).astype(BF16)
                qabs_ref[rs, hh * 256:(hh + 1) * 256] = (_dot(a_hi, wx_ref[hh]) + _dot(a_lo, wx_ref[hh])).astype(BF16)
                qpe_ref[rs, hh * 128:(hh + 1) * 128] = jnp.where(lane < 32, bn, pltpu.roll(bn, 96, axis=1)).astype(BF16)
            else:
                qcat_ref[rs, hh * 256:hh * 256 + 128] = an.astype(BF16)
                qcat_ref[rs, hh * 256 + 128:(hh + 1) * 256] = bn.astype(BF16)
        if not decode:
            gkn = gq_ref[:, 256:384]
            latb = latc.astype(BF16)
            kvx = _dot(latb, wx_ref[...])
            kbb = kb.astype(BF16)
            for hh in range(nh):
                kn = kvx[:, hh * 128:(hh + 1) * 128]
                kn = kn * lax.rsqrt(jnp.mean(kn * kn, axis=-1, keepdims=True) + EPS) * gkn
                kcat_ref[rs, hh * 256:hh * 256 + 128] = kn.astype(BF16)
                kcat_ref[rs, hh * 256 + 128:(hh + 1) * 256] = kbb
            vt_ref[0, :, rs] = _dot_nt(wvt_ref[...], latb).astype(BF16)


def _mla_proj_call(x, g, sh, sc, wts, cos_t, sin_t, rows_per_group, decode):
    rows, d = x.shape
    tm = min(512, rows)
    nh = MLA_HEADS
    ms = _mod_spec(sh, tm, rows_per_group)
    tps = max(rows_per_group // tm, 1)
    if decode:
        tab_spec = pl.BlockSpec((tm, 128), lambda i: (0, 0))
        wx = wts["wk_t"]
        wx_spec = pl.BlockSpec(wx.shape, lambda i: (0, 0, 0))
        out_specs = [pl.BlockSpec((tm, KV_LORA + QK_ROPE), lambda i: (i, 0)),
                     pl.BlockSpec((tm, nh * 256), lambda i: (i, 0)),
                     pl.BlockSpec((tm, nh * 128), lambda i: (i, 0))]
        out_shape = [jax.ShapeDtypeStruct((rows, KV_LORA + QK_ROPE), F32),
                     jax.ShapeDtypeStruct((rows, nh * 256), BF16),
                     jax.ShapeDtypeStruct((rows, nh * 128), BF16)]
    else:
        tab_spec = pl.BlockSpec((tm, 128), lambda i: (i % tps, 0))
        wx = wts["w_kn"]
        wx_spec = pl.BlockSpec(wx.shape, lambda i: (0, 0))
        out_specs = [pl.BlockSpec((tm, KV_LORA + QK_ROPE), lambda i: (i, 0)),
                     pl.BlockSpec((tm, nh * 256), lambda i: (i, 0)),
                     pl.BlockSpec((tm, nh * 256), lambda i: (i, 0)),
                     pl.BlockSpec((1, nh * V_DIM, tm), lambda i: (i // tps, 0, i % tps))]
        out_shape = [jax.ShapeDtypeStruct((rows, KV_LORA + QK_ROPE), F32),
                     jax.ShapeDtypeStruct((rows, nh * 256), BF16),
                     jax.ShapeDtypeStruct((rows, nh * 256), BF16),
                     jax.ShapeDtypeStruct((rows // rows_per_group, nh * V_DIM, rows_per_group), BF16)]

    def full(a):
        return pl.BlockSpec(a.shape, lambda i: (0,) * a.ndim)

    extra = [] if decode else [wts["wv_t"]]
    return pl.pallas_call(
        functools.partial(_mla_proj_kernel, decode=decode),
        grid=(rows // tm,),
        in_specs=[pl.BlockSpec((tm, d), lambda i: (i, 0)),
                  pl.BlockSpec((1, d), lambda i: (0, 0)),
                  ms, ms,
                  full(wts["w1"]), full(wts["g_qa"]), full(wts["w_qb"]), full(wts["gq"]),
                  full(wts["g_kva"]), full(wts["g_kr"]),
                  tab_spec, tab_spec, wx_spec] + [full(a) for a in extra],
        out_specs=out_specs,
        out_shape=out_shape,
        compiler_params=_params("parallel"),
        name="mla_proj_decode" if decode else "mla_proj",
    )(x, g, sh, sc, wts["w1"], wts["g_qa"], wts["w_qb"], wts["gq"], wts["g_kva"], wts["g_kr"],
      cos_t, sin_t, wx, *extra)


def _mla_causal_kernel(q_ref, k_ref, vt_ref, o_ref, *, seq_len, tq):
    r = lax.broadcasted_iota(jnp.int32, (tq, tq), 0)
    c = lax.broadcasted_iota(jnp.int32, (tq, tq), 1)
    ex = MLA_SCALE * LOG2_E
    for qb in range(seq_len // tq):
        lo = qb * tq
        q = q_ref[0, lo:lo + tq, :]
        s_d = jnp.where(r <= c, _dot_nt(k_ref[0, lo:lo + tq, :], q), NEG)
        m = jnp.max(s_d, axis=0, keepdims=True)
        if qb > 0:
            s_p = _dot_nt(k_ref[0, 0:lo, :], q)
            m = jnp.maximum(m, jnp.max(s_p, axis=0, keepdims=True))
        p_d = jnp.exp2((s_d - m) * ex)
        l = jnp.sum(p_d, axis=0, keepdims=True)
        ot = _dot(vt_ref[0, :, lo:lo + tq], p_d.astype(BF16))
        if qb > 0:
            p_p = jnp.exp2((s_p - m) * ex)
            l = l + jnp.sum(p_p, axis=0, keepdims=True)
            ot = ot + _dot(vt_ref[0, :, 0:lo], p_p.astype(BF16))
        o_ref[0, lo:lo + tq, :] = (ot * (1.0 / l)).T.astype(BF16)


def _mla_causal_call(qcat, kcat, vt, n_seq, seq_len):
    nh = MLA_HEADS
    tq = min(512, seq_len)
    q3 = qcat.reshape(n_seq, seq_len, nh * 256)
    k3 = kcat.reshape(n_seq, seq_len, nh * 256)
    o = pl.pallas_call(
        functools.partial(_mla_causal_kernel, seq_len=seq_len, tq=tq),
        grid=(n_seq, nh),
        in_specs=[pl.BlockSpec((1, seq_len, 256), lambda n, h: (n, 0, h)),
                  pl.BlockSpec((1, seq_len, 256), lambda n, h: (n, 0, h)),
                  pl.BlockSpec((1, V_DIM, seq_len), lambda n, h: (n, h, 0))],
        out_specs=pl.BlockSpec((1, seq_len, V_DIM), lambda n, h: (n, 0, h)),
        out_shape=jax.ShapeDtypeStruct((n_seq, seq_len, nh * V_DIM), BF16),
        compiler_params=_params("parallel", "parallel"),
        name="mla_causal",
    )(q3, k3, vt)
    return o.reshape(n_seq * seq_len, nh * V_DIM)


def _mla_decode_kernel(pt_ref, *refs, n_pg, s_new):
    del pt_ref
    page_refs = refs[:n_pg]
    (qabs_ref, qpe_ref, latn_ref, wkt_ref, wv_ref, o_ref, m_scr, l_scr, ctx_scr) = refs[n_pg:]
    c = pl.program_id(1)
    nh = MLA_HEADS
    nrow = nh * s_new

    @pl.when(c == 0)
    def _():
        m_scr[...] = jnp.full_like(m_scr, NEG)
        l_scr[...] = jnp.zeros_like(l_scr)
        ctx_scr[...] = jnp.zeros_like(ctx_scr)

    def process(pages, mask):
        latc = jnp.concatenate([pg[0:KV_LORA, :].astype(BF16) for pg in pages], axis=1)
        kpe = jnp.concatenate([pg[KV_LORA:KV_LORA + QK_ROPE, :].astype(BF16) for pg in pages], axis=1)
        r = _dot(wkt_ref[...], latc)
        rq = _dot(qabs_ref[0], latc)
        spe = _dot(qpe_ref[0][:, 0:QK_ROPE], kpe)
        parts = []
        for h in range(nh):
            blk = r[h * QK_NOPE:(h + 1) * QK_NOPE]
            rstd = lax.rsqrt(jnp.sum(blk * blk, axis=0, keepdims=True) * (1.0 / QK_NOPE) + EPS)
            parts.append(rq[h * s_new:(h + 1) * s_new] * rstd)
        s = (jnp.concatenate(parts, axis=0) + spe) * MLA_SCALE
        if mask is not None:
            s = jnp.where(mask, s, NEG)
        m_old = m_scr[...]
        m_new = jnp.maximum(m_old, jnp.max(s, axis=-1, keepdims=True))
        alpha = jnp.exp(m_old - m_new)
        p = jnp.exp(s - m_new)
        l_scr[...] = alpha * l_scr[...] + jnp.sum(p, axis=-1, keepdims=True)
        ctx_scr[...] = alpha * ctx_scr[...] + _dot_nt(p.astype(BF16), latc)
        m_scr[...] = m_new

    for lo in range(0, n_pg, 8):
        process([pr[0, 0] for pr in page_refs[lo:lo + 8]], None)

    @pl.when(c == pl.num_programs(1) - 1)
    def _():
        si = lax.broadcasted_iota(jnp.int32, (nrow, PAGE_SIZE), 0) % s_new
        kj = lax.broadcasted_iota(jnp.int32, (nrow, PAGE_SIZE), 1)
        process([latn_ref[0]], kj <= si)
        ctx = ctx_scr[...] / l_scr[...]
        for h in range(nh):
            ctx_h = ctx[h * s_new:(h + 1) * s_new].astype(BF16)
            o_ref[0, :, h * V_DIM:(h + 1) * V_DIM] = _dot(ctx_h, wv_ref[h]).astype(BF16)


def _mla_decode_call(cache_t, layer, page_table, qabs, qpe, lat_new, wkt, wv, n_seq, s_new):
    nh = MLA_HEADS
    n_pages = page_table.shape[1]
    n_pg = min(32, n_pages)
    lw = KV_LORA + QK_ROPE
    nrow = nh * s_new

    def page_spec(p):
        return pl.BlockSpec((1, 1, lw, PAGE_SIZE),
                            lambda n, c, pt: (layer, pt[n * n_pages + c * n_pg + p], 0, 0))

    latn_t = jnp.transpose(lat_new.reshape(n_seq, s_new, lw), (0, 2, 1))
    latn_t = jnp.pad(latn_t, ((0, 0), (0, 0), (0, PAGE_SIZE - s_new)))
    grid_spec = pltpu.PrefetchScalarGridSpec(
        num_scalar_prefetch=1,
        grid=(n_seq, n_pages // n_pg),
        in_specs=[page_spec(p) for p in range(n_pg)] + [
            pl.BlockSpec((1, nrow, 256), lambda n, c, pt: (n, 0, 0)),
            pl.BlockSpec((1, nrow, 128), lambda n, c, pt: (n, 0, 0)),
            pl.BlockSpec((1, lw, PAGE_SIZE), lambda n, c, pt: (n, 0, 0)),
            pl.BlockSpec(wkt.shape, lambda n, c, pt: (0, 0)),
            pl.BlockSpec(wv.shape, lambda n, c, pt: (0, 0, 0))],
        out_specs=pl.BlockSpec((1, s_new, nh * V_DIM), lambda n, c, pt: (n, 0, 0)),
        scratch_shapes=[pltpu.VMEM((nrow, 1), F32), pltpu.VMEM((nrow, 1), F32),
                        pltpu.VMEM((nrow, KV_LORA), F32)])
    o = pl.pallas_call(
        functools.partial(_mla_decode_kernel, n_pg=n_pg, s_new=s_new),
        grid_spec=grid_spec,
        out_shape=jax.ShapeDtypeStruct((n_seq, s_new, nh * V_DIM), BF16),
        compiler_params=_params("parallel", "arbitrary"),
        name="mla_decode",
    )(page_table.reshape(-1), *([cache_t] * n_pg),
      jnp.transpose(qabs.reshape(n_seq, s_new, nh, 256), (0, 2, 1, 3)).reshape(n_seq, nrow, 256),
      jnp.transpose(qpe.reshape(n_seq, s_new, nh, 128), (0, 2, 1, 3)).reshape(n_seq, nrow, 128),
      latn_t, wkt, wv)
    return o.reshape(n_seq * s_new, nh * V_DIM)


def _mla_weights(w_qa, g_qa, w_qb, w_kva, g_kva, w_kvb, g_qn, g_kn, g_qr, g_kr):
    nh = MLA_HEADS
    half = QK_ROPE // 2

    def pad_rope_cols(w):
        z = jnp.zeros(w.shape[:-1] + (half,), w.dtype)
        return jnp.concatenate([w[..., :half], z, w[..., half:], z], axis=-1)

    w_kva_p = jnp.concatenate([w_kva[:, :KV_LORA], pad_rope_cols(w_kva[:, KV_LORA:])], axis=1)
    w1 = jnp.concatenate([w_qa, w_kva_p], axis=1).astype(BF16)
    wq = w_qb.reshape(Q_LORA, nh, QK_NOPE + QK_ROPE)
    wq = jnp.concatenate([wq[..., :QK_NOPE], pad_rope_cols(wq[..., QK_NOPE:])], axis=-1)
    w_qb_p = wq.reshape(Q_LORA, nh * 256).astype(BF16)
    gq = jnp.concatenate([g_qn, pad_rope_cols(g_qr), g_kn])[None, :]
    wkv = w_kvb.reshape(KV_LORA, nh, QK_NOPE + V_DIM)
    w_kn = wkv[..., :QK_NOPE].reshape(KV_LORA, nh * QK_NOPE).astype(BF16)
    wv_t = jnp.transpose(wkv[..., QK_NOPE:], (1, 2, 0)).reshape(nh * V_DIM, KV_LORA).astype(BF16)
    wk_t = jnp.transpose(wkv[..., :QK_NOPE], (1, 2, 0)).astype(BF16)
    return dict(w1=w1, g_qa=g_qa[None, :], w_qb=w_qb_p, gq=gq, g_kva=g_kva[None, :],
                g_kr=pad_rope_cols(g_kr)[None, :], w_kn=w_kn, wv_t=wv_t, wk_t=wk_t,
                wkt_flat=wk_t.reshape(nh * QK_NOPE, KV_LORA),
                wv=jnp.transpose(wkv[..., QK_NOPE:], (1, 0, 2)).astype(BF16))


def _rope_tables(pos):
    half = QK_ROPE // 2
    inv = ROPE_BASE ** (-jnp.arange(half, dtype=F32) / half)
    ang = pos.astype(F32)[:, None] * inv[None, :]
    cos, sin = jnp.cos(ang), jnp.sin(ang)
    z = jnp.zeros_like(cos)
    return (jnp.concatenate([cos, z, cos, z], axis=1), jnp.concatenate([-sin, z, sin, z], axis=1))


def kernel(x_prompt, x_sample, state_conv, state_swa0, state_swa1, state_swa2, state_pool, cache_mla, page_table,
           c_prompt, c_sample, ada_w, ada_b, norm1_g, norm2_g, mlp_w1, mlp_w2, conv_w_in, conv_k, conv_w_out,
           swa_w_qkv, swa_qn_g, swa_kn_g, swa_w_o, pool_w, pool_scale, mla_w_qa, mla_g_qa, mla_w_qb, mla_w_kva,
           mla_g_kva, mla_w_kvb, mla_g_qn, mla_g_kn, mla_g_qr, mla_g_kr, mla_w_o):
    bp, t, d = x_prompt.shape
    bs, s, _ = x_sample.shape
    depth = ada_w.shape[0]
    past = page_table.shape[1] * PAGE_SIZE
    swa_states = (state_swa0, state_swa1, state_swa2)
    for (win, _), st in zip(SWA_GROUPS, swa_states):
        assert st.shape[2] == win, "sliding-window state must hold a full window"
    assert t % (SWA_GROUPS[-1][1] * SWA_BLK) == 0, "prompt length must split into full dilated blocks"
    rows_p, rows_s = bp * t, bs * s
    xp = x_prompt.reshape(rows_p, d)
    xs = x_sample.reshape(rows_s, d)

    mod = _ada_call(jnp.concatenate([c_prompt, c_sample], axis=0), ada_w, ada_b)

    def mods(i):
        out_p, out_s = [], []
        for j in range(6):
            m = mod[i, :, j * d:(j + 1) * d]
            out_p.append(m[:bp, None, :])
            out_s.append(m[bp:])
        return out_p, out_s

    def tok(m, tm):
        return _per_token(m, s, tm)

    tm_s512 = min(512, rows_s)
    tm_s1024 = min(1024, rows_s)
    conv_p = conv_s = pool_p = pool_s = mla_p = mla_s = None
    swa_p, swa_s = [None] * 3, [None] * 3

    for i in range(depth):
        kind, j = i % 4, i // 4
        (sh1p, sc1p, g1p, sh2p, sc2p, g2p), (sh1s, sc1s, g1s, sh2s, sc2s, g2s) = mods(i)
        n1 = norm1_g[i][None, :]
        proj_p = proj_s = None
        if kind == 0:
            w_in = conv_w_in[j].astype(BF16)
            w_out = conv_w_out[j].astype(BF16)
            bv, cu = _conv_call(xp, n1, sh1p, sc1p, w_in, conv_k[j], None, t)
            conv_p = cu.reshape(bp, t, d)[:, t - (CONV_WIDTH - 1):]
            proj_p = (bv, w_out, g1p)
            grp = CONV_GRP
            pad = jnp.zeros((bs, grp - s, d), F32)
            x_ext = jnp.concatenate([pad, xs.reshape(bs, s, d)], axis=1).reshape(bs * grp, d)
            ov = jnp.concatenate([jnp.zeros((bs, grp - s - 2, d), F32), state_conv[j], jnp.zeros((bs, s, d), F32)],
                                 axis=1).reshape(bs * grp, d)
            tme = min(512, bs * grp)
            she = _per_token(sh1s, grp, tme)
            sce = _per_token(sc1s, grp, tme)
            bv, cu = _conv_call(x_ext, n1, she, sce, w_in, conv_k[j], ov, grp)
            conv_s = cu.reshape(bs, grp, d)[:, grp - (CONV_WIDTH - 1):]
            bv = bv.reshape(bs, grp, d)[:, grp - s:].reshape(rows_s, d)
            proj_s = (bv, w_out, tok(g1s, tm_s1024))
        elif kind == 1:
            w_qkv = swa_w_qkv[j].astype(BF16)
            w_o = swa_w_o[j].astype(BF16)
            gn = jnp.concatenate([jnp.tile(swa_qn_g[j], (1, SWA_HEADS)), jnp.tile(swa_kn_g[j], (1, SWA_HEADS))],
                                 axis=0)
            seg = (jnp.arange(SWA_WIDTH)[:, None] // SWA_HEAD_DIM == jnp.arange(SWA_WIDTH)[None, :] // SWA_HEAD_DIM)
            seg = (seg.astype(F32) / SWA_HEAD_DIM).astype(BF16)
            dil = tuple(r for _, r in SWA_GROUPS)
            keeps = tuple(min(win, t) for win, _ in SWA_GROUPS)
            qkv = _swa_qkv_call(xp, n1, sh1p, sc1p, w_qkv, gn, seg, bp, t, dil, t, keeps)
            os_, ls_ = [], []
            for g, (win, r) in enumerate(SWA_GROUPS):
                o_g, l_g = _swa_band_call(qkv[g], qkv[3 + g], qkv[6 + g])
                os_.append(o_g)
                ls_.append(l_g)
                st = qkv[9 + g].reshape(bp, 2, SWA_HEADS, SWA_HEAD_DIM, keeps[g])
                swa_p[g] = jnp.transpose(st, (0, 4, 1, 2, 3))[None]
            xp = _swa_merge_call(os_, ls_, dil, w_o, xp, g1p, t, t)
            tm_q = min(512, rows_s)
            qkv_s = _swa_qkv_call(xs, n1, tok(sh1s, tm_q), tok(sc1s, tm_q), w_qkv, gn, seg, 1, rows_s,
                                  (1, 1, 1), s)
            os_, ls_ = [], []
            for g, (win, r) in enumerate(SWA_GROUPS):
                q_g, k_g, v_g = (qkv_s[c * 3 + g].reshape(bs, s, SWA_WIDTH) for c in range(3))
                st_t = jnp.transpose(swa_states[g][j], (0, 2, 3, 4, 1)).reshape(bs, 2, SWA_WIDTH, win)
                per_slab = 128 // s
                new_t = jnp.stack([k_g, v_g], axis=1).reshape(bs // per_slab, per_slab, 2, s, SWA_WIDTH)
                new_t = jnp.transpose(new_t, (0, 2, 4, 1, 3)).reshape(bs // per_slab, 2, SWA_WIDTH, 128)
                ns, o_g, l_g = _swa_step_call(q_g, st_t, new_t, win, r)
                swa_s[g] = jnp.transpose(ns.reshape(bs, 2, SWA_HEADS, SWA_HEAD_DIM, win), (0, 4, 1, 2, 3))[None]
                os_.append(o_g.reshape(1, rows_s, SWA_WIDTH))
                ls_.append(l_g.reshape(1, rows_s, SWA_WIDTH))
            xs = _swa_merge_call(os_, ls_, (1, 1, 1), w_o, xs, tok(g1s, tm_s512), rows_s, s)
        elif kind == 2:
            pw = pool_w[j].astype(BF16)
            ps = pool_scale[j][None, :]
            xp, tail = _pool_call(xp, n1, sh1p, sc1p, g1p, pw, ps, t, False, 0)
            tpt = t // min(512, rows_p)
            pool_p = tail.reshape(bp, tpt, 16, d)[:, -1, 1:]
            grp = POOL_GRP
            x_ext = jnp.concatenate([jnp.zeros((bs, 1, d), F32), state_pool[j], xs.reshape(bs, s, d),
                                     jnp.zeros((bs, grp - 16 - s, d), F32)], axis=1).reshape(bs * grp, d)
            tme = min(512, bs * grp)
            y_ext, ext = _pool_call(x_ext, n1, _per_token(sh1s, grp, tme), _per_token(sc1s, grp, tme),
                                    _per_token(g1s, grp, tme), pw, ps, grp, True, past)
            xs = y_ext.reshape(bs, grp, d)[:, 16:16 + s].reshape(rows_s, d)
            pool_s = ext.reshape(bs, grp, d)[:, 16 + s - POOL_HIST:16 + s]
        else:
            wts = _mla_weights(mla_w_qa[j], mla_g_qa[j], mla_w_qb[j], mla_w_kva[j], mla_g_kva[j], mla_w_kvb[j],
                               mla_g_qn[j], mla_g_kn[j], mla_g_qr[j], mla_g_kr[j])
            w_o = mla_w_o[j].astype(BF16)
            cos_p, sin_p = _rope_tables(jnp.arange(t))
            lat_p, qcat, kcat, vt = _mla_proj_call(xp, n1, sh1p, sc1p, wts, cos_p, sin_p, t, False)
            o_p = _mla_causal_call(qcat, kcat, vt, bp, t)
            proj_p = (o_p, w_o, g1p)
            mla_p = lat_p.reshape(bp, t, KV_LORA + QK_ROPE)
            cos_s, sin_s = _rope_tables(past + jnp.arange(s))
            reps = tm_s512 // s
            cos_s, sin_s = jnp.tile(cos_s, (reps, 1)), jnp.tile(sin_s, (reps, 1))
            lat_s, qabs, qpe = _mla_proj_call(xs, n1, tok(sh1s, tm_s512), tok(sc1s, tm_s512), wts, cos_s, sin_s, s, True)
            cache_t = jnp.transpose(cache_mla, (0, 1, 3, 2))
            o_s = _mla_decode_call(cache_t, j, page_table, qabs, qpe, lat_s, wts["wkt_flat"], wts["wv"], bs, s)
            proj_s = (o_s, w_o, tok(g1s, tm_s1024))
            mla_s = lat_s.reshape(bs, s, KV_LORA + QK_ROPE)
        n2 = norm2_g[i][None, :]
        w1 = mlp_w1[i].astype(BF16)
        w2 = mlp_w2[i].astype(BF16)
        xp = _mlp_call(xp, n2, sh2p, sc2p, g2p, w1, w2, t, proj_p)
        xs = _mlp_call(xs, n2, tok(sh2s, tm_s1024), tok(sc2s, tm_s1024), tok(g2s, tm_s1024), w1, w2, s, proj_s)

    return (xp.reshape(bp, t, d), xs.reshape(bs, s, d), conv_p[None], conv_s[None],
            swa_p[0], swa_s[0], swa_p[1], swa_s[1], swa_p[2], swa_s[2],
            pool_p[None], pool_s[None], mla_p[None], mla_s[None])
```

```python
import functools

import jax
import jax.numpy as jnp
from jax import lax
from jax.experimental import pallas as pl
from jax.experimental.pallas import tpu as pltpu

F32 = jnp.float32
BF16 = jnp.bfloat16

EPS = 1e-6
NEG = -1e30

CONV_WIDTH = 3
SWA_GROUPS = ((128, 1), (512, 4), (2048, 16))
SWA_HEADS = 8
SWA_HEAD_DIM = 64
SWA_WIDTH = SWA_HEADS * SWA_HEAD_DIM
SWA_BLK = 128
SWA_SCALE = SWA_HEAD_DIM ** -0.5
POOL_WINDOWS = (2, 4, 8, 16)
POOL_HIST = 15
MLA_HEADS = 8
Q_LORA = 384
KV_LORA = 256
QK_NOPE = 128
QK_ROPE = 64
V_DIM = 128
MLA_SCALE = (QK_NOPE + QK_ROPE) ** -0.5
LOG2_E = 1.4426950408889634
ROPE_BASE = 10000.0
PAGE_SIZE = 128

VMEM_LIMIT_BYTES = 56 * 1024 * 1024
CONV_GRP = 16
POOL_GRP = 32


def _params(*sem):
    return pltpu.CompilerParams(dimension_semantics=sem, vmem_limit_bytes=VMEM_LIMIT_BYTES)


def _rms_mod(x, g, shift, scale):
    y = x * lax.rsqrt(jnp.mean(x * x, axis=-1, keepdims=True) + EPS)
    return (y * g) * (1.0 + scale) + shift


def _dot(a, b):
    return jnp.dot(a, b, preferred_element_type=F32)


def _dot_nt(a, b):
    return lax.dot_general(a, b, (((1,), (1,)), ((), ())), preferred_element_type=F32)


def _mod_spec(m, tm, rows_per_group):
    d = m.shape[-1]
    if m.shape[1] == 1:
        tpg = rows_per_group // tm
        return pl.BlockSpec((1, 1, d), lambda i, *_: (i // tpg, 0, 0))
    return pl.BlockSpec((1, tm, d), lambda i, *_: (i, 0, 0))


def _per_token(m, reps, tm):
    e = jnp.repeat(m, reps, axis=0)
    return e.reshape(e.shape[0] // tm, tm, e.shape[1])


def _ada_kernel(c_ref, w_ref, b_ref, o_ref):
    c = c_ref[...]
    s = (c * (1.0 / (1.0 + jnp.exp(-c)))).astype(BF16)
    o_ref[0] = _dot(s, w_ref[0].astype(BF16)) + b_ref[0]


def _ada_call(c, ada_w, ada_b):
    depth, d, n6 = ada_w.shape
    n = c.shape[0]
    tn = 1536
    return pl.pallas_call(
        _ada_kernel,
        grid=(depth, n6 // tn),
        in_specs=[pl.BlockSpec((n, d), lambda i, j: (0, 0)),
                  pl.BlockSpec((1, d, tn), lambda i, j: (i, 0, j)),
                  pl.BlockSpec((1, 1, tn), lambda i, j: (i, 0, j))],
        out_specs=pl.BlockSpec((1, n, tn), lambda i, j: (i, 0, j)),
        out_shape=jax.ShapeDtypeStruct((depth, n, n6), F32),
        compiler_params=_params("parallel", "parallel"),
        name="adaln",
    )(c, ada_w, ada_b.reshape(depth, 1, n6))


def _mlp_kernel(*refs, mixer_proj):
    if mixer_proj:
        x_ref, a_ref, wm_ref, gt1_ref = refs[:4]
        refs = refs[:1] + refs[4:]
    x_ref, g_ref, sh_ref, sc_ref, gt_ref, w1_ref, w2_ref, o_ref, h_scr, acc_scr = refs[:10]
    j = pl.program_id(1)

    @pl.when(j == 0)
    def _():
        x = x_ref[...]
        if mixer_proj:
            x = x + gt1_ref[0] * _dot(a_ref[...], wm_ref[...])
            refs[10][...] = x
        h_scr[...] = _rms_mod(x, g_ref[...], sh_ref[0], sc_ref[0]).astype(BF16)
        acc_scr[...] = jnp.zeros_like(acc_scr)

    a = _dot(h_scr[...], w1_ref[...])
    a = jnp.square(jnp.maximum(a, 0.0)).astype(BF16)
    acc_scr[...] += _dot(a, w2_ref[...])

    @pl.when(j == pl.num_programs(1) - 1)
    def _():
        x1 = refs[10][...] if mixer_proj else x_ref[...]
        o_ref[...] = x1 + gt_ref[0] * acc_scr[...]


def _mlp_call(x, g, sh, sc, gt, w1, w2, rows_per_group, mixer_proj=None):
    rows, d = x.shape
    f = w1.shape[1]
    tm = min(1024, rows)
    tf = 1024
    ms = _mod_spec(sh, tm, rows_per_group)
    in_specs = [pl.BlockSpec((tm, d), lambda i, j: (i, 0))]
    args = [x]
    scratch = [pltpu.VMEM((tm, d), BF16), pltpu.VMEM((tm, d), F32)]
    if mixer_proj is not None:
        a, w_mix, gt1 = mixer_proj
        in_specs += [pl.BlockSpec((tm, a.shape[1]), lambda i, j: (i, 0)),
                     pl.BlockSpec(w_mix.shape, lambda i, j: (0, 0)),
                     ms]
        args += [a, w_mix, gt1]
        scratch.append(pltpu.VMEM((tm, d), F32))
    in_specs += [pl.BlockSpec((1, d), lambda i, j: (0, 0)),
                 ms, ms, ms,
                 pl.BlockSpec((d, tf), lambda i, j: (0, j)),
                 pl.BlockSpec((tf, d), lambda i, j: (j, 0))]
    return pl.pallas_call(
        functools.partial(_mlp_kernel, mixer_proj=mixer_proj is not None),
        grid=(rows // tm, f // tf),
        in_specs=in_specs,
        out_specs=pl.BlockSpec((tm, d), lambda i, j: (i, 0)),
        out_shape=jax.ShapeDtypeStruct((rows, d), F32),
        scratch_shapes=scratch,
        compiler_params=_params("parallel", "arbitrary"),
        name="mlp",
    )(*args, g, sh, sc, gt, w1, w2)


def _conv_kernel(*refs, tm, d, tiles_per_seq, has_override):
    if has_override:
        x_ref, g_ref, sh_ref, sc_ref, w_ref, k_ref, ov_ref, bv_ref, cu_ref, e_scr = refs
    else:
        x_ref, g_ref, sh_ref, sc_ref, w_ref, k_ref, bv_ref, cu_ref, e_scr = refs
    i = pl.program_id(0)
    if has_override:
        e_scr[0:8, :] = jnp.zeros((8, d), F32)
    else:
        @pl.when(i % tiles_per_seq == 0)
        def _():
            e_scr[0:8, :] = jnp.zeros((8, d), F32)

        @pl.when(i % tiles_per_seq != 0)
        def _():
            e_scr[0:8, :] = e_scr[tm:tm + 8, :]
    n_part = 2 if tm % 256 == 0 else 1
    hr = tm // n_part
    gates = []
    for part in range(n_part):
        rs = slice(part * hr, (part + 1) * hr)
        sh, sc = sh_ref[0], sc_ref[0]
        sh = sh if sh.shape[0] == 1 else sh[rs]
        sc = sc if sc.shape[0] == 1 else sc[rs]
        proj = _dot(_rms_mod(x_ref[rs, :], g_ref[...], sh, sc).astype(BF16), w_ref[...])
        cu = proj[:, d:2 * d] * proj[:, 2 * d:]
        if has_override:
            row = lax.broadcasted_iota(jnp.int32, (hr, 1), 0)
            cu = jnp.where((row % CONV_GRP) >= CONV_GRP // 2, cu, ov_ref[rs, :])
        e_scr[8 + part * hr:8 + (part + 1) * hr, :] = cu
        cu_ref[rs, :] = cu
        gates.append(proj[:, :d])
    k = k_ref[...]
    for part in range(n_part):
        lo = part * hr
        v = (k[0:1, :] * e_scr[6 + lo:6 + lo + hr, :] + k[1:2, :] * e_scr[7 + lo:7 + lo + hr, :]
             + k[2:3, :] * e_scr[8 + lo:8 + lo + hr, :])
        bv_ref[lo:lo + hr, :] = (gates[part] * v).astype(BF16)


def _conv_call(x, g, sh, sc, w_in, k, override, rows_per_group):
    rows, d = x.shape
    tm = min(512, rows)
    has_override = override is not None
    ms = _mod_spec(sh, tm, rows_per_group)
    in_specs = [pl.BlockSpec((tm, d), lambda i: (i, 0)),
                pl.BlockSpec((1, d), lambda i: (0, 0)),
                ms, ms,
                pl.BlockSpec((d, 3 * d), lambda i: (0, 0)),
                pl.BlockSpec((CONV_WIDTH, d), lambda i: (0, 0))]
    args = [x, g, sh, sc, w_in, k]
    if has_override:
        in_specs.append(pl.BlockSpec((tm, d), lambda i: (i, 0)))
        args.append(override)
    kern = functools.partial(_conv_kernel, tm=tm, d=d, tiles_per_seq=max(rows_per_group // tm, 1),
                             has_override=has_override)
    return pl.pallas_call(
        kern,
        grid=(rows // tm,),
        in_specs=in_specs,
        out_specs=[pl.BlockSpec((tm, d), lambda i: (i, 0)), pl.BlockSpec((tm, d), lambda i: (i, 0))],
        out_shape=[jax.ShapeDtypeStruct((rows, d), BF16), jax.ShapeDtypeStruct((rows, d), F32)],
        scratch_shapes=[pltpu.VMEM((tm + 8, d), F32)],
        compiler_params=_params("arbitrary"),
        name="conv_in",
    )(*args)


def _pool_kernel(*refs, tm, d, tiles_per_seq, sample, pos0):
    x_ref, g_ref, sh_ref, sc_ref, gt_ref, w_ref, ps_ref, o_ref, st_ref, e_scr = refs
    i = pl.program_id(0)
    x = x_ref[...]
    h = _rms_mod(x, g_ref[...], sh_ref[0], sc_ref[0])
    row = lax.broadcasted_iota(jnp.int32, (tm, 1), 0)
    if sample:
        local = (row % POOL_GRP) - POOL_GRP // 2
        h = jnp.where(local >= 0, h, x)
        e_scr[0:16, :] = jnp.zeros((16, d), F32)
    else:
        local = (i % tiles_per_seq) * tm + row

        @pl.when(i % tiles_per_seq == 0)
        def _():
            e_scr[0:16, :] = jnp.zeros((16, d), F32)

        @pl.when(i % tiles_per_seq != 0)
        def _():
            e_scr[0:16, :] = e_scr[tm:tm + 16, :]
    e_scr[16:tm + 16, :] = h
    if sample:
        st_ref[...] = h
    else:
        st_ref[0] = e_scr[tm:tm + 16, :]
    gq = d // len(POOL_WINDOWS)
    for gi, w in enumerate(POOL_WINDOWS):
        lo = gi * gq
        win = h[:, lo:lo + gq]
        for j in range(1, w):
            win = win + e_scr[16 - j:16 - j + tm, lo:lo + gq]
        cnt = jnp.clip(pos0 + local + 1, 1, w).astype(F32)
        dd = (win / cnt - h[:, lo:lo + gq]).astype(BF16)
        y = _dot(dd, w_ref[gi]) * ps_ref[:, lo:lo + gq]
        o_ref[:, lo:lo + gq] = x[:, lo:lo + gq] + gt_ref[0][:, lo:lo + gq] * y


def _pool_call(x, g, sh, sc, gt, w, ps, rows_per_group, sample, pos0):
    rows, d = x.shape
    tm = min(512, rows)
    ms = _mod_spec(sh, tm, rows_per_group)
    gq = d // len(POOL_WINDOWS)
    if sample:
        st_spec = pl.BlockSpec((tm, d), lambda i: (i, 0))
        st_shape = jax.ShapeDtypeStruct((rows, d), F32)
    else:
        st_spec = pl.BlockSpec((1, 16, d), lambda i: (i, 0, 0))
        st_shape = jax.ShapeDtypeStruct((rows // tm, 16, d), F32)
    kern = functools.partial(_pool_kernel, tm=tm, d=d, tiles_per_seq=max(rows_per_group // tm, 1),
                             sample=sample, pos0=pos0)
    return pl.pallas_call(
        kern,
        grid=(rows // tm,),
        in_specs=[pl.BlockSpec((tm, d), lambda i: (i, 0)),
                  pl.BlockSpec((1, d), lambda i: (0, 0)),
                  ms, ms, ms,
                  pl.BlockSpec((len(POOL_WINDOWS), gq, gq), lambda i: (0, 0, 0)),
                  pl.BlockSpec((1, d), lambda i: (0, 0))],
        out_specs=[pl.BlockSpec((tm, d), lambda i: (i, 0)), st_spec],
        out_shape=[jax.ShapeDtypeStruct((rows, d), F32), st_shape],
        scratch_shapes=[pltpu.VMEM((tm + 16, d), F32)],
        compiler_params=_params("arbitrary"),
        name="pool",
    )(x, g, sh, sc, gt, w, ps)


def _swa_qkv_kernel(*refs, tm, dil, keeps, tps):
    x_ref, g_ref, sh_ref, sc_ref, w_ref, gn_ref, seg_ref = refs[:7]
    outs = refs[7:16]
    states = refs[16:16 + len(keeps)]
    z_scr, h_scr = refs[16 + len(keeps):]
    w = SWA_WIDTH
    tile = pl.program_id(0) % tps
    h_scr[...] = _rms_mod(x_ref[...], g_ref[...], sh_ref[0], sc_ref[0]).astype(BF16)
    for jb in range(9):
        which, g = divmod(jb, 3)
        z = _dot(h_scr[...], w_ref[:, jb * w:(jb + 1) * w])
        if which < 2:
            ms = _dot((z * z).astype(BF16), seg_ref[...])
            z = z * lax.rsqrt(ms + EPS) * gn_ref[jb:jb + 1, :]
        if keeps and which > 0:
            keep = keeps[g]
            zk = z if keep >= tm else z[tm - keep:, :]

            def put(st_ref=states[g], kv=which - 1, zk=zk):
                st_ref[0, kv] = zk.T

            first_kept = tps - max(keep // tm, 1)
            if first_kept == 0:
                put()
            else:
                pl.when(tile >= first_kept)(put)
        r = dil[g]
        if r == 1:
            outs[jb][0, 0] = z
        else:
            for c in range(w // 128):
                z_scr[c] = z[:, c * 128:(c + 1) * 128]
            for rho in range(r):
                for c in range(w // 128):
                    outs[jb][0, rho, :, c * 128:(c + 1) * 128] = z_scr[c, pl.ds(rho, tm // r, stride=r), :]


def _swa_qkv_call(x, g, sh, sc, w, gn, seg, n_seq, seq_len, dil, rows_per_group, keeps=()):
    rows, d = x.shape
    wd = SWA_WIDTH
    tm = min(512, seq_len)
    tps = seq_len // tm
    ms = _mod_spec(sh, tm, rows_per_group)
    out_specs, out_shape = [], []
    for jb in range(9):
        r = dil[jb % 3]
        out_specs.append(pl.BlockSpec((1, r, tm // r, wd), lambda i: (i // tps, 0, i % tps, 0)))
        out_shape.append(jax.ShapeDtypeStruct((n_seq, r, seq_len // r, wd), F32))
    for keep in keeps:
        assert keep % tm == 0 or tm % keep == 0
        lanes = min(keep, tm)
        first_kept = tps - max(keep // tm, 1)
        out_specs.append(pl.BlockSpec(
            (1, 2, wd, lanes),
            lambda i, first_kept=first_kept: (i // tps, 0, 0, jnp.maximum(i % tps - first_kept, 0))))
        out_shape.append(jax.ShapeDtypeStruct((n_seq, 2, wd, keep), F32))
    return pl.pallas_call(
        functools.partial(_swa_qkv_kernel, tm=tm, dil=dil, keeps=tuple(keeps), tps=tps),
        grid=(rows // tm,),
        in_specs=[pl.BlockSpec((tm, d), lambda i: (i, 0)),
                  pl.BlockSpec((1, d), lambda i: (0, 0)),
                  ms, ms,
                  pl.BlockSpec(w.shape, lambda i: (0, 0), pipeline_mode=pl.Buffered(1)),
                  pl.BlockSpec(gn.shape, lambda i: (0, 0)),
                  pl.BlockSpec(seg.shape, lambda i: (0, 0))],
        out_specs=out_specs,
        out_shape=out_shape,
        scratch_shapes=[pltpu.VMEM((wd // 128, tm, 128), F32), pltpu.VMEM((tm, d), BF16)],
        compiler_params=_params("arbitrary"),
        name="swa_qkv",
    )(x, g, sh, sc, w, gn, seg)


def _swa_band_kernel(q_ref, kc_ref, kp_ref, vc_ref, vp_ref, o_ref, l_ref):
    b = pl.program_id(2)
    blk = SWA_BLK
    hd = SWA_HEAD_DIM
    key = lax.broadcasted_iota(jnp.int32, (2 * blk, blk), 0)
    qi = lax.broadcasted_iota(jnp.int32, (2 * blk, blk), 1)
    dist = qi + blk - key
    ok = jnp.logical_and(dist >= 0, dist <= jnp.where(b > 0, blk, qi))
    lane_head = lax.broadcasted_iota(jnp.int32, (blk, 2 * hd), 1) // hd
    for i in range(q_ref.shape[0]):
        q = q_ref[i, 0]
        k = jnp.concatenate([kp_ref[i, 0], kc_ref[i, 0]], axis=0).astype(BF16)
        vt = jnp.concatenate([vp_ref[i, 0], vc_ref[i, 0]], axis=0).T.astype(BF16)
        ots, lts = [], []
        for h in range(SWA_HEADS):
            slab = slice((h // 2) * 2 * hd, (h // 2 + 1) * 2 * hd)
            qm = jnp.where(lane_head == h % 2, q[:, slab], 0.0).astype(BF16)
            st = jnp.where(ok, _dot_nt(k[:, slab], qm) * SWA_SCALE, NEG)
            m = jnp.max(st, axis=0, keepdims=True)
            p = jnp.exp(st - m)
            l = jnp.sum(p, axis=0, keepdims=True)
            ots.append(_dot(vt[h * hd:(h + 1) * hd, :], p.astype(BF16)) * (1.0 / l))
            lts.append(jnp.broadcast_to(m + jnp.log(l), (hd, blk)))
        o_ref[i] = jnp.concatenate(ots, axis=0).T.astype(o_ref.dtype)
        l_ref[i] = jnp.concatenate(lts, axis=0).T


def _swa_band_call(q, k, v):
    n_seq, r, lr, w = q.shape
    nb = lr // SWA_BLK
    ns = next(c for c in (8, 4, 2, 1) if n_seq % c == 0)
    blk = (ns, 1, SWA_BLK, w)
    cur = pl.BlockSpec(blk, lambda n, rho, b: (n, rho, b, 0))
    prev = pl.BlockSpec(blk, lambda n, rho, b: (n, rho, jnp.maximum(b - 1, 0), 0))
    out_spec = pl.BlockSpec((ns, SWA_BLK, w), lambda n, rho, b: (n, b, rho))
    o, l = pl.pallas_call(
        _swa_band_kernel,
        grid=(n_seq // ns, r, nb),
        in_specs=[cur, cur, prev, cur, prev],
        out_specs=[out_spec, out_spec],
        out_shape=[jax.ShapeDtypeStruct((n_seq, lr, r * w), BF16), jax.ShapeDtypeStruct((n_seq, lr, r * w), F32)],
        compiler_params=_params("parallel", "parallel", "arbitrary"),
        name="swa_band",
    )(q, k, k, v, v)
    return o, l


def _swa_merge_kernel(*refs, dil, tm):
    o_refs, l_refs = refs[0:3], refs[3:6]
    w_ref, x_ref, gt_ref, out_ref = refs[6:10]
    scrs = list(refs[10:])
    wd = SWA_WIDTH

    def token_order(ref, r):
        if r == 1:
            return ref[0].astype(F32)
        scr = scrs.pop()
        for rho in range(r):
            for c in range(wd // 128):
                lo = rho * wd + c * 128
                scr[c, pl.ds(rho, tm // r, stride=r), :] = ref[0, :, lo:lo + 128].astype(F32)
        return jnp.concatenate([scr[c] for c in range(wd // 128)], axis=1)

    a = [token_order(l_refs[g], dil[g]) for g in range(3)]
    o = [token_order(o_refs[g], dil[g]) for g in range(3)]
    m = jnp.maximum(jnp.maximum(a[0], a[1]), a[2])
    e0, e1, e2 = jnp.exp(a[0] - m), jnp.exp(a[1] - m), jnp.exp(a[2] - m)
    inv = 1.0 / (e0 + e1 + e2)
    merged = (e0 * inv) * o[0] + (e1 * inv) * o[1] + (e2 * inv) * o[2]
    out_ref[...] = x_ref[...] + gt_ref[0] * _dot(merged.astype(BF16), w_ref[...])


def _swa_merge_call(os_, ls_, dil, w_o, x, gt, seq_len, rows_per_group):
    rows, d = x.shape
    tm = min(512, seq_len)
    tps = seq_len // tm
    w = SWA_WIDTH
    specs = [pl.BlockSpec((1, tm // r, r * w), lambda i: (i // tps, i % tps, 0)) for r in dil]
    n_scr = 2 * sum(1 for r in dil if r > 1)
    return pl.pallas_call(
        functools.partial(_swa_merge_kernel, dil=dil, tm=tm),
        grid=(rows // tm,),
        in_specs=specs + specs + [pl.BlockSpec((w, d), lambda i: (0, 0)),
                                  pl.BlockSpec((tm, d), lambda i: (i, 0)),
                                  _mod_spec(gt, tm, rows_per_group)],
        out_specs=pl.BlockSpec((tm, d), lambda i: (i, 0)),
        out_shape=jax.ShapeDtypeStruct((rows, d), F32),
        scratch_shapes=[pltpu.VMEM((w // 128, tm, 128), F32)] * n_scr,
        compiler_params=_params("parallel"),
        name="swa_merge",
    )(*os_, *ls_, w_o, x, gt)


def _swa_step_kernel(q_ref, st_ref, new_ref, ns_ref, o_ref, l_ref, *, win, r, lb, s_new, n_blk):
    w = SWA_WIDTH
    hd = SWA_HEAD_DIM
    nrow = SWA_HEADS * s_new
    pad = 128 - s_new
    rowi = lax.broadcasted_iota(jnp.int32, (nrow, w), 0)
    lanei = lax.broadcasted_iota(jnp.int32, (nrow, w), 1)
    diag = (rowi // s_new) == (lanei // hd)
    s_idx = lax.broadcasted_iota(jnp.int32, (nrow, lb + 128), 0) % s_new
    col = lax.broadcasted_iota(jnp.int32, (nrow, lb + 128), 1)
    dist = lb + s_idx - jnp.where(col < lb, col, col - pad)
    ok = jnp.logical_and(jnp.logical_and(dist >= 0, dist % r == 0), dist <= win)
    ok = jnp.logical_and(ok, jnp.logical_or(col < lb, col >= lb + pad))
    per_slab = 128 // s_new
    for i in range(n_blk):
        if n_blk % per_slab == 0:
            slab, shift = i // per_slab, pad - (i % per_slab) * s_new
        else:
            slab, shift = 0, pad - ((pl.program_id(0) * n_blk + i) % per_slab) * s_new
        new_k = pltpu.roll(new_ref[slab, 0], shift, axis=1)
        new_v = pltpu.roll(new_ref[slab, 1], shift, axis=1)
        q = q_ref[i]
        qbd = jnp.where(diag, jnp.concatenate([q] * SWA_HEADS, axis=0), 0.0).astype(BF16)
        kt = jnp.concatenate([st_ref[i, 0], new_k], axis=1).astype(BF16)
        vt = jnp.concatenate([st_ref[i, 1], new_v], axis=1).astype(BF16)
        sc = jnp.where(ok, _dot(qbd, kt) * SWA_SCALE, NEG)
        m = jnp.max(sc, axis=-1, keepdims=True)
        p = jnp.exp(sc - m)
        l = jnp.sum(p, axis=-1, keepdims=True)
        acc = jnp.where(diag, _dot_nt(p.astype(BF16), vt) / l, 0.0)
        lse = jnp.where(diag, m + jnp.log(l), 0.0)
        o = acc[0:s_new]
        ls = lse[0:s_new]
        for h in range(1, SWA_HEADS):
            o = o + acc[h * s_new:(h + 1) * s_new]
            ls = ls + lse[h * s_new:(h + 1) * s_new]
        o_ref[i] = o
        l_ref[i] = ls
        for kv, new in enumerate((new_k, new_v)):
            ns_ref[i, kv, :, 0:lb - s_new] = st_ref[i, kv, :, s_new:lb]
            ns_ref[i, kv, :, lb - s_new:lb] = new[:, pad:128]


def _swa_step_call(q, st_t, new_t, win, r):
    n_seq, s_new, w = q.shape
    lb = st_t.shape[-1]
    n_blk = max(1, min(n_seq, SWA_GROUPS[-1][0] // lb))
    per_slab = 128 // s_new
    assert n_seq % n_blk == 0 and (n_blk % per_slab == 0 or per_slab % n_blk == 0)
    if n_blk % per_slab == 0:
        new_spec = pl.BlockSpec((n_blk // per_slab, 2, w, 128), lambda n: (n, 0, 0, 0))
    else:
        new_spec = pl.BlockSpec((1, 2, w, 128), lambda n: ((n * n_blk) // per_slab, 0, 0, 0))
    return pl.pallas_call(
        functools.partial(_swa_step_kernel, win=win, r=r, lb=lb, s_new=s_new, n_blk=n_blk),
        grid=(n_seq // n_blk,),
        in_specs=[pl.BlockSpec((n_blk, s_new, w), lambda n: (n, 0, 0)),
                  pl.BlockSpec((n_blk, 2, w, lb), lambda n: (n, 0, 0, 0)),
                  new_spec],
        out_specs=[pl.BlockSpec((n_blk, 2, w, lb), lambda n: (n, 0, 0, 0)),
                   pl.BlockSpec((n_blk, s_new, w), lambda n: (n, 0, 0)),
                   pl.BlockSpec((n_blk, s_new, w), lambda n: (n, 0, 0))],
        out_shape=[jax.ShapeDtypeStruct((n_seq, 2, w, lb), F32),
                   jax.ShapeDtypeStruct((n_seq, s_new, w), F32),
                   jax.ShapeDtypeStruct((n_seq, s_new, w), F32)],
        compiler_params=_params("parallel"),
        name="swa_step",
    )(q, st_t, new_t)


def _rope128(bn, cos_t, sin_t):
    return bn * cos_t + pltpu.roll(bn, 64, axis=1) * sin_t


def _mla_proj_kernel(*refs, decode):
    (x_ref, g_ref, sh_ref, sc_ref, w1_ref, gqa_ref, wqb_ref, gq_ref, gkv_ref, gkr_ref,
     cos_ref, sin_ref, wx_ref) = refs[:13]
    if decode:
        outs = refs[13:]
    else:
        wvt_ref = refs[13]
        outs = refs[14:]
    nh = MLA_HEADS
    if decode:
        lat_ref, qabs_ref, qpe_ref = outs
    else:
        lat_ref, qcat_ref, kcat_ref, vt_ref = outs
    gq = gq_ref[...]
    tm = x_ref.shape[0]
    n_part = 2 if tm % 256 == 0 else 1

    def rows_of(ref, rs):
        v = ref[0]
        return v if v.shape[0] == 1 else v[rs]

    for part in range(n_part):
        rs = slice(part * (tm // n_part), (part + 1) * (tm // n_part))
        h = _rms_mod(x_ref[rs, :], g_ref[...], rows_of(sh_ref, rs), rows_of(sc_ref, rs)).astype(BF16)
        a = _dot(h, w1_ref[...])
        qa = a[:, :Q_LORA]
        qa = (qa * lax.rsqrt(jnp.mean(qa * qa, axis=-1, keepdims=True) + EPS) * gqa_ref[...]).astype(BF16)
        q = _dot(qa, wqb_ref[...])
        cos_t = cos_ref[rs, :]
        sin_t = sin_ref[rs, :]
        latc = a[:, Q_LORA:Q_LORA + KV_LORA]
        latc = latc * lax.rsqrt(jnp.mean(latc * latc, axis=-1, keepdims=True) + EPS) * gkv_ref[...]
        kb = a[:, Q_LORA + KV_LORA:]
        kb = kb * lax.rsqrt(jnp.sum(kb * kb, axis=-1, keepdims=True) * (1.0 / QK_ROPE) + EPS) * gkr_ref[...]
        kb = _rope128(kb, cos_t, sin_t)
        lane = lax.broadcasted_iota(jnp.int32, kb.shape, 1)
        kstd = jnp.where(lane < 32, kb, pltpu.roll(kb, 96, axis=1))
        lat_ref[rs, 0:KV_LORA] = latc
        lat_ref[rs, KV_LORA:KV_LORA + QK_ROPE] = kstd[:, 0:QK_ROPE]
        for hh in range(nh):
            an = q[:, hh * 256:hh * 256 + 128]
            an = an * lax.rsqrt(jnp.mean(an * an, axis=-1, keepdims=True) + EPS) * gq[:, 0:128]
            bn = q[:, hh * 256 + 128:hh * 256 + 256]
            bn = bn * lax.rsqrt(jnp.sum(bn * bn, axis=-1, keepdims=True) * (1.0 / QK_ROPE) + EPS) * gq[:, 128:256]
            bn = _rope128(bn, cos_t, sin_t)
            if decode:
                ag = an * gq[:, 256:384]
                a_hi = ag.astype(BF16)
                a_lo = (ag - a_hi.astype(F32)).astype(BF16)
                qabs_ref[rs, hh * 256:(hh + 1) * 256] = (_dot(a_hi, wx_ref[hh]) + _dot(a_lo, wx_ref[hh])).astype(BF16)
                qpe_ref[rs, hh * 128:(hh + 1) * 128] = jnp.where(lane < 32, bn, pltpu.roll(bn, 96, axis=1)).astype(BF16)
            else:
                qcat_ref[rs, hh * 256:hh * 256 + 128] = an.astype(BF16)
                qcat_ref[rs, hh * 256 + 128:(hh + 1) * 256] = bn.astype(BF16)
        if not decode:
            gkn = gq_ref[:, 256:384]
            latb = latc.astype(BF16)
            kvx = _dot(latb, wx_ref[...])
            kbb = kb.astype(BF16)
            for hh in range(nh):
                kn = kvx[:, hh * 128:(hh + 1) * 128]
                kn = kn * lax.rsqrt(jnp.mean(kn * kn, axis=-1, keepdims=True) + EPS) * gkn
                kcat_ref[rs, hh * 256:hh * 256 + 128] = kn.astype(BF16)
                kcat_ref[rs, hh * 256 + 128:(hh + 1) * 256] = kbb
            vt_ref[0, :, rs] = _dot_nt(wvt_ref[...], latb).astype(BF16)


def _mla_proj_call(x, g, sh, sc, wts, cos_t, sin_t, rows_per_group, decode):
    rows, d = x.shape
    tm = min(512, rows)
    nh = MLA_HEADS
    ms = _mod_spec(sh, tm, rows_per_group)
    tps = max(rows_per_group // tm, 1)
    if decode:
        tab_spec = pl.BlockSpec((tm, 128), lambda i: (0, 0))
        wx = wts["wk_t"]
        wx_spec = pl.BlockSpec(wx.shape, lambda i: (0, 0, 0))
        out_specs = [pl.BlockSpec((tm, KV_LORA + QK_ROPE), lambda i: (i, 0)),
                     pl.BlockSpec((tm, nh * 256), lambda i: (i, 0)),
                     pl.BlockSpec((tm, nh * 128), lambda i: (i, 0))]
        out_shape = [jax.ShapeDtypeStruct((rows, KV_LORA + QK_ROPE), F32),
                     jax.ShapeDtypeStruct((rows, nh * 256), BF16),
                     jax.ShapeDtypeStruct((rows, nh * 128), BF16)]
    else:
        tab_spec = pl.BlockSpec((tm, 128), lambda i: (i % tps, 0))
        wx = wts["w_kn"]
        wx_spec = pl.BlockSpec(wx.shape, lambda i: (0, 0))
        out_specs = [pl.BlockSpec((tm, KV_LORA + QK_ROPE), lambda i: (i, 0)),
                     pl.BlockSpec((tm, nh * 256), lambda i: (i, 0)),
                     pl.BlockSpec((tm, nh * 256), lambda i: (i, 0)),
                     pl.BlockSpec((1, nh * V_DIM, tm), lambda i: (i // tps, 0, i % tps))]
        out_shape = [jax.ShapeDtypeStruct((rows, KV_LORA + QK_ROPE), F32),
                     jax.ShapeDtypeStruct((rows, nh * 256), BF16),
                     jax.ShapeDtypeStruct((rows, nh * 256), BF16),
                     jax.ShapeDtypeStruct((rows // rows_per_group, nh * V_DIM, rows_per_group), BF16)]

    def full(a):
        return pl.BlockSpec(a.shape, lambda i: (0,) * a.ndim)

    extra = [] if decode else [wts["wv_t"]]
    return pl.pallas_call(
        functools.partial(_mla_proj_kernel, decode=decode),
        grid=(rows // tm,),
        in_specs=[pl.BlockSpec((tm, d), lambda i: (i, 0)),
                  pl.BlockSpec((1, d), lambda i: (0, 0)),
                  ms, ms,
                  full(wts["w1"]), full(wts["g_qa"]), full(wts["w_qb"]), full(wts["gq"]),
                  full(wts["g_kva"]), full(wts["g_kr"]),
                  tab_spec, tab_spec, wx_spec] + [full(a) for a in extra],
        out_specs=out_specs,
        out_shape=out_shape,
        compiler_params=_params("parallel"),
        name="mla_proj_decode" if decode else "mla_proj",
    )(x, g, sh, sc, wts["w1"], wts["g_qa"], wts["w_qb"], wts["gq"], wts["g_kva"], wts["g_kr"],
      cos_t, sin_t, wx, *extra)


def _mla_causal_kernel(q_ref, k_ref, vt_ref, o_ref, *, seq_len, tq):
    r = lax.broadcasted_iota(jnp.int32, (tq, tq), 0)
    c = lax.broadcasted_iota(jnp.int32, (tq, tq), 1)
    ex = MLA_SCALE * LOG2_E
    for qb in range(seq_len // tq):
        lo = qb * tq
        q = q_ref[0, lo:lo + tq, :]
        s_d = jnp.where(r <= c, _dot_nt(k_ref[0, lo:lo + tq, :], q), NEG)
        m = jnp.max(s_d, axis=0, keepdims=True)
        if qb > 0:
            s_p = _dot_nt(k_ref[0, 0:lo, :], q)
            m = jnp.maximum(m, jnp.max(s_p, axis=0, keepdims=True))
        p_d = jnp.exp2((s_d - m) * ex)
        l = jnp.sum(p_d, axis=0, keepdims=True)
        ot = _dot(vt_ref[0, :, lo:lo + tq], p_d.astype(BF16))
        if qb > 0:
            p_p = jnp.exp2((s_p - m) * ex)
            l = l + jnp.sum(p_p, axis=0, keepdims=True)
            ot = ot + _dot(vt_ref[0, :, 0:lo], p_p.astype(BF16))
        o_ref[0, lo:lo + tq, :] = (ot * (1.0 / l)).T.astype(BF16)


def _mla_causal_call(qcat, kcat, vt, n_seq, seq_len):
    nh = MLA_HEADS
    tq = min(512, seq_len)
    q3 = qcat.reshape(n_seq, seq_len, nh * 256)
    k3 = kcat.reshape(n_seq, seq_len, nh * 256)
    o = pl.pallas_call(
        functools.partial(_mla_causal_kernel, seq_len=seq_len, tq=tq),
        grid=(n_seq, nh),
        in_specs=[pl.BlockSpec((1, seq_len, 256), lambda n, h: (n, 0, h)),
                  pl.BlockSpec((1, seq_len, 256), lambda n, h: (n, 0, h)),
                  pl.BlockSpec((1, V_DIM, seq_len), lambda n, h: (n, h, 0))],
        out_specs=pl.BlockSpec((1, seq_len, V_DIM), lambda n, h: (n, 0, h)),
        out_shape=jax.ShapeDtypeStruct((n_seq, seq_len, nh * V_DIM), BF16),
        compiler_params=_params("parallel", "parallel"),
        name="mla_causal",
    )(q3, k3, vt)
    return o.reshape(n_seq * seq_len, nh * V_DIM)


def _mla_decode_kernel(pt_ref, *refs, n_pg, s_new):
    del pt_ref
    page_refs = refs[:n_pg]
    (qabs_ref, qpe_ref, latn_ref, wkt_ref, wv_ref, o_ref, m_scr, l_scr, ctx_scr) = refs[n_pg:]
    c = pl.program_id(1)
    nh = MLA_HEADS
    nrow = nh * s_new

    @pl.when(c == 0)
    def _():
        m_scr[...] = jnp.full_like(m_scr, NEG)
        l_scr[...] = jnp.zeros_like(l_scr)
        ctx_scr[...] = jnp.zeros_like(ctx_scr)

    def process(pages, mask):
        latc = jnp.concatenate([pg[0:KV_LORA, :].astype(BF16) for pg in pages], axis=1)
        kpe = jnp.concatenate([pg[KV_LORA:KV_LORA + QK_ROPE, :].astype(BF16) for pg in pages], axis=1)
        r = _dot(wkt_ref[...], latc)
        rq = _dot(qabs_ref[0], latc)
        spe = _dot(qpe_ref[0][:, 0:QK_ROPE], kpe)
        parts = []
        for h in range(nh):
            blk = r[h * QK_NOPE:(h + 1) * QK_NOPE]
            rstd = lax.rsqrt(jnp.sum(blk * blk, axis=0, keepdims=True) * (1.0 / QK_NOPE) + EPS)
            parts.append(rq[h * s_new:(h + 1) * s_new] * rstd)
        s = (jnp.concatenate(parts, axis=0) + spe) * MLA_SCALE
        if mask is not None:
            s = jnp.where(mask, s, NEG)
        m_old = m_scr[...]
        m_new = jnp.maximum(m_old, jnp.max(s, axis=-1, keepdims=True))
        alpha = jnp.exp(m_old - m_new)
        p = jnp.exp(s - m_new)
        l_scr[...] = alpha * l_scr[...] + jnp.sum(p, axis=-1, keepdims=True)
        ctx_scr[...] = alpha * ctx_scr[...] + _dot_nt(p.astype(BF16), latc)
        m_scr[...] = m_new

    for lo in range(0, n_pg, 16):
        process([pr[0, 0] for pr in page_refs[lo:lo + 16]], None)

    @pl.when(c == pl.num_programs(1) - 1)
    def _():
        si = lax.broadcasted_iota(jnp.int32, (nrow, PAGE_SIZE), 0) % s_new
        kj = lax.broadcasted_iota(jnp.int32, (nrow, PAGE_SIZE), 1)
        process([latn_ref[0]], kj <= si)
        ctx = ctx_scr[...] / l_scr[...]
        for h in range(nh):
            ctx_h = ctx[h * s_new:(h + 1) * s_new].astype(BF16)
            o_ref[0, :, h * V_DIM:(h + 1) * V_DIM] = _dot(ctx_h, wv_ref[h]).astype(BF16)


def _mla_decode_call(cache_t, layer, page_table, qabs, qpe, lat_new, wkt, wv, n_seq, s_new):
    nh = MLA_HEADS
    n_pages = page_table.shape[1]
    n_pg = min(64, n_pages)
    lw = KV_LORA + QK_ROPE
    nrow = nh * s_new

    def page_spec(p):
        return pl.BlockSpec((1, 1, lw, PAGE_SIZE),
                            lambda n, c, pt: (layer, pt[n * n_pages + c * n_pg + p], 0, 0))

    latn_t = jnp.transpose(lat_new.reshape(n_seq, s_new, lw), (0, 2, 1))
    latn_t = jnp.pad(latn_t, ((0, 0), (0, 0), (0, PAGE_SIZE - s_new)))
    grid_spec = pltpu.PrefetchScalarGridSpec(
        num_scalar_prefetch=1,
        grid=(n_seq, n_pages // n_pg),
        in_specs=[page_spec(p) for p in range(n_pg)] + [
            pl.BlockSpec((1, nrow, 256), lambda n, c, pt: (n, 0, 0)),
            pl.BlockSpec((1, nrow, 128), lambda n, c, pt: (n, 0, 0)),
            pl.BlockSpec((1, lw, PAGE_SIZE), lambda n, c, pt: (n, 0, 0)),
            pl.BlockSpec(wkt.shape, lambda n, c, pt: (0, 0)),
            pl.BlockSpec(wv.shape, lambda n, c, pt: (0, 0, 0))],
        out_specs=pl.BlockSpec((1, s_new, nh * V_DIM), lambda n, c, pt: (n, 0, 0)),
        scratch_shapes=[pltpu.VMEM((nrow, 1), F32), pltpu.VMEM((nrow, 1), F32),
                        pltpu.VMEM((nrow, KV_LORA), F32)])
    o = pl.pallas_call(
        functools.partial(_mla_decode_kernel, n_pg=n_pg, s_new=s_new),
        grid_spec=grid_spec,
        out_shape=jax.ShapeDtypeStruct((n_seq, s_new, nh * V_DIM), BF16),
        compiler_params=_params("parallel", "arbitrary"),
        name="mla_decode",
    )(page_table.reshape(-1), *([cache_t] * n_pg),
      jnp.transpose(qabs.reshape(n_seq, s_new, nh, 256), (0, 2, 1, 3)).reshape(n_seq, nrow, 256),
      jnp.transpose(qpe.reshape(n_seq, s_new, nh, 128), (0, 2, 1, 3)).reshape(n_seq, nrow, 128),
      latn_t, wkt, wv)
    return o.reshape(n_seq * s_new, nh * V_DIM)


def _mla_weights(w_qa, g_qa, w_qb, w_kva, g_kva, w_kvb, g_qn, g_kn, g_qr, g_kr):
    nh = MLA_HEADS
    half = QK_ROPE // 2

    def pad_rope_cols(w):
        z = jnp.zeros(w.shape[:-1] + (half,), w.dtype)
        return jnp.concatenate([w[..., :half], z, w[..., half:], z], axis=-1)

    w_kva_p = jnp.concatenate([w_kva[:, :KV_LORA], pad_rope_cols(w_kva[:, KV_LORA:])], axis=1)
    w1 = jnp.concatenate([w_qa, w_kva_p], axis=1).astype(BF16)
    wq = w_qb.reshape(Q_LORA, nh, QK_NOPE + QK_ROPE)
    wq = jnp.concatenate([wq[..., :QK_NOPE], pad_rope_cols(wq[..., QK_NOPE:])], axis=-1)
    w_qb_p = wq.reshape(Q_LORA, nh * 256).astype(BF16)
    gq = jnp.concatenate([g_qn, pad_rope_cols(g_qr), g_kn])[None, :]
    wkv = w_kvb.reshape(KV_LORA, nh, QK_NOPE + V_DIM)
    w_kn = wkv[..., :QK_NOPE].reshape(KV_LORA, nh * QK_NOPE).astype(BF16)
    wv_t = jnp.transpose(wkv[..., QK_NOPE:], (1, 2, 0)).reshape(nh * V_DIM, KV_LORA).astype(BF16)
    wk_t = jnp.transpose(wkv[..., :QK_NOPE], (1, 2, 0)).astype(BF16)
    return dict(w1=w1, g_qa=g_qa[None, :], w_qb=w_qb_p, gq=gq, g_kva=g_kva[None, :],
                g_kr=pad_rope_cols(g_kr)[None, :], w_kn=w_kn, wv_t=wv_t, wk_t=wk_t,
                wkt_flat=wk_t.reshape(nh * QK_NOPE, KV_LORA),
                wv=jnp.transpose(wkv[..., QK_NOPE:], (1, 0, 2)).astype(BF16))


def _rope_tables(pos):
    half = QK_ROPE // 2
    inv = ROPE_BASE ** (-jnp.arange(half, dtype=F32) / half)
    ang = pos.astype(F32)[:, None] * inv[None, :]
    cos, sin = jnp.cos(ang), jnp.sin(ang)
    z = jnp.zeros_like(cos)
    return (jnp.concatenate([cos, z, cos, z], axis=1), jnp.concatenate([-sin, z, sin, z], axis=1))


def kernel(x_prompt, x_sample, state_conv, state_swa0, state_swa1, state_swa2, state_pool, cache_mla, page_table,
           c_prompt, c_sample, ada_w, ada_b, norm1_g, norm2_g, mlp_w1, mlp_w2, conv_w_in, conv_k, conv_w_out,
           swa_w_qkv, swa_qn_g, swa_kn_g, swa_w_o, pool_w, pool_scale, mla_w_qa, mla_g_qa, mla_w_qb, mla_w_kva,
           mla_g_kva, mla_w_kvb, mla_g_qn, mla_g_kn, mla_g_qr, mla_g_kr, mla_w_o):
    bp, t, d = x_prompt.shape
    bs, s, _ = x_sample.shape
    depth = ada_w.shape[0]
    past = page_table.shape[1] * PAGE_SIZE
    swa_states = (state_swa0, state_swa1, state_swa2)
    for (win, _), st in zip(SWA_GROUPS, swa_states):
        assert st.shape[2] == win, "sliding-window state must hold a full window"
    assert t % (SWA_GROUPS[-1][1] * SWA_BLK) == 0, "prompt length must split into full dilated blocks"
    rows_p, rows_s = bp * t, bs * s
    xp = x_prompt.reshape(rows_p, d)
    xs = x_sample.reshape(rows_s, d)

    mod = _ada_call(jnp.concatenate([c_prompt, c_sample], axis=0), ada_w, ada_b)

    def mods(i):
        out_p, out_s = [], []
        for j in range(6):
            m = mod[i, :, j * d:(j + 1) * d]
            out_p.append(m[:bp, None, :])
            out_s.append(m[bp:])
        return out_p, out_s

    def tok(m, tm):
        return _per_token(m, s, tm)

    tm_s512 = min(512, rows_s)
    tm_s1024 = min(1024, rows_s)
    conv_p = conv_s = pool_p = pool_s = mla_p = mla_s = None
    swa_p, swa_s = [None] * 3, [None] * 3

    for i in range(depth):
        kind, j = i % 4, i // 4
        (sh1p, sc1p, g1p, sh2p, sc2p, g2p), (sh1s, sc1s, g1s, sh2s, sc2s, g2s) = mods(i)
        n1 = norm1_g[i][None, :]
        proj_p = proj_s = None
        if kind == 0:
            w_in = conv_w_in[j].astype(BF16)
            w_out = conv_w_out[j].astype(BF16)
            bv, cu = _conv_call(xp, n1, sh1p, sc1p, w_in, conv_k[j], None, t)
            conv_p = cu.reshape(bp, t, d)[:, t - (CONV_WIDTH - 1):]
            proj_p = (bv, w_out, g1p)
            grp = CONV_GRP
            pad = jnp.zeros((bs, grp - s, d), F32)
            x_ext = jnp.concatenate([pad, xs.reshape(bs, s, d)], axis=1).reshape(bs * grp, d)
            ov = jnp.concatenate([jnp.zeros((bs, grp - s - 2, d), F32), state_conv[j], jnp.zeros((bs, s, d), F32)],
                                 axis=1).reshape(bs * grp, d)
            tme = min(512, bs * grp)
            she = _per_token(sh1s, grp, tme)
            sce = _per_token(sc1s, grp, tme)
            bv, cu = _conv_call(x_ext, n1, she, sce, w_in, conv_k[j], ov, grp)
            conv_s = cu.reshape(bs, grp, d)[:, grp - (CONV_WIDTH - 1):]
            bv = bv.reshape(bs, grp, d)[:, grp - s:].reshape(rows_s, d)
            proj_s = (bv, w_out, tok(g1s, tm_s1024))
        elif kind == 1:
            w_qkv = swa_w_qkv[j].astype(BF16)
            w_o = swa_w_o[j].astype(BF16)
            gn = jnp.concatenate([jnp.tile(swa_qn_g[j], (1, SWA_HEADS)), jnp.tile(swa_kn_g[j], (1, SWA_HEADS))],
                                 axis=0)
            seg = (jnp.arange(SWA_WIDTH)[:, None] // SWA_HEAD_DIM == jnp.arange(SWA_WIDTH)[None, :] // SWA_HEAD_DIM)
            seg = (seg.astype(F32) / SWA_HEAD_DIM).astype(BF16)
            dil = tuple(r for _, r in SWA_GROUPS)
            keeps = tuple(min(win, t) for win, _ in SWA_GROUPS)
            qkv = _swa_qkv_call(xp, n1, sh1p, sc1p, w_qkv, gn, seg, bp, t, dil, t, keeps)
            os_, ls_ = [], []
            for g, (win, r) in enumerate(SWA_GROUPS):
                o_g, l_g = _swa_band_call(qkv[g], qkv[3 + g], qkv[6 + g])
                os_.append(o_g)
                ls_.append(l_g)
                st = qkv[9 + g].reshape(bp, 2, SWA_HEADS, SWA_HEAD_DIM, keeps[g])
                swa_p[g] = jnp.transpose(st, (0, 4, 1, 2, 3))[None]
            xp = _swa_merge_call(os_, ls_, dil, w_o, xp, g1p, t, t)
            tm_q = min(512, rows_s)
            qkv_s = _swa_qkv_call(xs, n1, tok(sh1s, tm_q), tok(sc1s, tm_q), w_qkv, gn, seg, 1, rows_s,
                                  (1, 1, 1), s)
            os_, ls_ = [], []
            for g, (win, r) in enumerate(SWA_GROUPS):
                q_g, k_g, v_g = (qkv_s[c * 3 + g].reshape(bs, s, SWA_WIDTH) for c in range(3))
                st_t = jnp.transpose(swa_states[g][j], (0, 2, 3, 4, 1)).reshape(bs, 2, SWA_WIDTH, win)
                per_slab = 128 // s
                new_t = jnp.stack([k_g, v_g], axis=1).reshape(bs // per_slab, per_slab, 2, s, SWA_WIDTH)
                new_t = jnp.transpose(new_t, (0, 2, 4, 1, 3)).reshape(bs // per_slab, 2, SWA_WIDTH, 128)
                ns, o_g, l_g = _swa_step_call(q_g, st_t, new_t, win, r)
                swa_s[g] = jnp.transpose(ns.reshape(bs, 2, SWA_HEADS, SWA_HEAD_DIM, win), (0, 4, 1, 2, 3))[None]
                os_.append(o_g.reshape(1, rows_s, SWA_WIDTH))
                ls_.append(l_g.reshape(1, rows_s, SWA_WIDTH))
            xs = _swa_merge_call(os_, ls_, (1, 1, 1), w_o, xs, tok(g1s, tm_s512), rows_s, s)
        elif kind == 2:
            pw = pool_w[j].astype(BF16)
            ps = pool_scale[j][None, :]
            xp, tail = _pool_call(xp, n1, sh1p, sc1p, g1p, pw, ps, t, False, 0)
            tpt = t // min(512, rows_p)
            pool_p = tail.reshape(bp, tpt, 16, d)[:, -1, 1:]
            grp = POOL_GRP
            x_ext = jnp.concatenate([jnp.zeros((bs, 1, d), F32), state_pool[j], xs.reshape(bs, s, d),
                                     jnp.zeros((bs, grp - 16 - s, d), F32)], axis=1).reshape(bs * grp, d)
            tme = min(512, bs * grp)
            y_ext, ext = _pool_call(x_ext, n1, _per_token(sh1s, grp, tme), _per_token(sc1s, grp, tme),
                                    _per_token(g1s, grp, tme), pw, ps, grp, True, past)
            xs = y_ext.reshape(bs, grp, d)[:, 16:16 + s].reshape(rows_s, d)
            pool_s = ext.reshape(bs, grp, d)[:, 16 + s - POOL_HIST:16 + s]
        else:
            wts = _mla_weights(mla_w_qa[j], mla_g_qa[j], mla_w_qb[j], mla_w_kva[j], mla_g_kva[j], mla_w_kvb[j],
                               mla_g_qn[j], mla_g_kn[j], mla_g_qr[j], mla_g_kr[j])
            w_o = mla_w_o[j].astype(BF16)
            cos_p, sin_p = _rope_tables(jnp.arange(t))
            lat_p, qcat, kcat, vt = _mla_proj_call(xp, n1, sh1p, sc1p, wts, cos_p, sin_p, t, False)
            o_p = _mla_causal_call(qcat, kcat, vt, bp, t)
            proj_p = (o_p, w_o, g1p)
            mla_p = lat_p.reshape(bp, t, KV_LORA + QK_ROPE)
            cos_s, sin_s = _rope_tables(past + jnp.arange(s))
            reps = tm_s512 // s
            cos_s, sin_s = jnp.tile(cos_s, (reps, 1)), jnp.tile(sin_s, (reps, 1))
            lat_s, qabs, qpe = _mla_proj_call(xs, n1, tok(sh1s, tm_s512), tok(sc1s, tm_s512), wts, cos_s, sin_s, s, True)
            cache_t = jnp.transpose(cache_mla, (0, 1, 3, 2))
            o_s = _mla_decode_call(cache_t, j, page_table, qabs, qpe, lat_s, wts["wkt_flat"], wts["wv"], bs, s)
            proj_s = (o_s, w_o, tok(g1s, tm_s1024))
            mla_s = lat_s.reshape(bs, s, KV_LORA + QK_ROPE)
        n2 = norm2_g[i][None, :]
        w1 = mlp_w1[i].astype(BF16)
        w2 = mlp_w2[i].astype(BF16)
        xp = _mlp_call(xp, n2, sh2p, sc2p, g2p, w1, w2, t, proj_p)
        xs = _mlp_call(xs, n2, tok(sh2s, tm_s1024), tok(sc2s, tm_s1024), tok(g2s, tm_s1024), w1, w2, s, proj_s)

    return (xp.reshape(bp, t, d), xs.reshape(bs, s, d), conv_p[None], conv_s[None],
            swa_p[0], swa_s[0], swa_p[1], swa_s[1], swa_p[2], swa_s[2],
            pool_p[None], pool_s[None], mla_p[None], mla_s[None])
```

```python
import functools

import jax
import jax.numpy as jnp
from jax import lax
from jax.experimental import pallas as pl
from jax.experimental.pallas import tpu as pltpu

F32 = jnp.float32
BF16 = jnp.bfloat16

EPS = 1e-6
NEG = -1e30

CONV_WIDTH = 3
SWA_GROUPS = ((128, 1), (512, 4), (2048, 16))
SWA_HEADS = 8
SWA_HEAD_DIM = 64
SWA_WIDTH = SWA_HEADS * SWA_HEAD_DIM
SWA_BLK = 128
SWA_SCALE = SWA_HEAD_DIM ** -0.5
POOL_WINDOWS = (2, 4, 8, 16)
POOL_HIST = 15
MLA_HEADS = 8
Q_LORA = 384
KV_LORA = 256
QK_NOPE = 128
QK_ROPE = 64
V_DIM = 128
MLA_SCALE = (QK_NOPE + QK_ROPE) ** -0.5
LOG2_E = 1.4426950408889634
ROPE_BASE = 10000.0
PAGE_SIZE = 128

VMEM_LIMIT_BYTES = 56 * 1024 * 1024
CONV_GRP = 16
POOL_GRP = 32


def _params(*sem):
    return pltpu.CompilerParams(dimension_semantics=sem, vmem_limit_bytes=VMEM_LIMIT_BYTES)


def _rms_mod(x, g, shift, scale):
    y = x * lax.rsqrt(jnp.mean(x * x, axis=-1, keepdims=True) + EPS)
    return (y * g) * (1.0 + scale) + shift


def _dot(a, b):
    return jnp.dot(a, b, preferred_element_type=F32)


def _dot_nt(a, b):
    return lax.dot_general(a, b, (((1,), (1,)), ((), ())), preferred_element_type=F32)


def _mod_spec(m, tm, rows_per_group):
    d = m.shape[-1]
    if m.shape[1] == 1:
        tpg = rows_per_group // tm
        return pl.BlockSpec((1, 1, d), lambda i, *_: (i // tpg, 0, 0))
    return pl.BlockSpec((1, tm, d), lambda i, *_: (i, 0, 0))


def _per_token(m, reps, tm):
    e = jnp.repeat(m, reps, axis=0)
    return e.reshape(e.shape[0] // tm, tm, e.shape[1])


def _ada_kernel(c_ref, w_ref, b_ref, o_ref):
    c = c_ref[...]
    s = (c * (1.0 / (1.0 + jnp.exp(-c)))).astype(BF16)
    o_ref[0] = _dot(s, w_ref[0].astype(BF16)) + b_ref[0]


def _ada_call(c, ada_w, ada_b):
    depth, d, n6 = ada_w.shape
    n = c.shape[0]
    tn = 1536
    return pl.pallas_call(
        _ada_kernel,
        grid=(depth, n6 // tn),
        in_specs=[pl.BlockSpec((n, d), lambda i, j: (0, 0)),
                  pl.BlockSpec((1, d, tn), lambda i, j: (i, 0, j)),
                  pl.BlockSpec((1, 1, tn), lambda i, j: (i, 0, j))],
        out_specs=pl.BlockSpec((1, n, tn), lambda i, j: (i, 0, j)),
        out_shape=jax.ShapeDtypeStruct((depth, n, n6), F32),
        compiler_params=_params("parallel", "parallel"),
        name="adaln",
    )(c, ada_w, ada_b.reshape(depth, 1, n6))


def _mlp_kernel(*refs, mixer_proj):
    if mixer_proj:
        x_ref, a_ref, wm_ref, gt1_ref = refs[:4]
        refs = refs[:1] + refs[4:]
    x_ref, g_ref, sh_ref, sc_ref, gt_ref, w1_ref, w2_ref, o_ref, h_scr, acc_scr = refs[:10]
    j = pl.program_id(1)

    @pl.when(j == 0)
    def _():
        x = x_ref[...]
        if mixer_proj:
            x = x + gt1_ref[0] * _dot(a_ref[...], wm_ref[...])
            refs[10][...] = x
        h_scr[...] = _rms_mod(x, g_ref[...], sh_ref[0], sc_ref[0]).astype(BF16)
        acc_scr[...] = jnp.zeros_like(acc_scr)

    a = _dot(h_scr[...], w1_ref[...])
    a = jnp.square(jnp.maximum(a, 0.0)).astype(BF16)
    acc_scr[...] += _dot(a, w2_ref[...])

    @pl.when(j == pl.num_programs(1) - 1)
    def _():
        x1 = refs[10][...] if mixer_proj else x_ref[...]
        o_ref[...] = x1 + gt_ref[0] * acc_scr[...]


def _mlp_call(x, g, sh, sc, gt, w1, w2, rows_per_group, mixer_proj=None):
    rows, d = x.shape
    f = w1.shape[1]
    tm = min(1024, rows)
    tf = 1024
    ms = _mod_spec(sh, tm, rows_per_group)
    in_specs = [pl.BlockSpec((tm, d), lambda i, j: (i, 0))]
    args = [x]
    scratch = [pltpu.VMEM((tm, d), BF16), pltpu.VMEM((tm, d), F32)]
    if mixer_proj is not None:
        a, w_mix, gt1 = mixer_proj
        in_specs += [pl.BlockSpec((tm, a.shape[1]), lambda i, j: (i, 0)),
                     pl.BlockSpec(w_mix.shape, lambda i, j: (0, 0)),
                     ms]
        args += [a, w_mix, gt1]
        scratch.append(pltpu.VMEM((tm, d), F32))
    in_specs += [pl.BlockSpec((1, d), lambda i, j: (0, 0)),
                 ms, ms, ms,
                 pl.BlockSpec((d, tf), lambda i, j: (0, j)),
                 pl.BlockSpec((tf, d), lambda i, j: (j, 0))]
    return pl.pallas_call(
        functools.partial(_mlp_kernel, mixer_proj=mixer_proj is not None),
        grid=(rows // tm, f // tf),
        in_specs=in_specs,
        out_specs=pl.BlockSpec((tm, d), lambda i, j: (i, 0)),
        out_shape=jax.ShapeDtypeStruct((rows, d), F32),
        scratch_shapes=scratch,
        compiler_params=_params("parallel", "arbitrary"),
        name="mlp",
    )(*args, g, sh, sc, gt, w1, w2)


def _conv_kernel(*refs, tm, d, tiles_per_seq, has_override):
    if has_override:
        x_ref, g_ref, sh_ref, sc_ref, w_ref, k_ref, ov_ref, bv_ref, cu_ref, e_scr = refs
    else:
        x_ref, g_ref, sh_ref, sc_ref, w_ref, k_ref, bv_ref, cu_ref, e_scr = refs
    i = pl.program_id(0)
    if has_override:
        e_scr[0:8, :] = jnp.zeros((8, d), F32)
    else:
        @pl.when(i % tiles_per_seq == 0)
        def _():
            e_scr[0:8, :] = jnp.zeros((8, d), F32)

        @pl.when(i % tiles_per_seq != 0)
        def _():
            e_scr[0:8, :] = e_scr[tm:tm + 8, :]
    n_part = 2 if tm % 256 == 0 else 1
    hr = tm // n_part
    gates = []
    for part in range(n_part):
        rs = slice(part * hr, (part + 1) * hr)
        sh, sc = sh_ref[0], sc_ref[0]
        sh = sh if sh.shape[0] == 1 else sh[rs]
        sc = sc if sc.shape[0] == 1 else sc[rs]
        proj = _dot(_rms_mod(x_ref[rs, :], g_ref[...], sh, sc).astype(BF16), w_ref[...])
        cu = proj[:, d:2 * d] * proj[:, 2 * d:]
        if has_override:
            row = lax.broadcasted_iota(jnp.int32, (hr, 1), 0)
            cu = jnp.where((row % CONV_GRP) >= CONV_GRP // 2, cu, ov_ref[rs, :])
        e_scr[8 + part * hr:8 + (part + 1) * hr, :] = cu
        cu_ref[rs, :] = cu
        gates.append(proj[:, :d])
    k = k_ref[...]
    for part in range(n_part):
        lo = part * hr
        v = (k[0:1, :] * e_scr[6 + lo:6 + lo + hr, :] + k[1:2, :] * e_scr[7 + lo:7 + lo + hr, :]
             + k[2:3, :] * e_scr[8 + lo:8 + lo + hr, :])
        bv_ref[lo:lo + hr, :] = (gates[part] * v).astype(BF16)


def _conv_call(x, g, sh, sc, w_in, k, override, rows_per_group):
    rows, d = x.shape
    tm = min(512, rows)
    has_override = override is not None
    ms = _mod_spec(sh, tm, rows_per_group)
    in_specs = [pl.BlockSpec((tm, d), lambda i: (i, 0)),
                pl.BlockSpec((1, d), lambda i: (0, 0)),
                ms, ms,
                pl.BlockSpec((d, 3 * d), lambda i: (0, 0)),
                pl.BlockSpec((CONV_WIDTH, d), lambda i: (0, 0))]
    args = [x, g, sh, sc, w_in, k]
    if has_override:
        in_specs.append(pl.BlockSpec((tm, d), lambda i: (i, 0)))
        args.append(override)
    kern = functools.partial(_conv_kernel, tm=tm, d=d, tiles_per_seq=max(rows_per_group // tm, 1),
                             has_override=has_override)
    return pl.pallas_call(
        kern,
        grid=(rows // tm,),
        in_specs=in_specs,
        out_specs=[pl.BlockSpec((tm, d), lambda i: (i, 0)), pl.BlockSpec((tm, d), lambda i: (i, 0))],
        out_shape=[jax.ShapeDtypeStruct((rows, d), BF16), jax.ShapeDtypeStruct((rows, d), F32)],
        scratch_shapes=[pltpu.VMEM((tm + 8, d), F32)],
        compiler_params=_params("arbitrary"),
        name="conv_in",
    )(*args)


def _pool_kernel(*refs, tm, d, tiles_per_seq, sample, pos0):
    x_ref, g_ref, sh_ref, sc_ref, gt_ref, w_ref, ps_ref, o_ref, st_ref, e_scr = refs
    i = pl.program_id(0)
    x = x_ref[...]
    h = _rms_mod(x, g_ref[...], sh_ref[0], sc_ref[0])
    row = lax.broadcasted_iota(jnp.int32, (tm, 1), 0)
    if sample:
        local = (row % POOL_GRP) - POOL_GRP // 2
        h = jnp.where(local >= 0, h, x)
        e_scr[0:16, :] = jnp.zeros((16, d), F32)
    else:
        local = (i % tiles_per_seq) * tm + row

        @pl.when(i % tiles_per_seq == 0)
        def _():
            e_scr[0:16, :] = jnp.zeros((16, d), F32)

        @pl.when(i % tiles_per_seq != 0)
        def _():
            e_scr[0:16, :] = e_scr[tm:tm + 16, :]
    e_scr[16:tm + 16, :] = h
    if sample:
        st_ref[...] = h
    else:
        st_ref[0] = e_scr[tm:tm + 16, :]
    gq = d // len(POOL_WINDOWS)
    for gi, w in enumerate(POOL_WINDOWS):
        lo = gi * gq
        win = h[:, lo:lo + gq]
        for j in range(1, w):
            win = win + e_scr[16 - j:16 - j + tm, lo:lo + gq]
        cnt = jnp.clip(pos0 + local + 1, 1, w).astype(F32)
        dd = (win / cnt - h[:, lo:lo + gq]).astype(BF16)
        y = _dot(dd, w_ref[gi]) * ps_ref[:, lo:lo + gq]
        o_ref[:, lo:lo + gq] = x[:, lo:lo + gq] + gt_ref[0][:, lo:lo + gq] * y


def _pool_call(x, g, sh, sc, gt, w, ps, rows_per_group, sample, pos0):
    rows, d = x.shape
    tm = min(512, rows)
    ms = _mod_spec(sh, tm, rows_per_group)
    gq = d // len(POOL_WINDOWS)
    if sample:
        st_spec = pl.BlockSpec((tm, d), lambda i: (i, 0))
        st_shape = jax.ShapeDtypeStruct((rows, d), F32)
    else:
        st_spec = pl.BlockSpec((1, 16, d), lambda i: (i, 0, 0))
        st_shape = jax.ShapeDtypeStruct((rows // tm, 16, d), F32)
    kern = functools.partial(_pool_kernel, tm=tm, d=d, tiles_per_seq=max(rows_per_group // tm, 1),
                             sample=sample, pos0=pos0)
    return pl.pallas_call(
        kern,
        grid=(rows // tm,),
        in_specs=[pl.BlockSpec((tm, d), lambda i: (i, 0)),
                  pl.BlockSpec((1, d), lambda i: (0, 0)),
                  ms, ms, ms,
                  pl.BlockSpec((len(POOL_WINDOWS), gq, gq), lambda i: (0, 0, 0)),
                  pl.BlockSpec((1, d), lambda i: (0, 0))],
        out_specs=[pl.BlockSpec((tm, d), lambda i: (i, 0)), st_spec],
        out_shape=[jax.ShapeDtypeStruct((rows, d), F32), st_shape],
        scratch_shapes=[pltpu.VMEM((tm + 16, d), F32)],
        compiler_params=_params("arbitrary"),
        name="pool",
    )(x, g, sh, sc, gt, w, ps)


def _swa_qkv_kernel(*refs, tm, dil, keeps, tps):
    x_ref, g_ref, sh_ref, sc_ref, w_ref, gn_ref, seg_ref = refs[:7]
    outs = refs[7:16]
    states = refs[16:16 + len(keeps)]
    z_scr, h_scr = refs[16 + len(keeps):]
    w = SWA_WIDTH
    tile = pl.program_id(0) % tps
    h_scr[...] = _rms_mod(x_ref[...], g_ref[...], sh_ref[0], sc_ref[0]).astype(BF16)
    for jb in range(9):
        which, g = divmod(jb, 3)
        z = _dot(h_scr[...], w_ref[:, jb * w:(jb + 1) * w])
        if which < 2:
            ms = _dot((z * z).astype(BF16), seg_ref[...])
            z = z * lax.rsqrt(ms + EPS) * gn_ref[jb:jb + 1, :]
        if keeps and which > 0:
            keep = keeps[g]
            zk = z if keep >= tm else z[tm - keep:, :]

            def put(st_ref=states[g], kv=which - 1, zk=zk):
                st_ref[0, kv] = zk.T

            first_kept = tps - max(keep // tm, 1)
            if first_kept == 0:
                put()
            else:
                pl.when(tile >= first_kept)(put)
        r = dil[g]
        if r == 1:
            outs[jb][0, 0] = z
        else:
            for c in range(w // 128):
                z_scr[c] = z[:, c * 128:(c + 1) * 128]
            for rho in range(r):
                for c in range(w // 128):
                    outs[jb][0, rho, :, c * 128:(c + 1) * 128] = z_scr[c, pl.ds(rho, tm // r, stride=r), :]


def _swa_qkv_call(x, g, sh, sc, w, gn, seg, n_seq, seq_len, dil, rows_per_group, keeps=()):
    rows, d = x.shape
    wd = SWA_WIDTH
    tm = min(512, seq_len)
    tps = seq_len // tm
    ms = _mod_spec(sh, tm, rows_per_group)
    out_specs, out_shape = [], []
    for jb in range(9):
        r = dil[jb % 3]
        out_specs.append(pl.BlockSpec((1, r, tm // r, wd), lambda i: (i // tps, 0, i % tps, 0)))
        out_shape.append(jax.ShapeDtypeStruct((n_seq, r, seq_len // r, wd), F32))
    for keep in keeps:
        assert keep % tm == 0 or tm % keep == 0
        lanes = min(keep, tm)
        first_kept = tps - max(keep // tm, 1)
        out_specs.append(pl.BlockSpec(
            (1, 2, wd, lanes),
            lambda i, first_kept=first_kept: (i // tps, 0, 0, jnp.maximum(i % tps - first_kept, 0))))
        out_shape.append(jax.ShapeDtypeStruct((n_seq, 2, wd, keep), F32))
    return pl.pallas_call(
        functools.partial(_swa_qkv_kernel, tm=tm, dil=dil, keeps=tuple(keeps), tps=tps),
        grid=(rows // tm,),
        in_specs=[pl.BlockSpec((tm, d), lambda i: (i, 0)),
                  pl.BlockSpec((1, d), lambda i: (0, 0)),
                  ms, ms,
                  pl.BlockSpec(w.shape, lambda i: (0, 0), pipeline_mode=pl.Buffered(1)),
                  pl.BlockSpec(gn.shape, lambda i: (0, 0)),
                  pl.BlockSpec(seg.shape, lambda i: (0, 0))],
        out_specs=out_specs,
        out_shape=out_shape,
        scratch_shapes=[pltpu.VMEM((wd // 128, tm, 128), F32), pltpu.VMEM((tm, d), BF16)],
        compiler_params=_params("arbitrary"),
        name="swa_qkv",
    )(x, g, sh, sc, w, gn, seg)


def _swa_band_kernel(q_ref, kc_ref, kp_ref, vc_ref, vp_ref, o_ref, l_ref):
    b = pl.program_id(2)
    blk = SWA_BLK
    hd = SWA_HEAD_DIM
    key = lax.broadcasted_iota(jnp.int32, (2 * blk, blk), 0)
    qi = lax.broadcasted_iota(jnp.int32, (2 * blk, blk), 1)
    dist = qi + blk - key
    ok = jnp.logical_and(dist >= 0, dist <= jnp.where(b > 0, blk, qi))
    lane_head = lax.broadcasted_iota(jnp.int32, (blk, 2 * hd), 1) // hd
    for i in range(q_ref.shape[0]):
        q = q_ref[i, 0]
        k = jnp.concatenate([kp_ref[i, 0], kc_ref[i, 0]], axis=0).astype(BF16)
        vt = jnp.concatenate([vp_ref[i, 0], vc_ref[i, 0]], axis=0).T.astype(BF16)
        ots, lts = [], []
        for h in range(SWA_HEADS):
            slab = slice((h // 2) * 2 * hd, (h // 2 + 1) * 2 * hd)
            qm = jnp.where(lane_head == h % 2, q[:, slab], 0.0).astype(BF16)
            st = jnp.where(ok, _dot_nt(k[:, slab], qm) * SWA_SCALE, NEG)
            m = jnp.max(st, axis=0, keepdims=True)
            p = jnp.exp(st - m)
            l = jnp.sum(p, axis=0, keepdims=True)
            ots.append(_dot(vt[h * hd:(h + 1) * hd, :], p.astype(BF16)) * (1.0 / l))
            lts.append(jnp.broadcast_to(m + jnp.log(l), (hd, blk)))
        o_ref[i] = jnp.concatenate(ots, axis=0).T.astype(o_ref.dtype)
        l_ref[i] = jnp.concatenate(lts, axis=0).T


def _swa_band_call(q, k, v):
    n_seq, r, lr, w = q.shape
    nb = lr // SWA_BLK
    ns = next(c for c in (8, 4, 2, 1) if n_seq % c == 0)
    blk = (ns, 1, SWA_BLK, w)
    cur = pl.BlockSpec(blk, lambda n, rho, b: (n, rho, b, 0))
    prev = pl.BlockSpec(blk, lambda n, rho, b: (n, rho, jnp.maximum(b - 1, 0), 0))
    out_spec = pl.BlockSpec((ns, SWA_BLK, w), lambda n, rho, b: (n, b, rho))
    o, l = pl.pallas_call(
        _swa_band_kernel,
        grid=(n_seq // ns, r, nb),
        in_specs=[cur, cur, prev, cur, prev],
        out_specs=[out_spec, out_spec],
        out_shape=[jax.ShapeDtypeStruct((n_seq, lr, r * w), BF16), jax.ShapeDtypeStruct((n_seq, lr, r * w), F32)],
        compiler_params=_params("parallel", "parallel", "arbitrary"),
        name="swa_band",
    )(q, k, k, v, v)
    return o, l


def _swa_merge_kernel(*refs, dil, tm):
    o_refs, l_refs = refs[0:3], refs[3:6]
    w_ref, x_ref, gt_ref, out_ref = refs[6:10]
    scrs = list(refs[10:])
    wd = SWA_WIDTH

    def token_order(ref, r):
        if r == 1:
            return ref[0].astype(F32)
        scr = scrs.pop()
        for rho in range(r):
            for c in range(wd // 128):
                lo = rho * wd + c * 128
                scr[c, pl.ds(rho, tm // r, stride=r), :] = ref[0, :, lo:lo + 128].astype(F32)
        return jnp.concatenate([scr[c] for c in range(wd // 128)], axis=1)

    a = [token_order(l_refs[g], dil[g]) for g in range(3)]
    o = [token_order(o_refs[g], dil[g]) for g in range(3)]
    m = jnp.maximum(jnp.maximum(a[0], a[1]), a[2])
    e0, e1, e2 = jnp.exp(a[0] - m), jnp.exp(a[1] - m), jnp.exp(a[2] - m)
    inv = 1.0 / (e0 + e1 + e2)
    merged = (e0 * inv) * o[0] + (e1 * inv) * o[1] + (e2 * inv) * o[2]
    out_ref[...] = x_ref[...] + gt_ref[0] * _dot(merged.astype(BF16), w_ref[...])


def _swa_merge_call(os_, ls_, dil, w_o, x, gt, seq_len, rows_per_group):
    rows, d = x.shape
    tm = min(512, seq_len)
    tps = seq_len // tm
    w = SWA_WIDTH
    specs = [pl.BlockSpec((1, tm // r, r * w), lambda i: (i // tps, i % tps, 0)) for r in dil]
    n_scr = 2 * sum(1 for r in dil if r > 1)
    return pl.pallas_call(
        functools.partial(_swa_merge_kernel, dil=dil, tm=tm),
        grid=(rows // tm,),
        in_specs=specs + specs + [pl.BlockSpec((w, d), lambda i: (0, 0)),
                                  pl.BlockSpec((tm, d), lambda i: (i, 0)),
                                  _mod_spec(gt, tm, rows_per_group)],
        out_specs=pl.BlockSpec((tm, d), lambda i: (i, 0)),
        out_shape=jax.ShapeDtypeStruct((rows, d), F32),
        scratch_shapes=[pltpu.VMEM((w // 128, tm, 128), F32)] * n_scr,
        compiler_params=_params("parallel"),
        name="swa_merge",
    )(*os_, *ls_, w_o, x, gt)


def _swa_step_kernel(q_ref, st_ref, new_ref, ns_ref, o_ref, l_ref, *, win, r, lb, s_new, n_blk):
    w = SWA_WIDTH
    hd = SWA_HEAD_DIM
    nrow = SWA_HEADS * s_new
    pad = 128 - s_new
    rowi = lax.broadcasted_iota(jnp.int32, (nrow, w), 0)
    lanei = lax.broadcasted_iota(jnp.int32, (nrow, w), 1)
    diag = (rowi // s_new) == (lanei // hd)
    s_idx = lax.broadcasted_iota(jnp.int32, (nrow, lb + 128), 0) % s_new
    col = lax.broadcasted_iota(jnp.int32, (nrow, lb + 128), 1)
    dist = lb + s_idx - jnp.where(col < lb, col, col - pad)
    ok = jnp.logical_and(jnp.logical_and(dist >= 0, dist % r == 0), dist <= win)
    ok = jnp.logical_and(ok, jnp.logical_or(col < lb, col >= lb + pad))
    per_slab = 128 // s_new
    for i in range(n_blk):
        if n_blk % per_slab == 0:
            slab, shift = i // per_slab, pad - (i % per_slab) * s_new
        else:
            slab, shift = 0, pad - ((pl.program_id(0) * n_blk + i) % per_slab) * s_new
        new_k = pltpu.roll(new_ref[slab, 0], shift, axis=1)
        new_v = pltpu.roll(new_ref[slab, 1], shift, axis=1)
        q = q_ref[i]
        qbd = jnp.where(diag, jnp.concatenate([q] * SWA_HEADS, axis=0), 0.0).astype(BF16)
        kt = jnp.concatenate([st_ref[i, 0], new_k], axis=1).astype(BF16)
        vt = jnp.concatenate([st_ref[i, 1], new_v], axis=1).astype(BF16)
        sc = jnp.where(ok, _dot(qbd, kt) * SWA_SCALE, NEG)
        m = jnp.max(sc, axis=-1, keepdims=True)
        p = jnp.exp(sc - m)
        l = jnp.sum(p, axis=-1, keepdims=True)
        acc = jnp.where(diag, _dot_nt(p.astype(BF16), vt) / l, 0.0)
        lse = jnp.where(diag, m + jnp.log(l), 0.0)
        o = acc[0:s_new]
        ls = lse[0:s_new]
        for h in range(1, SWA_HEADS):
            o = o + acc[h * s_new:(h + 1) * s_new]
            ls = ls + lse[h * s_new:(h + 1) * s_new]
        o_ref[i] = o
        l_ref[i] = ls
        for kv, new in enumerate((new_k, new_v)):
            ns_ref[i, kv, :, 0:lb - s_new] = st_ref[i, kv, :, s_new:lb]
            ns_ref[i, kv, :, lb - s_new:lb] = new[:, pad:128]


def _swa_step_call(q, st_t, new_t, win, r):
    n_seq, s_new, w = q.shape
    lb = st_t.shape[-1]
    n_blk = max(1, min(n_seq, SWA_GROUPS[-1][0] // lb))
    per_slab = 128 // s_new
    assert n_seq % n_blk == 0 and (n_blk % per_slab == 0 or per_slab % n_blk == 0)
    if n_blk % per_slab == 0:
        new_spec = pl.BlockSpec((n_blk // per_slab, 2, w, 128), lambda n: (n, 0, 0, 0))
    else:
        new_spec = pl.BlockSpec((1, 2, w, 128), lambda n: ((n * n_blk) // per_slab, 0, 0, 0))
    return pl.pallas_call(
        functools.partial(_swa_step_kernel, win=win, r=r, lb=lb, s_new=s_new, n_blk=n_blk),
        grid=(n_seq // n_blk,),
        in_specs=[pl.BlockSpec((n_blk, s_new, w), lambda n: (n, 0, 0)),
                  pl.BlockSpec((n_blk, 2, w, lb), lambda n: (n, 0, 0, 0)),
                  new_spec],
        out_specs=[pl.BlockSpec((n_blk, 2, w, lb), lambda n: (n, 0, 0, 0)),
                   pl.BlockSpec((n_blk, s_new, w), lambda n: (n, 0, 0)),
                   pl.BlockSpec((n_blk, s_new, w), lambda n: (n, 0, 0))],
        out_shape=[jax.ShapeDtypeStruct((n_seq, 2, w, lb), F32),
                   jax.ShapeDtypeStruct((n_seq, s_new, w), F32),
                   jax.ShapeDtypeStruct((n_seq, s_new, w), F32)],
        compiler_params=_params("parallel"),
        name="swa_step",
    )(q, st_t, new_t)


def _rope128(bn, cos_t, sin_t):
    return bn * cos_t + pltpu.roll(bn, 64, axis=1) * sin_t


def _mla_proj_kernel(*refs, decode):
    (x_ref, g_ref, sh_ref, sc_ref, w1_ref, gqa_ref, wqb_ref, gq_ref, gkv_ref, gkr_ref,
     cos_ref, sin_ref, wx_ref) = refs[:13]
    if decode:
        outs = refs[13:]
    else:
        wvt_ref = refs[13]
        outs = refs[14:]
    nh = MLA_HEADS
    if decode:
        lat_ref, qabs_ref, qpe_ref = outs
    else:
        lat_ref, qcat_ref, kcat_ref, vt_ref = outs
    gq = gq_ref[...]
    tm = x_ref.shape[0]
    n_part = 2 if tm % 256 == 0 else 1

    def rows_of(ref, rs):
        v = ref[0]
        return v if v.shape[0] == 1 else v[rs]

    for part in range(n_part):
        rs = slice(part * (tm // n_part), (part + 1) * (tm // n_part))
        h = _rms_mod(x_ref[rs, :], g_ref[...], rows_of(sh_ref, rs), rows_of(sc_ref, rs)).astype(BF16)
        a = _dot(h, w1_ref[...])
        qa = a[:, :Q_LORA]
        qa = (qa * lax.rsqrt(jnp.mean(qa * qa, axis=-1, keepdims=True) + EPS) * gqa_ref[...]).astype(BF16)
        q = _dot(qa, wqb_ref[...])
        cos_t = cos_ref[rs, :]
        sin_t = sin_ref[rs, :]
        latc = a[:, Q_LORA:Q_LORA + KV_LORA]
        latc = latc * lax.rsqrt(jnp.mean(latc * latc, axis=-1, keepdims=True) + EPS) * gkv_ref[...]
        kb = a[:, Q_LORA + KV_LORA:]
        kb = kb * lax.rsqrt(jnp.sum(kb * kb, axis=-1, keepdims=True) * (1.0 / QK_ROPE) + EPS) * gkr_ref[...]
        kb = _rope128(kb, cos_t, sin_t)
        lane = lax.broadcasted_iota(jnp.int32, kb.shape, 1)
        kstd = jnp.where(lane < 32, kb, pltpu.roll(kb, 96, axis=1))
        lat_ref[rs, 0:KV_LORA] = latc
        lat_ref[rs, KV_LORA:KV_LORA + QK_ROPE] = kstd[:, 0:QK_ROPE]
        for hh in range(nh):
            an = q[:, hh * 256:hh * 256 + 128]
            an = an * lax.rsqrt(jnp.mean(an * an, axis=-1, keepdims=True) + EPS) * gq[:, 0:128]
            bn = q[:, hh * 256 + 128:hh * 256 + 256]
            bn = bn * lax.rsqrt(jnp.sum(bn * bn, axis=-1, keepdims=True) * (1.0 / QK_ROPE) + EPS) * gq[:, 128:256]
            bn = _rope128(bn, cos_t, sin_t)
            if decode:
                ag = an * gq[:, 256:384]
                a_hi = ag.astype(BF16)
                a_lo = (ag - a_hi.astype(F32)).astype(BF16)
                qabs_ref[rs, hh * 256:(hh + 1) * 256] = (_dot(a_hi, wx_ref[hh]) + _dot(a_lo, wx_ref[hh])).astype(BF16)
                qpe_ref[rs, hh * 128:(hh + 1) * 128] = jnp.where(lane < 32, bn, pltpu.roll(bn, 96, axis=1)).astype(BF16)
            else:
                qcat_ref[rs, hh * 256:hh * 256 + 128] = an.astype(BF16)
                qcat_ref[rs, hh * 256 + 128:(hh + 1) * 256] = bn.astype(BF16)
        if not decode:
            gkn = gq_ref[:, 256:384]
            latb = latc.astype(BF16)
            kvx = _dot(latb, wx_ref[...])
            kbb = kb.astype(BF16)
            for hh in range(nh):
                kn = kvx[:, hh * 128:(hh + 1) * 128]
                kn = kn * lax.rsqrt(jnp.mean(kn * kn, axis=-1, keepdims=True) + EPS) * gkn
                kcat_ref[rs, hh * 256:hh * 256 + 128] = kn.astype(BF16)
                kcat_ref[rs, hh * 256 + 128:(hh + 1) * 256] = kbb
            vt_ref[0, :, rs] = _dot_nt(wvt_ref[...], latb).astype(BF16)


def _mla_proj_call(x, g, sh, sc, wts, cos_t, sin_t, rows_per_group, decode):
    rows, d = x.shape
    tm = min(512, rows)
    nh = MLA_HEADS
    ms = _mod_spec(sh, tm, rows_per_group)
    tps = max(rows_per_group // tm, 1)
    if decode:
        tab_spec = pl.BlockSpec((tm, 128), lambda i: (0, 0))
        wx = wts["wk_t"]
        wx_spec = pl.BlockSpec(wx.shape, lambda i: (0, 0, 0))
        out_specs = [pl.BlockSpec((tm, KV_LORA + QK_ROPE), lambda i: (i, 0)),
                     pl.BlockSpec((tm, nh * 256), lambda i: (i, 0)),
                     pl.BlockSpec((tm, nh * 128), lambda i: (i, 0))]
        out_shape = [jax.ShapeDtypeStruct((rows, KV_LORA + QK_ROPE), F32),
                     jax.ShapeDtypeStruct((rows, nh * 256), BF16),
                     jax.ShapeDtypeStruct((rows, nh * 128), BF16)]
    else:
        tab_spec = pl.BlockSpec((tm, 128), lambda i: (i % tps, 0))
        wx = wts["w_kn"]
        wx_spec = pl.BlockSpec(wx.shape, lambda i: (0, 0))
        out_specs = [pl.BlockSpec((tm, KV_LORA + QK_ROPE), lambda i: (i, 0)),
                     pl.BlockSpec((tm, nh * 256), lambda i: (i, 0)),
                     pl.BlockSpec((tm, nh * 256), lambda i: (i, 0)),
                     pl.BlockSpec((1, nh * V_DIM, tm), lambda i: (i // tps, 0, i % tps))]
        out_shape = [jax.ShapeDtypeStruct((rows, KV_LORA + QK_ROPE), F32),
                     jax.ShapeDtypeStruct((rows, nh * 256), BF16),
                     jax.ShapeDtypeStruct((rows, nh * 256), BF16),
                     jax.ShapeDtypeStruct((rows // rows_per_group, nh * V_DIM, rows_per_group), BF16)]

    def full(a):
        return pl.BlockSpec(a.shape, lambda i: (0,) * a.ndim)

    extra = [] if decode else [wts["wv_t"]]
    return pl.pallas_call(
        functools.partial(_mla_proj_kernel, decode=decode),
        grid=(rows // tm,),
        in_specs=[pl.BlockSpec((tm, d), lambda i: (i, 0)),
                  pl.BlockSpec((1, d), lambda i: (0, 0)),
                  ms, ms,
                  full(wts["w1"]), full(wts["g_qa"]), full(wts["w_qb"]), full(wts["gq"]),
                  full(wts["g_kva"]), full(wts["g_kr"]),
                  tab_spec, tab_spec, wx_spec] + [full(a) for a in extra],
        out_specs=out_specs,
        out_shape=out_shape,
        compiler_params=_params("parallel"),
        name="mla_proj_decode" if decode else "mla_proj",
    )(x, g, sh, sc, wts["w1"], wts["g_qa"], wts["w_qb"], wts["gq"], wts["g_kva"], wts["g_kr"],
      cos_t, sin_t, wx, *extra)


def _mla_causal_kernel(q_ref, k_ref, vt_ref, o_ref, *, seq_len, tq, heads):
    r = lax.broadcasted_iota(jnp.int32, (tq, tq), 0)
    c = lax.broadcasted_iota(jnp.int32, (tq, tq), 1)
    ex = MLA_SCALE * LOG2_E
    for qb in range(seq_len // tq):
        lo = qb * tq
        for h in range(heads):
            hq = slice(h * 256, (h + 1) * 256)
            hv = slice(h * V_DIM, (h + 1) * V_DIM)
            q = q_ref[0, lo:lo + tq, hq]
            s_d = jnp.where(r <= c, _dot_nt(k_ref[0, lo:lo + tq, hq], q), NEG)
            m = jnp.max(s_d, axis=0, keepdims=True)
            if qb > 0:
                s_p = _dot_nt(k_ref[0, 0:lo, hq], q)
                m = jnp.maximum(m, jnp.max(s_p, axis=0, keepdims=True))
            p_d = jnp.exp2((s_d - m) * ex)
            l = jnp.sum(p_d, axis=0, keepdims=True)
            ot = _dot(vt_ref[0, hv, lo:lo + tq], p_d.astype(BF16))
            if qb > 0:
                p_p = jnp.exp2((s_p - m) * ex)
                l = l + jnp.sum(p_p, axis=0, keepdims=True)
                ot = ot + _dot(vt_ref[0, hv, 0:lo], p_p.astype(BF16))
            o_ref[0, lo:lo + tq, hv] = (ot * (1.0 / l)).T.astype(BF16)


def _mla_causal_call(qcat, kcat, vt, n_seq, seq_len):
    nh = MLA_HEADS
    tq = min(512, seq_len)
    hps = 4
    q3 = qcat.reshape(n_seq, seq_len, nh * 256)
    k3 = kcat.reshape(n_seq, seq_len, nh * 256)
    o = pl.pallas_call(
        functools.partial(_mla_causal_kernel, seq_len=seq_len, tq=tq, heads=hps),
        grid=(n_seq, nh // hps),
        in_specs=[pl.BlockSpec((1, seq_len, hps * 256), lambda n, h: (n, 0, h)),
                  pl.BlockSpec((1, seq_len, hps * 256), lambda n, h: (n, 0, h)),
                  pl.BlockSpec((1, hps * V_DIM, seq_len), lambda n, h: (n, h, 0))],
        out_specs=pl.BlockSpec((1, seq_len, hps * V_DIM), lambda n, h: (n, 0, h)),
        out_shape=jax.ShapeDtypeStruct((n_seq, seq_len, nh * V_DIM), BF16),
        compiler_params=_params("parallel", "parallel"),
        name="mla_causal",
    )(q3, k3, vt)
    return o.reshape(n_seq * seq_len, nh * V_DIM)


def _mla_decode_kernel(pt_ref, *refs, n_pg, s_new):
    del pt_ref
    page_refs = refs[:n_pg]
    (qabs_ref, qpe_ref, latn_ref, wkt_ref, wv_ref, o_ref, m_scr, l_scr, ctx_scr) = refs[n_pg:]
    c = pl.program_id(1)
    nh = MLA_HEADS
    nrow = nh * s_new

    @pl.when(c == 0)
    def _():
        m_scr[...] = jnp.full_like(m_scr, NEG)
        l_scr[...] = jnp.zeros_like(l_scr)
        ctx_scr[...] = jnp.zeros_like(ctx_scr)

    def process(pages, mask):
        latc = jnp.concatenate([pg[0:KV_LORA, :].astype(BF16) for pg in pages], axis=1)
        kpe = jnp.concatenate([pg[KV_LORA:KV_LORA + QK_ROPE, :].astype(BF16) for pg in pages], axis=1)
        r = _dot(wkt_ref[...], latc)
        rq = _dot(qabs_ref[0], latc)
        spe = _dot(qpe_ref[0][:, 0:QK_ROPE], kpe)
        parts = []
        for h in range(nh):
            blk = r[h * QK_NOPE:(h + 1) * QK_NOPE]
            rstd = lax.rsqrt(jnp.sum(blk * blk, axis=0, keepdims=True) * (1.0 / QK_NOPE) + EPS)
            parts.append(rq[h * s_new:(h + 1) * s_new] * rstd)
        s = (jnp.concatenate(parts, axis=0) + spe) * MLA_SCALE
        if mask is not None:
            s = jnp.where(mask, s, NEG)
        m_old = m_scr[...]
        m_new = jnp.maximum(m_old, jnp.max(s, axis=-1, keepdims=True))
        alpha = jnp.exp(m_old - m_new)
        p = jnp.exp(s - m_new)
        l_scr[...] = alpha * l_scr[...] + jnp.sum(p, axis=-1, keepdims=True)
        ctx_scr[...] = alpha * ctx_scr[...] + _dot_nt(p.astype(BF16), latc)
        m_scr[...] = m_new

    for lo in range(0, n_pg, 16):
        process([pr[0, 0] for pr in page_refs[lo:lo + 16]], None)

    @pl.when(c == pl.num_programs(1) - 1)
    def _():
        si = lax.broadcasted_iota(jnp.int32, (nrow, PAGE_SIZE), 0) % s_new
        kj = lax.broadcasted_iota(jnp.int32, (nrow, PAGE_SIZE), 1)
        process([latn_ref[0]], kj <= si)
        ctx = ctx_scr[...] / l_scr[...]
        for h in range(nh):
            ctx_h = ctx[h * s_new:(h + 1) * s_new].astype(BF16)
            o_ref[0, :, h * V_DIM:(h + 1) * V_DIM] = _dot(ctx_h, wv_ref[h]).astype(BF16)


def _mla_decode_call(cache_t, layer, page_table, qabs, qpe, lat_new, wkt, wv, n_seq, s_new):
    nh = MLA_HEADS
    n_pages = page_table.shape[1]
    n_pg = min(64, n_pages)
    lw = KV_LORA + QK_ROPE
    nrow = nh * s_new

    def page_spec(p):
        return pl.BlockSpec((1, 1, lw, PAGE_SIZE),
                            lambda n, c, pt: (layer, pt[n * n_pages + c * n_pg + p], 0, 0))

    latn_t = jnp.transpose(lat_new.reshape(n_seq, s_new, lw), (0, 2, 1))
    latn_t = jnp.pad(latn_t, ((0, 0), (0, 0), (0, PAGE_SIZE - s_new)))
    grid_spec = pltpu.PrefetchScalarGridSpec(
        num_scalar_prefetch=1,
        grid=(n_seq, n_pages // n_pg),
        in_specs=[page_spec(p) for p in range(n_pg)] + [
            pl.BlockSpec((1, nrow, 256), lambda n, c, pt: (n, 0, 0)),
            pl.BlockSpec((1, nrow, 128), lambda n, c, pt: (n, 0, 0)),
            pl.BlockSpec((1, lw, PAGE_SIZE), lambda n, c, pt: (n, 0, 0)),
            pl.BlockSpec(wkt.shape, lambda n, c, pt: (0, 0)),
            pl.BlockSpec(wv.shape, lambda n, c, pt: (0, 0, 0))],
        out_specs=pl.BlockSpec((1, s_new, nh * V_DIM), lambda n, c, pt: (n, 0, 0)),
        scratch_shapes=[pltpu.VMEM((nrow, 1), F32), pltpu.VMEM((nrow, 1), F32),
                        pltpu.VMEM((nrow, KV_LORA), F32)])
    o = pl.pallas_call(
        functools.partial(_mla_decode_kernel, n_pg=n_pg, s_new=s_new),
        grid_spec=grid_spec,
        out_shape=jax.ShapeDtypeStruct((n_seq, s_new, nh * V_DIM), BF16),
        compiler_params=_params("parallel", "arbitrary"),
        name="mla_decode",
    )(page_table.reshape(-1), *([cache_t] * n_pg),
      jnp.transpose(qabs.reshape(n_seq, s_new, nh, 256), (0, 2, 1, 3)).reshape(n_seq, nrow, 256),
      jnp.transpose(qpe.reshape(n_seq, s_new, nh, 128), (0, 2, 1, 3)).reshape(n_seq, nrow, 128),
      latn_t, wkt, wv)
    return o.reshape(n_seq * s_new, nh * V_DIM)


def _mla_weights(w_qa, g_qa, w_qb, w_kva, g_kva, w_kvb, g_qn, g_kn, g_qr, g_kr):
    nh = MLA_HEADS
    half = QK_ROPE // 2

    def pad_rope_cols(w):
        z = jnp.zeros(w.shape[:-1] + (half,), w.dtype)
        return jnp.concatenate([w[..., :half], z, w[..., half:], z], axis=-1)

    w_kva_p = jnp.concatenate([w_kva[:, :KV_LORA], pad_rope_cols(w_kva[:, KV_LORA:])], axis=1)
    w1 = jnp.concatenate([w_qa, w_kva_p], axis=1).astype(BF16)
    wq = w_qb.reshape(Q_LORA, nh, QK_NOPE + QK_ROPE)
    wq = jnp.concatenate([wq[..., :QK_NOPE], pad_rope_cols(wq[..., QK_NOPE:])], axis=-1)
    w_qb_p = wq.reshape(Q_LORA, nh * 256).astype(BF16)
    gq = jnp.concatenate([g_qn, pad_rope_cols(g_qr), g_kn])[None, :]
    wkv = w_kvb.reshape(KV_LORA, nh, QK_NOPE + V_DIM)
    w_kn = wkv[..., :QK_NOPE].reshape(KV_LORA, nh * QK_NOPE).astype(BF16)
    wv_t = jnp.transpose(wkv[..., QK_NOPE:], (1, 2, 0)).reshape(nh * V_DIM, KV_LORA).astype(BF16)
    wk_t = jnp.transpose(wkv[..., :QK_NOPE], (1, 2, 0)).astype(BF16)
    return dict(w1=w1, g_qa=g_qa[None, :], w_qb=w_qb_p, gq=gq, g_kva=g_kva[None, :],
                g_kr=pad_rope_cols(g_kr)[None, :], w_kn=w_kn, wv_t=wv_t, wk_t=wk_t,
                wkt_flat=wk_t.reshape(nh * QK_NOPE, KV_LORA),
                wv=jnp.transpose(wkv[..., QK_NOPE:], (1, 0, 2)).astype(BF16))


def _rope_tables(pos):
    half = QK_ROPE // 2
    inv = ROPE_BASE ** (-jnp.arange(half, dtype=F32) / half)
    ang = pos.astype(F32)[:, None] * inv[None, :]
    cos, sin = jnp.cos(ang), jnp.sin(ang)
    z = jnp.zeros_like(cos)
    return (jnp.concatenate([cos, z, cos, z], axis=1), jnp.concatenate([-sin, z, sin, z], axis=1))


def kernel(x_prompt, x_sample, state_conv, state_swa0, state_swa1, state_swa2, state_pool, cache_mla, page_table,
           c_prompt, c_sample, ada_w, ada_b, norm1_g, norm2_g, mlp_w1, mlp_w2, conv_w_in, conv_k, conv_w_out,
           swa_w_qkv, swa_qn_g, swa_kn_g, swa_w_o, pool_w, pool_scale, mla_w_qa, mla_g_qa, mla_w_qb, mla_w_kva,
           mla_g_kva, mla_w_kvb, mla_g_qn, mla_g_kn, mla_g_qr, mla_g_kr, mla_w_o):
    bp, t, d = x_prompt.shape
    bs, s, _ = x_sample.shape
    depth = ada_w.shape[0]
    past = page_table.shape[1] * PAGE_SIZE
    swa_states = (state_swa0, state_swa1, state_swa2)
    for (win, _), st in zip(SWA_GROUPS, swa_states):
        assert st.shape[2] == win, "sliding-window state must hold a full window"
    assert t % (SWA_GROUPS[-1][1] * SWA_BLK) == 0, "prompt length must split into full dilated blocks"
    rows_p, rows_s = bp * t, bs * s
    xp = x_prompt.reshape(rows_p, d)
    xs = x_sample.reshape(rows_s, d)

    mod = _ada_call(jnp.concatenate([c_prompt, c_sample], axis=0), ada_w, ada_b)

    def mods(i):
        out_p, out_s = [], []
        for j in range(6):
            m = mod[i, :, j * d:(j + 1) * d]
            out_p.append(m[:bp, None, :])
            out_s.append(m[bp:])
        return out_p, out_s

    def tok(m, tm):
        return _per_token(m, s, tm)

    tm_s512 = min(512, rows_s)
    tm_s1024 = min(1024, rows_s)
    conv_p = conv_s = pool_p = pool_s = mla_p = mla_s = None
    swa_p, swa_s = [None] * 3, [None] * 3

    for i in range(depth):
        kind, j = i % 4, i // 4
        (sh1p, sc1p, g1p, sh2p, sc2p, g2p), (sh1s, sc1s, g1s, sh2s, sc2s, g2s) = mods(i)
        n1 = norm1_g[i][None, :]
        proj_p = proj_s = None
        if kind == 0:
            w_in = conv_w_in[j].astype(BF16)
            w_out = conv_w_out[j].astype(BF16)
            bv, cu = _conv_call(xp, n1, sh1p, sc1p, w_in, conv_k[j], None, t)
            conv_p = cu.reshape(bp, t, d)[:, t - (CONV_WIDTH - 1):]
            proj_p = (bv, w_out, g1p)
            grp = CONV_GRP
            pad = jnp.zeros((bs, grp - s, d), F32)
            x_ext = jnp.concatenate([pad, xs.reshape(bs, s, d)], axis=1).reshape(bs * grp, d)
            ov = jnp.concatenate([jnp.zeros((bs, grp - s - 2, d), F32), state_conv[j], jnp.zeros((bs, s, d), F32)],
                                 axis=1).reshape(bs * grp, d)
            tme = min(512, bs * grp)
            she = _per_token(sh1s, grp, tme)
            sce = _per_token(sc1s, grp, tme)
            bv, cu = _conv_call(x_ext, n1, she, sce, w_in, conv_k[j], ov, grp)
            conv_s = cu.reshape(bs, grp, d)[:, grp - (CONV_WIDTH - 1):]
            bv = bv.reshape(bs, grp, d)[:, grp - s:].reshape(rows_s, d)
            proj_s = (bv, w_out, tok(g1s, tm_s1024))
        elif kind == 1:
            w_qkv = swa_w_qkv[j].astype(BF16)
            w_o = swa_w_o[j].astype(BF16)
            gn = jnp.concatenate([jnp.tile(swa_qn_g[j], (1, SWA_HEADS)), jnp.tile(swa_kn_g[j], (1, SWA_HEADS))],
                                 axis=0)
            seg = (jnp.arange(SWA_WIDTH)[:, None] // SWA_HEAD_DIM == jnp.arange(SWA_WIDTH)[None, :] // SWA_HEAD_DIM)
            seg = (seg.astype(F32) / SWA_HEAD_DIM).astype(BF16)
            dil = tuple(r for _, r in SWA_GROUPS)
            keeps = tuple(min(win, t) for win, _ in SWA_GROUPS)
            qkv = _swa_qkv_call(xp, n1, sh1p, sc1p, w_qkv, gn, seg, bp, t, dil, t, keeps)
            os_, ls_ = [], []
            for g, (win, r) in enumerate(SWA_GROUPS):
                o_g, l_g = _swa_band_call(qkv[g], qkv[3 + g], qkv[6 + g])
                os_.append(o_g)
                ls_.append(l_g)
                st = qkv[9 + g].reshape(bp, 2, SWA_HEADS, SWA_HEAD_DIM, keeps[g])
                swa_p[g] = jnp.transpose(st, (0, 4, 1, 2, 3))[None]
            xp = _swa_merge_call(os_, ls_, dil, w_o, xp, g1p, t, t)
            tm_q = min(512, rows_s)
            qkv_s = _swa_qkv_call(xs, n1, tok(sh1s, tm_q), tok(sc1s, tm_q), w_qkv, gn, seg, 1, rows_s,
                                  (1, 1, 1), s)
            os_, ls_ = [], []
            for g, (win, r) in enumerate(SWA_GROUPS):
                q_g, k_g, v_g = (qkv_s[c * 3 + g].reshape(bs, s, SWA_WIDTH) for c in range(3))
                st_t = jnp.transpose(swa_states[g][j], (0, 2, 3, 4, 1)).reshape(bs, 2, SWA_WIDTH, win)
                per_slab = 128 // s
                new_t = jnp.stack([k_g, v_g], axis=1).reshape(bs // per_slab, per_slab, 2, s, SWA_WIDTH)
                new_t = jnp.transpose(new_t, (0, 2, 4, 1, 3)).reshape(bs // per_slab, 2, SWA_WIDTH, 128)
                ns, o_g, l_g = _swa_step_call(q_g, st_t, new_t, win, r)
                swa_s[g] = jnp.transpose(ns.reshape(bs, 2, SWA_HEADS, SWA_HEAD_DIM, win), (0, 4, 1, 2, 3))[None]
                os_.append(o_g.reshape(1, rows_s, SWA_WIDTH))
                ls_.append(l_g.reshape(1, rows_s, SWA_WIDTH))
            xs = _swa_merge_call(os_, ls_, (1, 1, 1), w_o, xs, tok(g1s, tm_s512), rows_s, s)
        elif kind == 2:
            pw = pool_w[j].astype(BF16)
            ps = pool_scale[j][None, :]
            xp, tail = _pool_call(xp, n1, sh1p, sc1p, g1p, pw, ps, t, False, 0)
            tpt = t // min(512, rows_p)
            pool_p = tail.reshape(bp, tpt, 16, d)[:, -1, 1:]
            grp = POOL_GRP
            x_ext = jnp.concatenate([jnp.zeros((bs, 1, d), F32), state_pool[j], xs.reshape(bs, s, d),
                                     jnp.zeros((bs, grp - 16 - s, d), F32)], axis=1).reshape(bs * grp, d)
            tme = min(512, bs * grp)
            y_ext, ext = _pool_call(x_ext, n1, _per_token(sh1s, grp, tme), _per_token(sc1s, grp, tme),
                                    _per_token(g1s, grp, tme), pw, ps, grp, True, past)
            xs = y_ext.reshape(bs, grp, d)[:, 16:16 + s].reshape(rows_s, d)
            pool_s = ext.reshape(bs, grp, d)[:, 16 + s - POOL_HIST:16 + s]
        else:
            wts = _mla_weights(mla_w_qa[j], mla_g_qa[j], mla_w_qb[j], mla_w_kva[j], mla_g_kva[j], mla_w_kvb[j],
                               mla_g_qn[j], mla_g_kn[j], mla_g_qr[j], mla_g_kr[j])
            w_o = mla_w_o[j].astype(BF16)
            cos_p, sin_p = _rope_tables(jnp.arange(t))
            lat_p, qcat, kcat, vt = _mla_proj_call(xp, n1, sh1p, sc1p, wts, cos_p, sin_p, t, False)
            o_p = _mla_causal_call(qcat, kcat, vt, bp, t)
            proj_p = (o_p, w_o, g1p)
            mla_p = lat_p.reshape(bp, t, KV_LORA + QK_ROPE)
            cos_s, sin_s = _rope_tables(past + jnp.arange(s))
            reps = tm_s512 // s
            cos_s, sin_s = jnp.tile(cos_s, (reps, 1)), jnp.tile(sin_s, (reps, 1))
            lat_s, qabs, qpe = _mla_proj_call(xs, n1, tok(sh1s, tm_s512), tok(sc1s, tm_s512), wts, cos_s, sin_s, s, True)
            cache_t = jnp.transpose(cache_mla, (0, 1, 3, 2))
            o_s = _mla_decode_call(cache_t, j, page_table, qabs, qpe, lat_s, wts["wkt_flat"], wts["wv"], bs, s)
            proj_s = (o_s, w_o, tok(g1s, tm_s1024))
            mla_s = lat_s.reshape(bs, s, KV_LORA + QK_ROPE)
        n2 = norm2_g[i][None, :]
        w1 = mlp_w1[i].astype(BF16)
        w2 = mlp_w2[i].astype(BF16)
        xp = _mlp_call(xp, n2, sh2p, sc2p, g2p, w1, w2, t, proj_p)
        xs = _mlp_call(xs, n2, tok(sh2s, tm_s1024), tok(sc2s, tm_s1024), tok(g2s, tm_s1024), w1, w2, s, proj_s)

    return (xp.reshape(bp, t, d), xs.reshape(bs, s, d), conv_p[None], conv_s[None],
            swa_p[0], swa_s[0], swa_p[1], swa_s[1], swa_p[2], swa_s[2],
            pool_p[None], pool_s[None], mla_p[None], mla_s[None])
```

```python
import functools

import jax
import jax.numpy as jnp
from jax import lax
from jax.experimental import pallas as pl
from jax.experimental.pallas import tpu as pltpu

F32 = jnp.float32
BF16 = jnp.bfloat16

EPS = 1e-6
NEG = -1e30

CONV_WIDTH = 3
SWA_GROUPS = ((128, 1), (512, 4), (2048, 16))
SWA_HEADS = 8
SWA_HEAD_DIM = 64
SWA_WIDTH = SWA_HEADS * SWA_HEAD_DIM
SWA_BLK = 128
SWA_SCALE = SWA_HEAD_DIM ** -0.5
POOL_WINDOWS = (2, 4, 8, 16)
POOL_HIST = 15
MLA_HEADS = 8
Q_LORA = 384
KV_LORA = 256
QK_NOPE = 128
QK_ROPE = 64
V_DIM = 128
MLA_SCALE = (QK_NOPE + QK_ROPE) ** -0.5
LOG2_E = 1.4426950408889634
ROPE_BASE = 10000.0
PAGE_SIZE = 128

VMEM_LIMIT_BYTES = 56 * 1024 * 1024
CONV_GRP = 16
POOL_GRP = 32


def _params(*sem):
    return pltpu.CompilerParams(dimension_semantics=sem, vmem_limit_bytes=VMEM_LIMIT_BYTES)


def _rms_mod(x, g, shift, scale):
    y = x * lax.rsqrt(jnp.mean(x * x, axis=-1, keepdims=True) + EPS)
    return (y * g) * (1.0 + scale) + shift


def _dot(a, b):
    return jnp.dot(a, b, preferred_element_type=F32)


def _dot_nt(a, b):
    return lax.dot_general(a, b, (((1,), (1,)), ((), ())), preferred_element_type=F32)


def _mod_spec(m, tm, rows_per_group):
    d = m.shape[-1]
    if m.shape[1] == 1:
        tpg = rows_per_group // tm
        return pl.BlockSpec((1, 1, d), lambda i, *_: (i // tpg, 0, 0))
    return pl.BlockSpec((1, tm, d), lambda i, *_: (i, 0, 0))


def _per_token(m, reps, tm):
    e = jnp.repeat(m, reps, axis=0)
    return e.reshape(e.shape[0] // tm, tm, e.shape[1])


def _ada_kernel(c_ref, w_ref, b_ref, o_ref):
    c = c_ref[...]
    s = (c * (1.0 / (1.0 + jnp.exp(-c)))).astype(BF16)
    o_ref[0] = _dot(s, w_ref[0].astype(BF16)) + b_ref[0]


def _ada_call(c, ada_w, ada_b):
    depth, d, n6 = ada_w.shape
    n = c.shape[0]
    tn = 1536
    return pl.pallas_call(
        _ada_kernel,
        grid=(depth, n6 // tn),
        in_specs=[pl.BlockSpec((n, d), lambda i, j: (0, 0)),
                  pl.BlockSpec((1, d, tn), lambda i, j: (i, 0, j)),
                  pl.BlockSpec((1, 1, tn), lambda i, j: (i, 0, j))],
        out_specs=pl.BlockSpec((1, n, tn), lambda i, j: (i, 0, j)),
        out_shape=jax.ShapeDtypeStruct((depth, n, n6), F32),
        compiler_params=_params("parallel", "parallel"),
        name="adaln",
    )(c, ada_w, ada_b.reshape(depth, 1, n6))


def _mlp_kernel(*refs, mixer_proj):
    if mixer_proj:
        x_ref, a_ref, wm_ref, gt1_ref = refs[:4]
        refs = refs[:1] + refs[4:]
    x_ref, g_ref, sh_ref, sc_ref, gt_ref, w1_ref, w2_ref, o_ref, h_scr, acc_scr = refs[:10]
    j = pl.program_id(1)

    @pl.when(j == 0)
    def _():
        x = x_ref[...]
        if mixer_proj:
            x = x + gt1_ref[0] * _dot(a_ref[...], wm_ref[...])
            refs[10][...] = x
        h_scr[...] = _rms_mod(x, g_ref[...], sh_ref[0], sc_ref[0]).astype(BF16)
        acc_scr[...] = jnp.zeros_like(acc_scr)

    a = _dot(h_scr[...], w1_ref[...])
    a = jnp.square(jnp.maximum(a, 0.0)).astype(BF16)
    acc_scr[...] += _dot(a, w2_ref[...])

    @pl.when(j == pl.num_programs(1) - 1)
    def _():
        x1 = refs[10][...] if mixer_proj else x_ref[...]
        o_ref[...] = x1 + gt_ref[0] * acc_scr[...]


def _mlp_call(x, g, sh, sc, gt, w1, w2, rows_per_group, mixer_proj=None):
    rows, d = x.shape
    f = w1.shape[1]
    tm = min(1024, rows)
    tf = 1024
    ms = _mod_spec(sh, tm, rows_per_group)
    in_specs = [pl.BlockSpec((tm, d), lambda i, j: (i, 0))]
    args = [x]
    scratch = [pltpu.VMEM((tm, d), BF16), pltpu.VMEM((tm, d), F32)]
    if mixer_proj is not None:
        a, w_mix, gt1 = mixer_proj
        in_specs += [pl.BlockSpec((tm, a.shape[1]), lambda i, j: (i, 0)),
                     pl.BlockSpec(w_mix.shape, lambda i, j: (0, 0)),
                     ms]
        args += [a, w_mix, gt1]
        scratch.append(pltpu.VMEM((tm, d), F32))
    in_specs += [pl.BlockSpec((1, d), lambda i, j: (0, 0)),
                 ms, ms, ms,
                 pl.BlockSpec((d, tf), lambda i, j: (0, j)),
                 pl.BlockSpec((tf, d), lambda i, j: (j, 0))]
    return pl.pallas_call(
        functools.partial(_mlp_kernel, mixer_proj=mixer_proj is not None),
        grid=(rows // tm, f // tf),
        in_specs=in_specs,
        out_specs=pl.BlockSpec((tm, d), lambda i, j: (i, 0)),
        out_shape=jax.ShapeDtypeStruct((rows, d), F32),
        scratch_shapes=scratch,
        compiler_params=_params("parallel", "arbitrary"),
        name="mlp",
    )(*args, g, sh, sc, gt, w1, w2)


def _conv_kernel(*refs, tm, d, tiles_per_seq, has_override):
    if has_override:
        x_ref, g_ref, sh_ref, sc_ref, w_ref, k_ref, ov_ref, bv_ref, cu_ref, e_scr = refs
    else:
        x_ref, g_ref, sh_ref, sc_ref, w_ref, k_ref, bv_ref, cu_ref, e_scr = refs
    i = pl.program_id(0)
    if has_override:
        e_scr[0:8, :] = jnp.zeros((8, d), F32)
    else:
        @pl.when(i % tiles_per_seq == 0)
        def _():
            e_scr[0:8, :] = jnp.zeros((8, d), F32)

        @pl.when(i % tiles_per_seq != 0)
        def _():
            e_scr[0:8, :] = e_scr[tm:tm + 8, :]
    n_part = 2 if tm % 256 == 0 else 1
    hr = tm // n_part
    gates = []
    for part in range(n_part):
        rs = slice(part * hr, (part + 1) * hr)
        sh, sc = sh_ref[0], sc_ref[0]
        sh = sh if sh.shape[0] == 1 else sh[rs]
        sc = sc if sc.shape[0] == 1 else sc[rs]
        proj = _dot(_rms_mod(x_ref[rs, :], g_ref[...], sh, sc).astype(BF16), w_ref[...])
        cu = proj[:, d:2 * d] * proj[:, 2 * d:]
        if has_override:
            row = lax.broadcasted_iota(jnp.int32, (hr, 1), 0)
            cu = jnp.where((row % CONV_GRP) >= CONV_GRP // 2, cu, ov_ref[rs, :])
        e_scr[8 + part * hr:8 + (part + 1) * hr, :] = cu
        cu_ref[rs, :] = cu
        gates.append(proj[:, :d])
    k = k_ref[...]
    for part in range(n_part):
        lo = part * hr
        v = (k[0:1, :] * e_scr[6 + lo:6 + lo + hr, :] + k[1:2, :] * e_scr[7 + lo:7 + lo + hr, :]
             + k[2:3, :] * e_scr[8 + lo:8 + lo + hr, :])
        bv_ref[lo:lo + hr, :] = (gates[part] * v).astype(BF16)


def _conv_call(x, g, sh, sc, w_in, k, override, rows_per_group):
    rows, d = x.shape
    tm = min(512, rows)
    has_override = override is not None
    ms = _mod_spec(sh, tm, rows_per_group)
    in_specs = [pl.BlockSpec((tm, d), lambda i: (i, 0)),
                pl.BlockSpec((1, d), lambda i: (0, 0)),
                ms, ms,
                pl.BlockSpec((d, 3 * d), lambda i: (0, 0)),
                pl.BlockSpec((CONV_WIDTH, d), lambda i: (0, 0))]
    args = [x, g, sh, sc, w_in, k]
    if has_override:
        in_specs.append(pl.BlockSpec((tm, d), lambda i: (i, 0)))
        args.append(override)
    kern = functools.partial(_conv_kernel, tm=tm, d=d, tiles_per_seq=max(rows_per_group // tm, 1),
                             has_override=has_override)
    return pl.pallas_call(
        kern,
        grid=(rows // tm,),
        in_specs=in_specs,
        out_specs=[pl.BlockSpec((tm, d), lambda i: (i, 0)), pl.BlockSpec((tm, d), lambda i: (i, 0))],
        out_shape=[jax.ShapeDtypeStruct((rows, d), BF16), jax.ShapeDtypeStruct((rows, d), F32)],
        scratch_shapes=[pltpu.VMEM((tm + 8, d), F32)],
        compiler_params=_params("arbitrary"),
        name="conv_in",
    )(*args)


def _pool_kernel(*refs, tm, d, tiles_per_seq, sample, pos0):
    x_ref, g_ref, sh_ref, sc_ref, gt_ref, w_ref, ps_ref, o_ref, st_ref, e_scr = refs
    i = pl.program_id(0)
    x = x_ref[...]
    h = _rms_mod(x, g_ref[...], sh_ref[0], sc_ref[0])
    row = lax.broadcasted_iota(jnp.int32, (tm, 1), 0)
    if sample:
        local = (row % POOL_GRP) - POOL_GRP // 2
        h = jnp.where(local >= 0, h, x)
        e_scr[0:16, :] = jnp.zeros((16, d), F32)
    else:
        local = (i % tiles_per_seq) * tm + row

        @pl.when(i % tiles_per_seq == 0)
        def _():
            e_scr[0:16, :] = jnp.zeros((16, d), F32)

        @pl.when(i % tiles_per_seq != 0)
        def _():
            e_scr[0:16, :] = e_scr[tm:tm + 16, :]
    e_scr[16:tm + 16, :] = h
    if sample:
        st_ref[...] = h
    else:
        st_ref[0] = e_scr[tm:tm + 16, :]
    gq = d // len(POOL_WINDOWS)
    for gi, w in enumerate(POOL_WINDOWS):
        lo = gi * gq
        win = h[:, lo:lo + gq]
        for j in range(1, w):
            win = win + e_scr[16 - j:16 - j + tm, lo:lo + gq]
        cnt = jnp.clip(pos0 + local + 1, 1, w).astype(F32)
        dd = (win / cnt - h[:, lo:lo + gq]).astype(BF16)
        y = _dot(dd, w_ref[gi]) * ps_ref[:, lo:lo + gq]
        o_ref[:, lo:lo + gq] = x[:, lo:lo + gq] + gt_ref[0][:, lo:lo + gq] * y


def _pool_call(x, g, sh, sc, gt, w, ps, rows_per_group, sample, pos0):
    rows, d = x.shape
    tm = min(512, rows)
    ms = _mod_spec(sh, tm, rows_per_group)
    gq = d // len(POOL_WINDOWS)
    if sample:
        st_spec = pl.BlockSpec((tm, d), lambda i: (i, 0))
        st_shape = jax.ShapeDtypeStruct((rows, d), F32)
    else:
        st_spec = pl.BlockSpec((1, 16, d), lambda i: (i, 0, 0))
        st_shape = jax.ShapeDtypeStruct((rows // tm, 16, d), F32)
    kern = functools.partial(_pool_kernel, tm=tm, d=d, tiles_per_seq=max(rows_per_group // tm, 1),
                             sample=sample, pos0=pos0)
    return pl.pallas_call(
        kern,
        grid=(rows // tm,),
        in_specs=[pl.BlockSpec((tm, d), lambda i: (i, 0)),
                  pl.BlockSpec((1, d), lambda i: (0, 0)),
                  ms, ms, ms,
                  pl.BlockSpec((len(POOL_WINDOWS), gq, gq), lambda i: (0, 0, 0)),
                  pl.BlockSpec((1, d), lambda i: (0, 0))],
        out_specs=[pl.BlockSpec((tm, d), lambda i: (i, 0)), st_spec],
        out_shape=[jax.ShapeDtypeStruct((rows, d), F32), st_shape],
        scratch_shapes=[pltpu.VMEM((tm + 16, d), F32)],
        compiler_params=_params("arbitrary"),
        name="pool",
    )(x, g, sh, sc, gt, w, ps)


def _swa_qkv_kernel(*refs, tm, dil, keeps, tps):
    x_ref, g_ref, sh_ref, sc_ref, w_ref, gn_ref, seg_ref = refs[:7]
    outs = refs[7:16]
    states = refs[16:16 + len(keeps)]
    z_scr, h_scr = refs[16 + len(keeps):]
    w = SWA_WIDTH
    tile = pl.program_id(0) % tps
    h_scr[...] = _rms_mod(x_ref[...], g_ref[...], sh_ref[0], sc_ref[0]).astype(BF16)
    for jb in range(9):
        which, g = divmod(jb, 3)
        z = _dot(h_scr[...], w_ref[:, jb * w:(jb + 1) * w])
        if which < 2:
            ms = _dot((z * z).astype(BF16), seg_ref[...])
            z = z * lax.rsqrt(ms + EPS) * gn_ref[jb:jb + 1, :]
        if keeps and which > 0:
            keep = keeps[g]
            zk = z if keep >= tm else z[tm - keep:, :]

            def put(st_ref=states[g], kv=which - 1, zk=zk):
                st_ref[0, kv] = zk.T

            first_kept = tps - max(keep // tm, 1)
            if first_kept == 0:
                put()
            else:
                pl.when(tile >= first_kept)(put)
        r = dil[g]
        if r == 1:
            outs[jb][0, 0] = z
        else:
            for c in range(w // 128):
                z_scr[c] = z[:, c * 128:(c + 1) * 128]
            for rho in range(r):
                for c in range(w // 128):
                    outs[jb][0, rho, :, c * 128:(c + 1) * 128] = z_scr[c, pl.ds(rho, tm // r, stride=r), :]


def _swa_qkv_call(x, g, sh, sc, w, gn, seg, n_seq, seq_len, dil, rows_per_group, keeps=()):
    rows, d = x.shape
    wd = SWA_WIDTH
    tm = min(512, seq_len)
    tps = seq_len // tm
    ms = _mod_spec(sh, tm, rows_per_group)
    out_specs, out_shape = [], []
    for jb in range(9):
        r = dil[jb % 3]
        out_specs.append(pl.BlockSpec((1, r, tm // r, wd), lambda i: (i // tps, 0, i % tps, 0)))
        out_shape.append(jax.ShapeDtypeStruct((n_seq, r, seq_len // r, wd), F32))
    for keep in keeps:
        assert keep % tm == 0 or tm % keep == 0
        lanes = min(keep, tm)
        first_kept = tps - max(keep // tm, 1)
        out_specs.append(pl.BlockSpec(
            (1, 2, wd, lanes),
            lambda i, first_kept=first_kept: (i // tps, 0, 0, jnp.maximum(i % tps - first_kept, 0))))
        out_shape.append(jax.ShapeDtypeStruct((n_seq, 2, wd, keep), F32))
    return pl.pallas_call(
        functools.partial(_swa_qkv_kernel, tm=tm, dil=dil, keeps=tuple(keeps), tps=tps),
        grid=(rows // tm,),
        in_specs=[pl.BlockSpec((tm, d), lambda i: (i, 0)),
                  pl.BlockSpec((1, d), lambda i: (0, 0)),
                  ms, ms,
                  pl.BlockSpec(w.shape, lambda i: (0, 0), pipeline_mode=pl.Buffered(1)),
                  pl.BlockSpec(gn.shape, lambda i: (0, 0)),
                  pl.BlockSpec(seg.shape, lambda i: (0, 0))],
        out_specs=out_specs,
        out_shape=out_shape,
        scratch_shapes=[pltpu.VMEM((wd // 128, tm, 128), F32), pltpu.VMEM((tm, d), BF16)],
        compiler_params=_params("arbitrary"),
        name="swa_qkv",
    )(x, g, sh, sc, w, gn, seg)


def _swa_band_kernel(q_ref, kc_ref, kp_ref, vc_ref, vp_ref, o_ref, l_ref):
    b = pl.program_id(2)
    blk = SWA_BLK
    hd = SWA_HEAD_DIM
    key = lax.broadcasted_iota(jnp.int32, (2 * blk, blk), 0)
    qi = lax.broadcasted_iota(jnp.int32, (2 * blk, blk), 1)
    dist = qi + blk - key
    ok = jnp.logical_and(dist >= 0, dist <= jnp.where(b > 0, blk, qi))
    lane_head = lax.broadcasted_iota(jnp.int32, (blk, 2 * hd), 1) // hd
    for i in range(q_ref.shape[0]):
        q = q_ref[i, 0]
        k = jnp.concatenate([kp_ref[i, 0], kc_ref[i, 0]], axis=0).astype(BF16)
        vt = jnp.concatenate([vp_ref[i, 0], vc_ref[i, 0]], axis=0).T.astype(BF16)
        ots, lts = [], []
        for h in range(SWA_HEADS):
            slab = slice((h // 2) * 2 * hd, (h // 2 + 1) * 2 * hd)
            qm = jnp.where(lane_head == h % 2, q[:, slab], 0.0).astype(BF16)
            st = jnp.where(ok, _dot_nt(k[:, slab], qm) * SWA_SCALE, NEG)
            m = jnp.max(st, axis=0, keepdims=True)
            p = jnp.exp(st - m)
            l = jnp.sum(p, axis=0, keepdims=True)
            ots.append(_dot(vt[h * hd:(h + 1) * hd, :], p.astype(BF16)) * (1.0 / l))
            lts.append(jnp.broadcast_to(m + jnp.log(l), (hd, blk)))
        o_ref[i] = jnp.concatenate(ots, axis=0).T.astype(o_ref.dtype)
        l_ref[i] = jnp.concatenate(lts, axis=0).T


def _swa_band_call(q, k, v):
    n_seq, r, lr, w = q.shape
    nb = lr // SWA_BLK
    ns = next(c for c in (8, 4, 2, 1) if n_seq % c == 0)
    blk = (ns, 1, SWA_BLK, w)
    cur = pl.BlockSpec(blk, lambda n, rho, b: (n, rho, b, 0))
    prev = pl.BlockSpec(blk, lambda n, rho, b: (n, rho, jnp.maximum(b - 1, 0), 0))
    out_spec = pl.BlockSpec((ns, SWA_BLK, w), lambda n, rho, b: (n, b, rho))
    o, l = pl.pallas_call(
        _swa_band_kernel,
        grid=(n_seq // ns, r, nb),
        in_specs=[cur, cur, prev, cur, prev],
        out_specs=[out_spec, out_spec],
        out_shape=[jax.ShapeDtypeStruct((n_seq, lr, r * w), BF16), jax.ShapeDtypeStruct((n_seq, lr, r * w), F32)],
        compiler_params=_params("parallel", "parallel", "arbitrary"),
        name="swa_band",
    )(q, k, k, v, v)
    return o, l


def _swa_merge_kernel(*refs, dil, tm):
    o_refs, l_refs = refs[0:3], refs[3:6]
    w_ref, x_ref, gt_ref, out_ref = refs[6:10]
    scrs = list(refs[10:])
    wd = SWA_WIDTH

    def token_order(ref, r):
        if r == 1:
            return ref[0].astype(F32)
        scr = scrs.pop()
        for rho in range(r):
            for c in range(wd // 128):
                lo = rho * wd + c * 128
                scr[c, pl.ds(rho, tm // r, stride=r), :] = ref[0, :, lo:lo + 128].astype(F32)
        return jnp.concatenate([scr[c] for c in range(wd // 128)], axis=1)

    a = [token_order(l_refs[g], dil[g]) for g in range(3)]
    o = [token_order(o_refs[g], dil[g]) for g in range(3)]
    m = jnp.maximum(jnp.maximum(a[0], a[1]), a[2])
    e0, e1, e2 = jnp.exp(a[0] - m), jnp.exp(a[1] - m), jnp.exp(a[2] - m)
    inv = 1.0 / (e0 + e1 + e2)
    merged = (e0 * inv) * o[0] + (e1 * inv) * o[1] + (e2 * inv) * o[2]
    out_ref[...] = x_ref[...] + gt_ref[0] * _dot(merged.astype(BF16), w_ref[...])


def _swa_merge_call(os_, ls_, dil, w_o, x, gt, seq_len, rows_per_group):
    rows, d = x.shape
    tm = min(512, seq_len)
    tps = seq_len // tm
    w = SWA_WIDTH
    specs = [pl.BlockSpec((1, tm // r, r * w), lambda i: (i // tps, i % tps, 0)) for r in dil]
    n_scr = 2 * sum(1 for r in dil if r > 1)
    return pl.pallas_call(
        functools.partial(_swa_merge_kernel, dil=dil, tm=tm),
        grid=(rows // tm,),
        in_specs=specs + specs + [pl.BlockSpec((w, d), lambda i: (0, 0)),
                                  pl.BlockSpec((tm, d), lambda i: (i, 0)),
                                  _mod_spec(gt, tm, rows_per_group)],
        out_specs=pl.BlockSpec((tm, d), lambda i: (i, 0)),
        out_shape=jax.ShapeDtypeStruct((rows, d), F32),
        scratch_shapes=[pltpu.VMEM((w // 128, tm, 128), F32)] * n_scr,
        compiler_params=_params("parallel"),
        name="swa_merge",
    )(*os_, *ls_, w_o, x, gt)


def _swa_step_kernel(q_ref, st_ref, new_ref, ns_ref, o_ref, l_ref, *, win, r, lb, s_new, n_blk):
    w = SWA_WIDTH
    hd = SWA_HEAD_DIM
    nrow = SWA_HEADS * s_new
    pad = 128 - s_new
    rowi = lax.broadcasted_iota(jnp.int32, (nrow, w), 0)
    lanei = lax.broadcasted_iota(jnp.int32, (nrow, w), 1)
    diag = (rowi // s_new) == (lanei // hd)
    s_idx = lax.broadcasted_iota(jnp.int32, (nrow, lb + 128), 0) % s_new
    col = lax.broadcasted_iota(jnp.int32, (nrow, lb + 128), 1)
    dist = lb + s_idx - jnp.where(col < lb, col, col - pad)
    ok = jnp.logical_and(jnp.logical_and(dist >= 0, dist % r == 0), dist <= win)
    ok = jnp.logical_and(ok, jnp.logical_or(col < lb, col >= lb + pad))
    per_slab = 128 // s_new
    for i in range(n_blk):
        if n_blk % per_slab == 0:
            slab, shift = i // per_slab, pad - (i % per_slab) * s_new
        else:
            slab, shift = 0, pad - ((pl.program_id(0) * n_blk + i) % per_slab) * s_new
        new_k = pltpu.roll(new_ref[slab, 0], shift, axis=1)
        new_v = pltpu.roll(new_ref[slab, 1], shift, axis=1)
        q = q_ref[i]
        qbd = jnp.where(diag, jnp.concatenate([q] * SWA_HEADS, axis=0), 0.0).astype(BF16)
        kt = jnp.concatenate([st_ref[i, 0], new_k], axis=1).astype(BF16)
        vt = jnp.concatenate([st_ref[i, 1], new_v], axis=1).astype(BF16)
        sc = jnp.where(ok, _dot(qbd, kt) * SWA_SCALE, NEG)
        m = jnp.max(sc, axis=-1, keepdims=True)
        p = jnp.exp(sc - m)
        l = jnp.sum(p, axis=-1, keepdims=True)
        acc = jnp.where(diag, _dot_nt(p.astype(BF16), vt) / l, 0.0)
        lse = jnp.where(diag, m + jnp.log(l), 0.0)
        o = acc[0:s_new]
        ls = lse[0:s_new]
        for h in range(1, SWA_HEADS):
            o = o + acc[h * s_new:(h + 1) * s_new]
            ls = ls + lse[h * s_new:(h + 1) * s_new]
        o_ref[i] = o
        l_ref[i] = ls
        for kv, new in enumerate((new_k, new_v)):
            ns_ref[i, kv, :, 0:lb - s_new] = st_ref[i, kv, :, s_new:lb]
            ns_ref[i, kv, :, lb - s_new:lb] = new[:, pad:128]


def _swa_step_call(q, st_t, new_t, win, r):
    n_seq, s_new, w = q.shape
    lb = st_t.shape[-1]
    n_blk = max(1, min(n_seq, SWA_GROUPS[-1][0] // lb))
    per_slab = 128 // s_new
    assert n_seq % n_blk == 0 and (n_blk % per_slab == 0 or per_slab % n_blk == 0)
    if n_blk % per_slab == 0:
        new_spec = pl.BlockSpec((n_blk // per_slab, 2, w, 128), lambda n: (n, 0, 0, 0))
    else:
        new_spec = pl.BlockSpec((1, 2, w, 128), lambda n: ((n * n_blk) // per_slab, 0, 0, 0))
    return pl.pallas_call(
        functools.partial(_swa_step_kernel, win=win, r=r, lb=lb, s_new=s_new, n_blk=n_blk),
        grid=(n_seq // n_blk,),
        in_specs=[pl.BlockSpec((n_blk, s_new, w), lambda n: (n, 0, 0)),
                  pl.BlockSpec((n_blk, 2, w, lb), lambda n: (n, 0, 0, 0)),
                  new_spec],
        out_specs=[pl.BlockSpec((n_blk, 2, w, lb), lambda n: (n, 0, 0, 0)),
                   pl.BlockSpec((n_blk, s_new, w), lambda n: (n, 0, 0)),
                   pl.BlockSpec((n_blk, s_new, w), lambda n: (n, 0, 0))],
        out_shape=[jax.ShapeDtypeStruct((n_seq, 2, w, lb), F32),
                   jax.ShapeDtypeStruct((n_seq, s_new, w), F32),
                   jax.ShapeDtypeStruct((n_seq, s_new, w), F32)],
        compiler_params=_params("parallel"),
        name="swa_step",
    )(q, st_t, new_t)


def _rope128(bn, cos_t, sin_t):
    return bn * cos_t + pltpu.roll(bn, 64, axis=1) * sin_t


def _mla_proj_kernel(*refs, decode):
    (x_ref, g_ref, sh_ref, sc_ref, w1_ref, gqa_ref, wqb_ref, gq_ref, gkv_ref, gkr_ref,
     cos_ref, sin_ref, wx_ref) = refs[:13]
    if decode:
        outs = refs[13:]
    else:
        wvt_ref = refs[13]
        outs = refs[14:]
    nh = MLA_HEADS
    if decode:
        lat_ref, qabs_ref, qpe_ref = outs
    else:
        lat_ref, qcat_ref, kcat_ref, vt_ref = outs
    gq = gq_ref[...]
    tm = x_ref.shape[0]
    n_part = 2 if tm % 256 == 0 else 1

    def rows_of(ref, rs):
        v = ref[0]
        return v if v.shape[0] == 1 else v[rs]

    for part in range(n_part):
        rs = slice(part * (tm // n_part), (part + 1) * (tm // n_part))
        h = _rms_mod(x_ref[rs, :], g_ref[...], rows_of(sh_ref, rs), rows_of(sc_ref, rs)).astype(BF16)
        a = _dot(h, w1_ref[...])
        qa = a[:, :Q_LORA]
        qa = (qa * lax.rsqrt(jnp.mean(qa * qa, axis=-1, keepdims=True) + EPS) * gqa_ref[...]).astype(BF16)
        q = _dot(qa, wqb_ref[...])
        cos_t = cos_ref[rs, :]
        sin_t = sin_ref[rs, :]
        latc = a[:, Q_LORA:Q_LORA + KV_LORA]
        latc = latc * lax.rsqrt(jnp.mean(latc * latc, axis=-1, keepdims=True) + EPS) * gkv_ref[...]
        kb = a[:, Q_LORA + KV_LORA:]
        kb = kb * lax.rsqrt(jnp.sum(kb * kb, axis=-1, keepdims=True) * (1.0 / QK_ROPE) + EPS) * gkr_ref[...]
        kb = _rope128(kb, cos_t, sin_t)
        lane = lax.broadcasted_iota(jnp.int32, kb.shape, 1)
        kstd = jnp.where(lane < 32, kb, pltpu.roll(kb, 96, axis=1))
        lat_ref[rs, 0:KV_LORA] = latc
        lat_ref[rs, KV_LORA:KV_LORA + QK_ROPE] = kstd[:, 0:QK_ROPE]
        for hh in range(nh):
            an = q[:, hh * 256:hh * 256 + 128]
            an = an * lax.rsqrt(jnp.mean(an * an, axis=-1, keepdims=True) + EPS) * gq[:, 0:128]
            bn = q[:, hh * 256 + 128:hh * 256 + 256]
            bn = bn * lax.rsqrt(jnp.sum(bn * bn, axis=-1, keepdims=True) * (1.0 / QK_ROPE) + EPS) * gq[:, 128:256]
            bn = _rope128(bn, cos_t, sin_t)
            if decode:
                ag = an * gq[:, 256:384]
                a_hi = ag.astype(BF16)
                a_lo = (ag - a_hi.astype(F32)).astype(BF16)
                qabs_ref[rs, hh * 256:(hh + 1) * 256] = (_dot(a_hi, wx_ref[hh]) + _dot(a_lo, wx_ref[hh])).astype(BF16)
                qpe_ref[rs, hh * 128:(hh + 1) * 128] = jnp.where(lane < 32, bn, pltpu.roll(bn, 96, axis=1)).astype(BF16)
            else:
                qcat_ref[rs, hh * 256:hh * 256 + 128] = an.astype(BF16)
                qcat_ref[rs, hh * 256 + 128:(hh + 1) * 256] = bn.astype(BF16)
        if not decode:
            gkn = gq_ref[:, 256:384]
            latb = latc.astype(BF16)
            kvx = _dot(latb, wx_ref[...])
            kbb = kb.astype(BF16)
            for hh in range(nh):
                kn = kvx[:, hh * 128:(hh + 1) * 128]
                kn = kn * lax.rsqrt(jnp.mean(kn * kn, axis=-1, keepdims=True) + EPS) * gkn
                kcat_ref[rs, hh * 256:hh * 256 + 128] = kn.astype(BF16)
                kcat_ref[rs, hh * 256 + 128:(hh + 1) * 256] = kbb
            vt_ref[0, :, rs] = _dot_nt(wvt_ref[...], latb).astype(BF16)


def _mla_proj_call(x, g, sh, sc, wts, cos_t, sin_t, rows_per_group, decode):
    rows, d = x.shape
    tm = min(512, rows)
    nh = MLA_HEADS
    ms = _mod_spec(sh, tm, rows_per_group)
    tps = max(rows_per_group // tm, 1)
    if decode:
        tab_spec = pl.BlockSpec((tm, 128), lambda i: (0, 0))
        wx = wts["wk_t"]
        wx_spec = pl.BlockSpec(wx.shape, lambda i: (0, 0, 0))
        out_specs = [pl.BlockSpec((tm, KV_LORA + QK_ROPE), lambda i: (i, 0)),
                     pl.BlockSpec((tm, nh * 256), lambda i: (i, 0)),
                     pl.BlockSpec((tm, nh * 128), lambda i: (i, 0))]
        out_shape = [jax.ShapeDtypeStruct((rows, KV_LORA + QK_ROPE), F32),
                     jax.ShapeDtypeStruct((rows, nh * 256), BF16),
                     jax.ShapeDtypeStruct((rows, nh * 128), BF16)]
    else:
        tab_spec = pl.BlockSpec((tm, 128), lambda i: (i % tps, 0))
        wx = wts["w_kn"]
        wx_spec = pl.BlockSpec(wx.shape, lambda i: (0, 0))
        out_specs = [pl.BlockSpec((tm, KV_LORA + QK_ROPE), lambda i: (i, 0)),
                     pl.BlockSpec((tm, nh * 256), lambda i: (i, 0)),
                     pl.BlockSpec((tm, nh * 256), lambda i: (i, 0)),
                     pl.BlockSpec((1, nh * V_DIM, tm), lambda i: (i // tps, 0, i % tps))]
        out_shape = [jax.ShapeDtypeStruct((rows, KV_LORA + QK_ROPE), F32),
                     jax.ShapeDtypeStruct((rows, nh * 256), BF16),
                     jax.ShapeDtypeStruct((rows, nh * 256), BF16),
                     jax.ShapeDtypeStruct((rows // rows_per_group, nh * V_DIM, rows_per_group), BF16)]

    def full(a):
        return pl.BlockSpec(a.shape, lambda i: (0,) * a.ndim)

    extra = [] if decode else [wts["wv_t"]]
    return pl.pallas_call(
        functools.partial(_mla_proj_kernel, decode=decode),
        grid=(rows // tm,),
        in_specs=[pl.BlockSpec((tm, d), lambda i: (i, 0)),
                  pl.BlockSpec((1, d), lambda i: (0, 0)),
                  ms, ms,
                  full(wts["w1"]), full(wts["g_qa"]), full(wts["w_qb"]), full(wts["gq"]),
                  full(wts["g_kva"]), full(wts["g_kr"]),
                  tab_spec, tab_spec, wx_spec] + [full(a) for a in extra],
        out_specs=out_specs,
        out_shape=out_shape,
        compiler_params=_params("parallel"),
        name="mla_proj_decode" if decode else "mla_proj",
    )(x, g, sh, sc, wts["w1"], wts["g_qa"], wts["w_qb"], wts["gq"], wts["g_kva"], wts["g_kr"],
      cos_t, sin_t, wx, *extra)


def _mla_causal_kernel(q_ref, k_ref, vt_ref, o_ref, *, seq_len, tq, heads):
    r = lax.broadcasted_iota(jnp.int32, (tq, tq), 0)
    c = lax.broadcasted_iota(jnp.int32, (tq, tq), 1)
    ex = MLA_SCALE * LOG2_E
    for qb in range(seq_len // tq):
        lo = qb * tq
        for h in range(heads):
            hq = slice(h * 256, (h + 1) * 256)
            hv = slice(h * V_DIM, (h + 1) * V_DIM)
            q = q_ref[0, lo:lo + tq, hq]
            s_d = jnp.where(r <= c, _dot_nt(k_ref[0, lo:lo + tq, hq], q), NEG)
            m = jnp.max(s_d, axis=0, keepdims=True)
            if qb > 0:
                s_p = _dot_nt(k_ref[0, 0:lo, hq], q)
                m = jnp.maximum(m, jnp.max(s_p, axis=0, keepdims=True))
            p_d = jnp.exp2((s_d - m) * ex)
            l = jnp.sum(p_d, axis=0, keepdims=True)
            ot = _dot(vt_ref[0, hv, lo:lo + tq], p_d.astype(BF16))
            if qb > 0:
                p_p = jnp.exp2((s_p - m) * ex)
                l = l + jnp.sum(p_p, axis=0, keepdims=True)
                ot = ot + _dot(vt_ref[0, hv, 0:lo], p_p.astype(BF16))
            o_ref[0, lo:lo + tq, hv] = (ot * (1.0 / l)).T.astype(BF16)


def _mla_causal_call(qcat, kcat, vt, n_seq, seq_len):
    nh = MLA_HEADS
    tq = min(512, seq_len)
    hps = 4
    q3 = qcat.reshape(n_seq, seq_len, nh * 256)
    k3 = kcat.reshape(n_seq, seq_len, nh * 256)
    o = pl.pallas_call(
        functools.partial(_mla_causal_kernel, seq_len=seq_len, tq=tq, heads=hps),
        grid=(n_seq, nh // hps),
        in_specs=[pl.BlockSpec((1, seq_len, hps * 256), lambda n, h: (n, 0, h)),
                  pl.BlockSpec((1, seq_len, hps * 256), lambda n, h: (n, 0, h)),
                  pl.BlockSpec((1, hps * V_DIM, seq_len), lambda n, h: (n, h, 0))],
        out_specs=pl.BlockSpec((1, seq_len, hps * V_DIM), lambda n, h: (n, 0, h)),
        out_shape=jax.ShapeDtypeStruct((n_seq, seq_len, nh * V_DIM), BF16),
        compiler_params=_params("parallel", "parallel"),
        name="mla_causal",
    )(q3, k3, vt)
    return o.reshape(n_seq * seq_len, nh * V_DIM)


def _mla_decode_kernel(pt_ref, *refs, n_pg, s_new):
    del pt_ref
    page_refs = refs[:n_pg]
    (qabs_ref, qpe_ref, latn_ref, wkt_ref, wv_ref, o_ref, m_scr, l_scr, ctx_scr) = refs[n_pg:]
    c = pl.program_id(1)
    nh = MLA_HEADS
    nrow = nh * s_new

    @pl.when(c == 0)
    def _():
        m_scr[...] = jnp.full_like(m_scr, NEG)
        l_scr[...] = jnp.zeros_like(l_scr)
        ctx_scr[...] = jnp.zeros_like(ctx_scr)

    lhs = jnp.concatenate([wkt_ref[...], qabs_ref[0]], axis=0)

    def process(pages, mask):
        latc = jnp.concatenate([pg[0:KV_LORA, :].astype(BF16) for pg in pages], axis=1)
        kpe = jnp.concatenate([pg[KV_LORA:KV_LORA + QK_ROPE, :].astype(BF16) for pg in pages], axis=1)
        r = _dot(lhs, latc)
        rq = r[nh * QK_NOPE:]
        spe = _dot(qpe_ref[0][:, 0:QK_ROPE], kpe)
        parts = []
        for h in range(nh):
            blk = r[h * QK_NOPE:(h + 1) * QK_NOPE]
            rstd = lax.rsqrt(jnp.sum(blk * blk, axis=0, keepdims=True) * (1.0 / QK_NOPE) + EPS)
            parts.append(rq[h * s_new:(h + 1) * s_new] * rstd)
        s = (jnp.concatenate(parts, axis=0) + spe) * MLA_SCALE
        if mask is not None:
            s = jnp.where(mask, s, NEG)
        m_old = m_scr[...]
        m_new = jnp.maximum(m_old, jnp.max(s, axis=-1, keepdims=True))
        alpha = jnp.exp(m_old - m_new)
        p = jnp.exp(s - m_new)
        l_scr[...] = alpha * l_scr[...] + jnp.sum(p, axis=-1, keepdims=True)
        ctx_scr[...] = alpha * ctx_scr[...] + _dot_nt(p.astype(BF16), latc)
        m_scr[...] = m_new

    for lo in range(0, n_pg, 16):
        process([pr[0, 0] for pr in page_refs[lo:lo + 16]], None)

    @pl.when(c == pl.num_programs(1) - 1)
    def _():
        si = lax.broadcasted_iota(jnp.int32, (nrow, PAGE_SIZE), 0) % s_new
        kj = lax.broadcasted_iota(jnp.int32, (nrow, PAGE_SIZE), 1)
        process([latn_ref[0]], kj <= si)
        ctx = ctx_scr[...] / l_scr[...]
        for h in range(nh):
            ctx_h = ctx[h * s_new:(h + 1) * s_new].astype(BF16)
            o_ref[0, :, h * V_DIM:(h + 1) * V_DIM] = _dot(ctx_h, wv_ref[h]).astype(BF16)


def _mla_decode_call(cache_t, layer, page_table, qabs, qpe, lat_new, wkt, wv, n_seq, s_new):
    nh = MLA_HEADS
    n_pages = page_table.shape[1]
    n_pg = min(64, n_pages)
    lw = KV_LORA + QK_ROPE
    nrow = nh * s_new

    def page_spec(p):
        return pl.BlockSpec((1, 1, lw, PAGE_SIZE),
                            lambda n, c, pt: (layer, pt[n * n_pages + c * n_pg + p], 0, 0))

    latn_t = jnp.transpose(lat_new.reshape(n_seq, s_new, lw), (0, 2, 1))
    latn_t = jnp.pad(latn_t, ((0, 0), (0, 0), (0, PAGE_SIZE - s_new)))
    grid_spec = pltpu.PrefetchScalarGridSpec(
        num_scalar_prefetch=1,
        grid=(n_seq, n_pages // n_pg),
        in_specs=[page_spec(p) for p in range(n_pg)] + [
            pl.BlockSpec((1, nrow, 256), lambda n, c, pt: (n, 0, 0)),
            pl.BlockSpec((1, nrow, 128), lambda n, c, pt: (n, 0, 0)),
            pl.BlockSpec((1, lw, PAGE_SIZE), lambda n, c, pt: (n, 0, 0)),
            pl.BlockSpec(wkt.shape, lambda n, c, pt: (0, 0)),
            pl.BlockSpec(wv.shape, lambda n, c, pt: (0, 0, 0))],
        out_specs=pl.BlockSpec((1, s_new, nh * V_DIM), lambda n, c, pt: (n, 0, 0)),
        scratch_shapes=[pltpu.VMEM((nrow, 1), F32), pltpu.VMEM((nrow, 1), F32),
                        pltpu.VMEM((nrow, KV_LORA), F32)])
    o = pl.pallas_call(
        functools.partial(_mla_decode_kernel, n_pg=n_pg, s_new=s_new),
        grid_spec=grid_spec,
        out_shape=jax.ShapeDtypeStruct((n_seq, s_new, nh * V_DIM), BF16),
        compiler_params=_params("parallel", "arbitrary"),
        name="mla_decode",
    )(page_table.reshape(-1), *([cache_t] * n_pg),
      jnp.transpose(qabs.reshape(n_seq, s_new, nh, 256), (0, 2, 1, 3)).reshape(n_seq, nrow, 256),
      jnp.transpose(qpe.reshape(n_seq, s_new, nh, 128), (0, 2, 1, 3)).reshape(n_seq, nrow, 128),
      latn_t, wkt, wv)
    return o.reshape(n_seq * s_new, nh * V_DIM)


def _mla_weights(w_qa, g_qa, w_qb, w_kva, g_kva, w_kvb, g_qn, g_kn, g_qr, g_kr):
    nh = MLA_HEADS
    half = QK_ROPE // 2

    def pad_rope_cols(w):
        z = jnp.zeros(w.shape[:-1] + (half,), w.dtype)
        return jnp.concatenate([w[..., :half], z, w[..., half:], z], axis=-1)

    w_kva_p = jnp.concatenate([w_kva[:, :KV_LORA], pad_rope_cols(w_kva[:, KV_LORA:])], axis=1)
    w1 = jnp.concatenate([w_qa, w_kva_p], axis=1).astype(BF16)
    wq = w_qb.reshape(Q_LORA, nh, QK_NOPE + QK_ROPE)
    wq = jnp.concatenate([wq[..., :QK_NOPE], pad_rope_cols(wq[..., QK_NOPE:])], axis=-1)
    w_qb_p = wq.reshape(Q_LORA, nh * 256).astype(BF16)
    gq = jnp.concatenate([g_qn, pad_rope_cols(g_qr), g_kn])[None, :]
    wkv = w_kvb.reshape(KV_LORA, nh, QK_NOPE + V_DIM)
    w_kn = wkv[..., :QK_NOPE].reshape(KV_LORA, nh * QK_NOPE).astype(BF16)
    wv_t = jnp.transpose(wkv[..., QK_NOPE:], (1, 2, 0)).reshape(nh * V_DIM, KV_LORA).astype(BF16)
    wk_t = jnp.transpose(wkv[..., :QK_NOPE], (1, 2, 0)).astype(BF16)
    return dict(w1=w1, g_qa=g_qa[None, :], w_qb=w_qb_p, gq=gq, g_kva=g_kva[None, :],
                g_kr=pad_rope_cols(g_kr)[None, :], w_kn=w_kn, wv_t=wv_t, wk_t=wk_t,
                wkt_flat=wk_t.reshape(nh * QK_NOPE, KV_LORA),
                wv=jnp.transpose(wkv[..., QK_NOPE:], (1, 0, 2)).astype(BF16))


def _rope_tables(pos):
    half = QK_ROPE // 2
    inv = ROPE_BASE ** (-jnp.arange(half, dtype=F32) / half)
    ang = pos.astype(F32)[:, None] * inv[None, :]
    cos, sin = jnp.cos(ang), jnp.sin(ang)
    z = jnp.zeros_like(cos)
    return (jnp.concatenate([cos, z, cos, z], axis=1), jnp.concatenate([-sin, z, sin, z], axis=1))


def kernel(x_prompt, x_sample, state_conv, state_swa0, state_swa1, state_swa2, state_pool, cache_mla, page_table,
           c_prompt, c_sample, ada_w, ada_b, norm1_g, norm2_g, mlp_w1, mlp_w2, conv_w_in, conv_k, conv_w_out,
           swa_w_qkv, swa_qn_g, swa_kn_g, swa_w_o, pool_w, pool_scale, mla_w_qa, mla_g_qa, mla_w_qb, mla_w_kva,
           mla_g_kva, mla_w_kvb, mla_g_qn, mla_g_kn, mla_g_qr, mla_g_kr, mla_w_o):
    bp, t, d = x_prompt.shape
    bs, s, _ = x_sample.shape
    depth = ada_w.shape[0]
    past = page_table.shape[1] * PAGE_SIZE
    swa_states = (state_swa0, state_swa1, state_swa2)
    for (win, _), st in zip(SWA_GROUPS, swa_states):
        assert st.shape[2] == win, "sliding-window state must hold a full window"
    assert t % (SWA_GROUPS[-1][1] * SWA_BLK) == 0, "prompt length must split into full dilated blocks"
    rows_p, rows_s = bp * t, bs * s
    xp = x_prompt.reshape(rows_p, d)
    xs = x_sample.reshape(rows_s, d)

    mod = _ada_call(jnp.concatenate([c_prompt, c_sample], axis=0), ada_w, ada_b)

    def mods(i):
        out_p, out_s = [], []
        for j in range(6):
            m = mod[i, :, j * d:(j + 1) * d]
            out_p.append(m[:bp, None, :])
            out_s.append(m[bp:])
        return out_p, out_s

    def tok(m, tm):
        return _per_token(m, s, tm)

    tm_s512 = min(512, rows_s)
    tm_s1024 = min(1024, rows_s)
    conv_p = conv_s = pool_p = pool_s = mla_p = mla_s = None
    swa_p, swa_s = [None] * 3, [None] * 3

    for i in range(depth):
        kind, j = i % 4, i // 4
        (sh1p, sc1p, g1p, sh2p, sc2p, g2p), (sh1s, sc1s, g1s, sh2s, sc2s, g2s) = mods(i)
        n1 = norm1_g[i][None, :]
        proj_p = proj_s = None
        if kind == 0:
            w_in = conv_w_in[j].astype(BF16)
            w_out = conv_w_out[j].astype(BF16)
            bv, cu = _conv_call(xp, n1, sh1p, sc1p, w_in, conv_k[j], None, t)
            conv_p = cu.reshape(bp, t, d)[:, t - (CONV_WIDTH - 1):]
            proj_p = (bv, w_out, g1p)
            grp = CONV_GRP
            pad = jnp.zeros((bs, grp - s, d), F32)
            x_ext = jnp.concatenate([pad, xs.reshape(bs, s, d)], axis=1).reshape(bs * grp, d)
            ov = jnp.concatenate([jnp.zeros((bs, grp - s - 2, d), F32), state_conv[j], jnp.zeros((bs, s, d), F32)],
                                 axis=1).reshape(bs * grp, d)
            tme = min(512, bs * grp)
            she = _per_token(sh1s, grp, tme)
            sce = _per_token(sc1s, grp, tme)
            bv, cu = _conv_call(x_ext, n1, she, sce, w_in, conv_k[j], ov, grp)
            conv_s = cu.reshape(bs, grp, d)[:, grp - (CONV_WIDTH - 1):]
            bv = bv.reshape(bs, grp, d)[:, grp - s:].reshape(rows_s, d)
            proj_s = (bv, w_out, tok(g1s, tm_s1024))
        elif kind == 1:
            w_qkv = swa_w_qkv[j].astype(BF16)
            w_o = swa_w_o[j].astype(BF16)
            gn = jnp.concatenate([jnp.tile(swa_qn_g[j], (1, SWA_HEADS)), jnp.tile(swa_kn_g[j], (1, SWA_HEADS))],
                                 axis=0)
            seg = (jnp.arange(SWA_WIDTH)[:, None] // SWA_HEAD_DIM == jnp.arange(SWA_WIDTH)[None, :] // SWA_HEAD_DIM)
            seg = (seg.astype(F32) / SWA_HEAD_DIM).astype(BF16)
            dil = tuple(r for _, r in SWA_GROUPS)
            keeps = tuple(min(win, t) for win, _ in SWA_GROUPS)
            qkv = _swa_qkv_call(xp, n1, sh1p, sc1p, w_qkv, gn, seg, bp, t, dil, t, keeps)
            os_, ls_ = [], []
            for g, (win, r) in enumerate(SWA_GROUPS):
                o_g, l_g = _swa_band_call(qkv[g], qkv[3 + g], qkv[6 + g])
                os_.append(o_g)
                ls_.append(l_g)
                st = qkv[9 + g].reshape(bp, 2, SWA_HEADS, SWA_HEAD_DIM, keeps[g])
                swa_p[g] = jnp.transpose(st, (0, 4, 1, 2, 3))[None]
            xp = _swa_merge_call(os_, ls_, dil, w_o, xp, g1p, t, t)
            tm_q = min(512, rows_s)
            qkv_s = _swa_qkv_call(xs, n1, tok(sh1s, tm_q), tok(sc1s, tm_q), w_qkv, gn, seg, 1, rows_s,
                                  (1, 1, 1), s)
            os_, ls_ = [], []
            for g, (win, r) in enumerate(SWA_GROUPS):
                q_g, k_g, v_g = (qkv_s[c * 3 + g].reshape(bs, s, SWA_WIDTH) for c in range(3))
                st_t = jnp.transpose(swa_states[g][j], (0, 2, 3, 4, 1)).reshape(bs, 2, SWA_WIDTH, win)
                per_slab = 128 // s
                new_t = jnp.stack([k_g, v_g], axis=1).reshape(bs // per_slab, per_slab, 2, s, SWA_WIDTH)
                new_t = jnp.transpose(new_t, (0, 2, 4, 1, 3)).reshape(bs // per_slab, 2, SWA_WIDTH, 128)
                ns, o_g, l_g = _swa_step_call(q_g, st_t, new_t, win, r)
                swa_s[g] = jnp.transpose(ns.reshape(bs, 2, SWA_HEADS, SWA_HEAD_DIM, win), (0, 4, 1, 2, 3))[None]
                os_.append(o_g.reshape(1, rows_s, SWA_WIDTH))
                ls_.append(l_g.reshape(1, rows_s, SWA_WIDTH))
            xs = _swa_merge_call(os_, ls_, (1, 1, 1), w_o, xs, tok(g1s, tm_s512), rows_s, s)
        elif kind == 2:
            pw = pool_w[j].astype(BF16)
            ps = pool_scale[j][None, :]
            xp, tail = _pool_call(xp, n1, sh1p, sc1p, g1p, pw, ps, t, False, 0)
            tpt = t // min(512, rows_p)
            pool_p = tail.reshape(bp, tpt, 16, d)[:, -1, 1:]
            grp = POOL_GRP
            x_ext = jnp.concatenate([jnp.zeros((bs, 1, d), F32), state_pool[j], xs.reshape(bs, s, d),
                                     jnp.zeros((bs, grp - 16 - s, d), F32)], axis=1).reshape(bs * grp, d)
            tme = min(512, bs * grp)
            y_ext, ext = _pool_call(x_ext, n1, _per_token(sh1s, grp, tme), _per_token(sc1s, grp, tme),
                                    _per_token(g1s, grp, tme), pw, ps, grp, True, past)
            xs = y_ext.reshape(bs, grp, d)[:, 16:16 + s].reshape(rows_s, d)
            pool_s = ext.reshape(bs, grp, d)[:, 16 + s - POOL_HIST:16 + s]
        else:
            wts = _mla_weights(mla_w_qa[j], mla_g_qa[j], mla_w_qb[j], mla_w_kva[j], mla_g_kva[j], mla_w_kvb[j],
                               mla_g_qn[j], mla_g_kn[j], mla_g_qr[j], mla_g_kr[j])
            w_o = mla_w_o[j].astype(BF16)
            cos_p, sin_p = _rope_tables(jnp.arange(t))
            lat_p, qcat, kcat, vt = _mla_proj_call(xp, n1, sh1p, sc1p, wts, cos_p, sin_p, t, False)
            o_p = _mla_causal_call(qcat, kcat, vt, bp, t)
            proj_p = (o_p, w_o, g1p)
            mla_p = lat_p.reshape(bp, t, KV_LORA + QK_ROPE)
            cos_s, sin_s = _rope_tables(past + jnp.arange(s))
            reps = tm_s512 // s
            cos_s, sin_s = jnp.tile(cos_s, (reps, 1)), jnp.tile(sin_s, (reps, 1))
            lat_s, qabs, qpe = _mla_proj_call(xs, n1, tok(sh1s, tm_s512), tok(sc1s, tm_s512), wts, cos_s, sin_s, s, True)
            cache_t = jnp.transpose(cache_mla, (0, 1, 3, 2))
            o_s = _mla_decode_call(cache_t, j, page_table, qabs, qpe, lat_s, wts["wkt_flat"], wts["wv"], bs, s)
            proj_s = (o_s, w_o, tok(g1s, tm_s1024))
            mla_s = lat_s.reshape(bs, s, KV_LORA + QK_ROPE)
        n2 = norm2_g[i][None, :]
        w1 = mlp_w1[i].astype(BF16)
        w2 = mlp_w2[i].astype(BF16)
        xp = _mlp_call(xp, n2, sh2p, sc2p, g2p, w1, w2, t, proj_p)
        xs = _mlp_call(xs, n2, tok(sh2s, tm_s1024), tok(sc2s, tm_s1024), tok(g2s, tm_s1024), w1, w2, s, proj_s)

    return (xp.reshape(bp, t, d), xs.reshape(bs, s, d), conv_p[None], conv_s[None],
            swa_p[0], swa_s[0], swa_p[1], swa_s[1], swa_p[2], swa_s[2],
            pool_p[None], pool_s[None], mla_p[None], mla_s[None])
```
